```python
import math
import jax
import jax.numpy as jnp
from jax import lax
import numpy as np

D_MODEL = 1024
BATCH = 4
SEQ = 8192
DEPTH = 1

F32 = jnp.float32
HEAD_DIM = 128
GDN_HEADS = 4
GDN_WIDTH = GDN_HEADS * HEAD_DIM
ATTN_Q_HEADS = 4
ATTN_KV_HEADS = 2
ATTN_WIDTH = ATTN_Q_HEADS * HEAD_DIM
ATTN_KV_WIDTH = ATTN_KV_HEADS * HEAD_DIM
MIX_WIDTH = GDN_WIDTH + ATTN_WIDTH
IN_SPLITS = (GDN_WIDTH, GDN_WIDTH, GDN_WIDTH, GDN_WIDTH, 4 * GDN_HEADS, ATTN_WIDTH, ATTN_KV_WIDTH, ATTN_KV_WIDTH)
IN_COLS = sum(IN_SPLITS)
CONV_K = 5
CHUNK = 64
Q_BLOCK = 128
GRID_W = 64
ROPE_THETA = 10000.0
N_GROUPS = 4
EXPERTS_PER_GROUP = 8
N_EXPERTS = N_GROUPS * EXPERTS_PER_GROUP
TOP_K_IN_GROUP = 2
D_EXPERT = 256
MOE_BLOCK = 128
ALPHA = (2.0 * DEPTH) ** 0.25
BETA = (8.0 * DEPTH) ** -0.25
LN_EPS = 1e-5
RMS_EPS = 1e-6

kernel_name = 'hybrid_gdn_axialgqa_hmoe_block'


def layer_norm(x, gain=None, bias=None):
    xf = x.astype(F32)
    mu = jnp.mean(xf, axis=-1, keepdims=True)
    var = jnp.mean(jnp.square(xf - mu), axis=-1, keepdims=True)
    y = (xf - mu) * lax.rsqrt(var + LN_EPS)
    if gain is not None:
        y = y * gain.astype(F32) + bias.astype(F32)
    return y.astype(x.dtype)


def rms_norm(x, gain):
    xf = x.astype(F32)
    y = xf * lax.rsqrt(jnp.mean(jnp.square(xf), axis=-1, keepdims=True) + RMS_EPS)
    return (y * gain.astype(F32)).astype(x.dtype)


def l2_normalize(x):
    xf = x.astype(F32)
    return (xf * lax.rsqrt(jnp.sum(jnp.square(xf), axis=-1, keepdims=True) + RMS_EPS)).astype(x.dtype)


def centred_short_conv(x, w):
    pad = CONV_K // 2
    y = lax.conv_general_dilated(x, w[:, None, :].astype(x.dtype), window_strides=(1,), padding=[(pad, pad)],
                                 dimension_numbers=('NWC', 'WIO', 'NWC'), feature_group_count=x.shape[-1])
    return jax.nn.silu(y)


def chunk_gated_delta_rule(q, k, v, g, beta):
    b, h, s, dk = q.shape
    dv = v.shape[-1]
    n = s // CHUNK
    out_dtype = v.dtype
    q, k, v = [t.astype(F32).reshape(b, h, n, CHUNK, t.shape[-1]) for t in (q, k, v)]
    g = jnp.cumsum(g.astype(F32).reshape(b, h, n, CHUNK), axis=-1)
    beta = beta.astype(F32).reshape(b, h, n, CHUNK)
    idx = jnp.arange(CHUNK)
    lower_incl = idx[:, None] >= idx[None, :]
    strict = idx[:, None] > idx[None, :]
    decay = jnp.exp(jnp.where(lower_incl, g[..., :, None] - g[..., None, :], -jnp.inf))
    kk = jnp.einsum('bhncd,bhnsd->bhncs', k, k)
    m = jnp.where(strict, beta[..., :, None] * kk * decay, 0.0)
    eye = jnp.eye(CHUNK, dtype=F32)
    rhs = jnp.concatenate([v * beta[..., None], k * (beta * jnp.exp(g))[..., None]], axis=-1)
    sol = lax.linalg.triangular_solve(eye + m, rhs, left_side=True, lower=True, unit_diagonal=True)
    u, w = sol[..., :dv], sol[..., dv:]
    qk = jnp.einsum('bhncd,bhnsd->bhncs', q, k) * decay

    def step(state, xs):
        q_c, k_c, u_c, w_c, g_c, qk_c = xs
        v_new = u_c - jnp.einsum('bhck,bhkv->bhcv', w_c, state)
        o = jnp.einsum('bhck,bhkv->bhcv', q_c * jnp.exp(g_c)[..., None], state) + jnp.einsum('bhcs,bhsv->bhcv', qk_c, v_new)
        g_last = g_c[..., -1]
        state = state * jnp.exp(g_last)[..., None, None] + jnp.einsum(
            'bhck,bhcv->bhkv', k_c * jnp.exp(g_last[..., None] - g_c)[..., None], v_new)
        return state, o

    xs = tuple(jnp.moveaxis(t, 2, 0) for t in (q, k, u, w, g, qk))
    _, o = lax.scan(step, jnp.zeros((b, h, dk, dv), F32), xs)
    return jnp.moveaxis(o, 0, 2).reshape(b, h, s, dv).astype(out_dtype)


def gdn_mixer(q, k, v, z, ab, conv_w, a_log, dt_bias, norm_g):
    b, s, _ = q.shape
    qkv = centred_short_conv(jnp.concatenate([q, k, v], axis=-1), conv_w)
    q, k, v = jnp.split(qkv, 3, axis=-1)
    to_heads = lambda t: t.reshape(b, s, GDN_HEADS, HEAD_DIM).transpose(0, 2, 1, 3)
    q = l2_normalize(to_heads(q)) * (HEAD_DIM ** -0.5)
    k = l2_normalize(to_heads(k))
    v = to_heads(v)
    a_f, a_b, b_f, b_b = jnp.split(ab.astype(F32), 4, axis=-1)

    def decay_and_beta(a, bb, d):
        g = -jnp.exp(a_log[d].astype(F32)) * jax.nn.softplus(a + dt_bias[d].astype(F32))
        return g.transpose(0, 2, 1), jax.nn.sigmoid(bb).transpose(0, 2, 1)

    g_f, beta_f = decay_and_beta(a_f, b_f, 0)
    g_b, beta_b = decay_and_beta(a_b, b_b, 1)
    rev = lambda t: jnp.flip(t, axis=2)
    o_f = chunk_gated_delta_rule(q, k, v, g_f, beta_f)
    o_b = rev(chunk_gated_delta_rule(rev(q), rev(k), rev(v), rev(g_b), rev(beta_b)))
    o = (o_f + o_b).transpose(0, 2, 1, 3)
    o = rms_norm(o, norm_g) * jax.nn.silu(z.reshape(b, s, GDN_HEADS, HEAD_DIM))
    return o.reshape(b, s, GDN_WIDTH)


def rope_1d(x, pos):
    d = x.shape[-1]
    inv = ROPE_THETA ** (-jnp.arange(0, d, 2, dtype=F32) / d)
    ang = pos.astype(F32)[:, None] * inv[None, :]
    cos = jnp.cos(ang)[None, :, None, :]
    sin = jnp.sin(ang)[None, :, None, :]
    xf = x.astype(F32)
    x1, x2 = xf[..., : d // 2], xf[..., d // 2:]
    return jnp.concatenate([x1 * cos - x2 * sin, x2 * cos + x1 * sin], axis=-1).astype(x.dtype)


def axial_rope(x, row, col):
    half = x.shape[-1] // 2
    return jnp.concatenate([rope_1d(x[..., :half], row), rope_1d(x[..., half:], col)], axis=-1)


def blocked_gqa(q, k, v):
    b, s = q.shape[:2]
    n_blk = s // Q_BLOCK
    grp = ATTN_Q_HEADS // ATTN_KV_HEADS
    qb = q.reshape(b, n_blk, Q_BLOCK, ATTN_KV_HEADS, grp, HEAD_DIM).transpose(1, 0, 3, 4, 2, 5)
    kt = k.transpose(0, 2, 1, 3)
    vt = v.transpose(0, 2, 1, 3)
    scale = HEAD_DIM ** -0.5

    def one_block(q_blk):
        logits = jnp.einsum('bkgqd,bksd->bkgqs', q_blk, kt).astype(F32) * scale
        p = jax.nn.softmax(logits, axis=-1).astype(vt.dtype)
        return jnp.einsum('bkgqs,bksd->bkgqd', p, vt)

    o = lax.map(one_block, qb)
    return o.transpose(1, 0, 4, 2, 3, 5).reshape(b, s, ATTN_WIDTH)


def attention_mixer(q, k, v, q_norm_g, k_norm_g, out_norm_g):
    b, s, _ = q.shape
    q = rms_norm(q.reshape(b, s, ATTN_Q_HEADS, HEAD_DIM), q_norm_g)
    k = rms_norm(k.reshape(b, s, ATTN_KV_HEADS, HEAD_DIM), k_norm_g)
    v = v.reshape(b, s, ATTN_KV_HEADS, HEAD_DIM)
    n_rows = s // GRID_W
    row = jnp.repeat(jnp.arange(n_rows), GRID_W)
    col = jnp.tile(jnp.arange(GRID_W), n_rows)
    q = axial_rope(q, row, col)
    k = axial_rope(k, row, col)
    return rms_norm(blocked_gqa(q, k, v), out_norm_g)


def hierarchical_moe(h, w_group, b_group, w_router, b_router, w1, w3, w2):
    b, s, d = h.shape
    t = h.reshape(-1, d)
    n_tok = t.shape[0]
    grp_logits = (t @ w_group).astype(F32) + b_group.astype(F32)
    grp_prob = jax.nn.softmax(grp_logits, axis=-1)
    grp_idx = jnp.argmax(grp_logits, axis=-1)
    grp_p = jnp.take_along_axis(grp_prob, grp_idx[:, None], axis=-1)[:, 0]
    exp_logits = ((t @ w_router).astype(F32) + b_router.astype(F32)).reshape(n_tok, N_GROUPS, EXPERTS_PER_GROUP)
    in_grp = jnp.take_along_axis(exp_logits, grp_idx[:, None, None], axis=1)[:, 0]
    top_val, top_local = lax.top_k(in_grp, TOP_K_IN_GROUP)
    weights = grp_p[:, None] * jax.nn.softmax(top_val, axis=-1)
    expert_idx = grp_idx[:, None] * EXPERTS_PER_GROUP + top_local

    n_assign = n_tok * TOP_K_IN_GROUP
    flat_e = expert_idx.reshape(-1)
    flat_w = weights.reshape(-1)
    flat_tok = jnp.repeat(jnp.arange(n_tok), TOP_K_IN_GROUP)
    order = jnp.argsort(flat_e)
    e_sorted = flat_e[order]
    counts = jnp.bincount(flat_e, length=N_EXPERTS)
    padded = (counts + MOE_BLOCK - 1) // MOE_BLOCK * MOE_BLOCK
    pad_end = jnp.cumsum(padded)
    pad_start = pad_end - padded
    start = jnp.cumsum(counts) - counts
    dest = pad_start[e_sorted] + jnp.arange(n_assign) - start[e_sorted]
    n_blocks = -(-n_assign // MOE_BLOCK) + N_EXPERTS
    n_rows = n_blocks * MOE_BLOCK
    row_tok = jnp.zeros((n_rows,), jnp.int32).at[dest].set(flat_tok[order])
    row_w = jnp.zeros((n_rows,), F32).at[dest].set(flat_w[order])
    blk_expert = jnp.minimum(jnp.searchsorted(pad_end, jnp.arange(n_blocks) * MOE_BLOCK, side='right'), N_EXPERTS - 1)
    xs = t[row_tok].reshape(n_blocks, MOE_BLOCK, d)

    def expert_block(args):
        xb, e = args
        hid = jax.nn.silu(xb @ w1[e]) * (xb @ w3[e])
        return hid @ w2[e]

    ys = lax.map(expert_block, (xs, blk_expert)).reshape(n_rows, d)
    ys = ys * row_w[:, None].astype(ys.dtype)
    out = jax.ops.segment_sum(ys, row_tok, num_segments=n_tok)
    return out.reshape(b, s, d)


def setup_inputs(seed: int = 0) -> dict:
    key = jax.random.key(seed)
    ks = jax.random.split(key, 24)
    L = DEPTH
    nrm = lambda k, shape, scale: jax.random.normal(k, shape, F32) * scale
    gain = lambda k, shape: 1.0 + 0.02 * jax.random.normal(k, shape, F32)
    dt = jnp.exp(jax.random.uniform(ks[6], (L, 2, GDN_HEADS), F32, math.log(1e-3), math.log(1e-1)))
    return {
        'x': nrm(ks[0], (BATCH, SEQ, D_MODEL), 1.0),
        'c': nrm(ks[1], (BATCH, D_MODEL), 1.0),
        'w_ada': nrm(ks[2], (L, D_MODEL, 6 * D_MODEL), 0.5 * D_MODEL ** -0.5),
        'b_ada': nrm(ks[3], (L, 6 * D_MODEL), 0.02),
        'w_in': nrm(ks[4], (L, D_MODEL, IN_COLS), D_MODEL ** -0.5),
        'conv_w': nrm(ks[5], (L, CONV_K, 3 * GDN_WIDTH), CONV_K ** -0.5),
        'a_log': jnp.log(jax.random.uniform(ks[7], (L, 2, GDN_HEADS), F32, 1.0, 16.0)),
        'dt_bias': dt + jnp.log(-jnp.expm1(-dt)),
        'gdn_norm_g': gain(ks[8], (L, HEAD_DIM)),
        'q_norm_g': gain(ks[9], (L, HEAD_DIM)),
        'k_norm_g': gain(ks[10], (L, HEAD_DIM)),
        'attn_norm_g': gain(ks[11], (L, ATTN_WIDTH)),
        'w_out': nrm(ks[12], (L, MIX_WIDTH, D_MODEL), BETA * MIX_WIDTH ** -0.5),
        'ln1_g': gain(ks[13], (L, D_MODEL)),
        'ln1_b': nrm(ks[14], (L, D_MODEL), 0.02),
        'w_group': nrm(ks[15], (L, D_MODEL, N_GROUPS), D_MODEL ** -0.5),
        'b_group': nrm(ks[16], (L, N_GROUPS), 0.01),
        'w_router': nrm(ks[17], (L, D_MODEL, N_EXPERTS), D_MODEL ** -0.5),
        'b_router': nrm(ks[18], (L, N_EXPERTS), 0.01),
        'w1': nrm(ks[19], (L, N_EXPERTS, D_MODEL, D_EXPERT), D_MODEL ** -0.5),
        'w3': nrm(ks[20], (L, N_EXPERTS, D_MODEL, D_EXPERT), D_MODEL ** -0.5),
        'w2': nrm(ks[21], (L, N_EXPERTS, D_EXPERT, D_MODEL), BETA * D_EXPERT ** -0.5),
        'ln2_g': gain(ks[22], (L, D_MODEL)),
        'ln2_b': nrm(ks[23], (L, D_MODEL), 0.02),
    }


def reference(x, c, w_ada, b_ada, w_in, conv_w, a_log, dt_bias, gdn_norm_g, q_norm_g, k_norm_g, attn_norm_g,
              w_out, ln1_g, ln1_b, w_group, b_group, w_router, b_router, w1, w3, w2, ln2_g, ln2_b):
    split_points = np.cumsum(IN_SPLITS)[:-1].tolist()
    for layer in range(DEPTH):
        mod = jax.nn.silu(c) @ w_ada[layer] + b_ada[layer]
        sh1, sc1, gt1, sh2, sc2, gt2 = jnp.split(mod[:, None, :], 6, axis=-1)
        h = layer_norm(x) * (1.0 + sc1) + sh1
        proj = h @ w_in[layer]
        g_q, g_k, g_v, g_z, g_ab, a_q, a_k, a_v = jnp.split(proj, split_points, axis=-1)
        gdn_out = gdn_mixer(g_q, g_k, g_v, g_z, g_ab, conv_w[layer], a_log[layer], dt_bias[layer], gdn_norm_g[layer])
        attn_out = attention_mixer(a_q, a_k, a_v, q_norm_g[layer], k_norm_g[layer], attn_norm_g[layer])
        mixed = jnp.concatenate([gdn_out, attn_out], axis=-1) @ w_out[layer]
        x = layer_norm(ALPHA * x + gt1 * mixed, ln1_g[layer], ln1_b[layer])
        h = layer_norm(x) * (1.0 + sc2) + sh2
        ffn = hierarchical_moe(h, w_group[layer], b_group[layer], w_router[layer], b_router[layer],
                               w1[layer], w3[layer], w2[layer])
        x = layer_norm(ALPHA * x + gt2 * ffn, ln2_g[layer], ln2_b[layer])
    return x
```

```python
import functools
import math

import jax
import jax.numpy as jnp
from jax import lax
from jax.experimental import pallas as pl
from jax.experimental.pallas import tpu as pltpu

F32 = jnp.float32
BF16 = jnp.bfloat16
I32 = jnp.int32

HEAD_DIM = 128
GDN_HEADS = 4
GDN_WIDTH = GDN_HEADS * HEAD_DIM
ATTN_Q_HEADS = 4
ATTN_KV_HEADS = 2
ATTN_GROUP = ATTN_Q_HEADS // ATTN_KV_HEADS
ATTN_WIDTH = ATTN_Q_HEADS * HEAD_DIM
ATTN_KV_WIDTH = ATTN_KV_HEADS * HEAD_DIM
CONV_K = 5
CHUNK = 64
GRID_W = 64
ROPE_THETA = 10000.0
N_GROUPS = 4
EXPERTS_PER_GROUP = 8
N_EXPERTS = N_GROUPS * EXPERTS_PER_GROUP
D_EXPERT = 256
DEPTH = 1
ALPHA = (2.0 * DEPTH) ** 0.25
LN_EPS = 1e-5
RMS_EPS = 1e-6

LANES = 128
VMEM_LIMIT_BYTES = 56 * 1024 * 1024
NEG_BIG = -1e30


def _cparams(semantics):
    return pltpu.CompilerParams(dimension_semantics=semantics, vmem_limit_bytes=VMEM_LIMIT_BYTES)


def _dot(a, b):
    return jnp.dot(a, b, preferred_element_type=F32)


def _dot_nt(a, b):
    return lax.dot_general(a, b, (((1,), (1,)), ((), ())), preferred_element_type=F32)


def _dot_tn(a, b):
    return lax.dot_general(a, b, (((0,), (0,)), ((), ())), preferred_element_type=F32)


def _split3(a):
    hi = a.astype(BF16)
    r = a - hi.astype(F32)
    mid = r.astype(BF16)
    lo = (r - mid.astype(F32)).astype(BF16)
    return hi, mid, lo


def _dot_f32_lhs_exact(a_bf16_exact, b):
    hi, mid, lo = _split3(b)
    return _dot(a_bf16_exact, hi) + _dot(a_bf16_exact, mid) + _dot(a_bf16_exact, lo)


def _dot_f32_rhs_exact(a, b_bf16_exact):
    hi, mid, lo = _split3(a)
    return _dot(hi, b_bf16_exact) + _dot(mid, b_bf16_exact) + _dot(lo, b_bf16_exact)


def _dot_f32(a, b):
    ah, am, _ = _split3(a)
    bh, bm, _ = _split3(b)
    return _dot(ah, bh) + (_dot(ah, bm) + _dot(am, bh))


def _sigmoid(x):
    return 1.0 / (1.0 + jnp.exp(-x))


def _silu(x):
    return x * _sigmoid(x)


def _softplus(x):
    return jnp.maximum(x, 0.0) + jnp.log1p(jnp.exp(-jnp.abs(x)))


def _layer_norm(x):
    mu = jnp.mean(x, axis=-1, keepdims=True)
    xc = x - mu
    var = jnp.mean(xc * xc, axis=-1, keepdims=True)
    return xc * lax.rsqrt(var + LN_EPS)


def _mod_kernel(c_ref, w_ref, b_ref, o_ref):
    o_ref[...] = _dot_f32(_silu(c_ref[...]), w_ref[...]) + b_ref[...]


def _adaln_mod(c_pad, w_ada, b_ada, tn):
    rows, d = c_pad.shape
    n = w_ada.shape[1]
    return pl.pallas_call(
        _mod_kernel,
        grid=(n // tn,),
        in_specs=[pl.BlockSpec((rows, d), lambda j: (0, 0)),
                  pl.BlockSpec((d, tn), lambda j: (0, j)),
                  pl.BlockSpec((1, tn), lambda j: (0, j))],
        out_specs=pl.BlockSpec((rows, tn), lambda j: (0, j)),
        out_shape=jax.ShapeDtypeStruct((rows, n), F32),
        compiler_params=_cparams(("arbitrary",)),
        name="adaln_mod",
    )(c_pad, w_ada, b_ada)


_W_GQKV = 0
_W_Z = 3 * GDN_WIDTH
_W_AQ = _W_Z + GDN_WIDTH
_W_AK = _W_AQ + ATTN_WIDTH
_W_AV = _W_AK + ATTN_KV_WIDTH
_W_AB = _W_AV + ATTN_KV_WIDTH
_W_COLS = _W_AB + LANES
N_GATES = 4 * GDN_HEADS


def _rope(xh, cos, sin_signed, lane):
    fwd = pltpu.roll(xh, 32, 1)
    bwd = pltpu.roll(xh, LANES - 32, 1)
    partner = jnp.where((lane % 64) < 32, bwd, fwd)
    return xh * cos + partner * sin_signed


def _inproj_kernel(x_ref, mod_ref, w_ref, wabt_ref, cos_ref, sin_ref, qg_ref, kg_ref,
                   gqkv_ref, z_ref, ab_ref, abt_ref, aq_ref, ak_ref, av_ref):
    x = x_ref[0]
    sh1 = mod_ref[0, 0:1, :]
    sc1 = mod_ref[0, 1:2, :]
    h = _layer_norm(x) * (1.0 + sc1) + sh1
    hb = h.astype(BF16)
    gqkv_ref[0] = _dot(hb, w_ref[:, _W_GQKV:_W_Z]).astype(BF16)
    z_ref[0] = _dot(hb, w_ref[:, _W_Z:_W_AQ]).astype(BF16)
    ab_ref[0] = _dot(hb, w_ref[:, _W_AB:_W_COLS])[:, 0:N_GATES]
    abt_ref[0] = _dot_nt(wabt_ref[...], hb)
    att = _dot(hb, w_ref[:, _W_AQ:_W_AB])
    cos = cos_ref[...]
    sin = sin_ref[...]
    lane = lax.broadcasted_iota(I32, cos.shape, 1)
    q_scale = HEAD_DIM ** -0.5
    for i in range(ATTN_Q_HEADS + ATTN_KV_HEADS):
        xh = att[:, i * HEAD_DIM:(i + 1) * HEAD_DIM]
        gain = qg_ref[...] if i < ATTN_Q_HEADS else kg_ref[...]
        xn = xh * lax.rsqrt(jnp.mean(xh * xh, axis=-1, keepdims=True) + RMS_EPS) * gain
        xr = _rope(xn, cos, sin, lane)
        if i < ATTN_Q_HEADS:
            aq_ref[0, :, i * HEAD_DIM:(i + 1) * HEAD_DIM] = (xr * q_scale).astype(BF16)
        else:
            j = i - ATTN_Q_HEADS
            ak_ref[0, :, j * HEAD_DIM:(j + 1) * HEAD_DIM] = xr.astype(BF16)
    av_ref[0] = att[:, ATTN_WIDTH + ATTN_KV_WIDTH:].astype(BF16)


def _in_projection(x, mod3, w_packed, wab_t, cos_t, sin_t, qg, kg, tm):
    b, s, d = x.shape
    grid = (b, s // tm)
    row = lambda w: pl.BlockSpec((1, tm, w), lambda bi, i: (bi, i, 0))
    const = lambda shape: pl.BlockSpec(shape, lambda bi, i: (0,) * len(shape))
    out_shapes = [
        jax.ShapeDtypeStruct((b, s, 3 * GDN_WIDTH), BF16),
        jax.ShapeDtypeStruct((b, s, GDN_WIDTH), BF16),
        jax.ShapeDtypeStruct((b, s, N_GATES), F32),
        jax.ShapeDtypeStruct((b, N_GATES, s), F32),
        jax.ShapeDtypeStruct((b, s, ATTN_WIDTH), BF16),
        jax.ShapeDtypeStruct((b, s, ATTN_KV_WIDTH), BF16),
        jax.ShapeDtypeStruct((b, s, ATTN_KV_WIDTH), BF16),
    ]
    out_specs = [row(3 * GDN_WIDTH), row(GDN_WIDTH), row(N_GATES),
                 pl.BlockSpec((1, N_GATES, tm), lambda bi, i: (bi, 0, i)),
                 row(ATTN_WIDTH), row(ATTN_KV_WIDTH), row(ATTN_KV_WIDTH)]
    return pl.pallas_call(
        _inproj_kernel,
        grid=grid,
        in_specs=[row(d),
                  pl.BlockSpec((1, 6, d), lambda bi, i: (bi, 0, 0)),
                  const((d, _W_COLS)),
                  const((N_GATES, d)),
                  pl.BlockSpec((tm, HEAD_DIM), lambda bi, i: (i, 0)),
                  pl.BlockSpec((tm, HEAD_DIM), lambda bi, i: (i, 0)),
                  const((1, HEAD_DIM)),
                  const((1, HEAD_DIM))],
        out_specs=out_specs,
        out_shape=out_shapes,
        compiler_params=_cparams(("arbitrary", "arbitrary")),
        name="in_projection",
    )(x, mod3, w_packed, wab_t, cos_t, sin_t, qg, kg)


_HALO = 16


def _chunk_masks(n, transpose=False):
    r = lax.broadcasted_iota(I32, (n, n), 0)
    c = lax.broadcasted_iota(I32, (n, n), 1)
    same = (r // CHUNK) == (c // CHUNK)
    lower = same & (r >= c)
    upper = same & (r <= c)
    return lower, upper


def _gdn_prep_kernel(prev_ref, main_ref, next_ref, convw_ref, ab_ref, abt_ref,
                     alog_c_ref, dtb_c_ref, alog_r_ref, dtb_r_ref,
                     q_ref, k_ref, v_ref, gb_ref, egc_ref, egl_ref, ea_ref, gcr_ref, buf_ref):
    i = pl.program_id(1)
    n_i = pl.num_programs(1)
    tc = main_ref.shape[1]
    prev_scale = jnp.where(i > 0, 1.0, 0.0).astype(F32)
    next_scale = jnp.where(i < n_i - 1, 1.0, 0.0).astype(F32)
    buf_ref[0:_HALO, :] = prev_ref[0].astype(F32) * prev_scale
    buf_ref[_HALO:_HALO + tc, :] = main_ref[0].astype(F32)
    buf_ref[_HALO + tc:, :] = next_ref[0].astype(F32) * next_scale
    pad = CONV_K // 2
    for part, out_ref in enumerate((q_ref, k_ref, v_ref)):
        for hh in range(GDN_HEADS):
            c0 = part * GDN_WIDTH + hh * HEAD_DIM
            acc = None
            for j in range(CONV_K):
                term = buf_ref[_HALO - pad + j:_HALO - pad + j + tc, c0:c0 + HEAD_DIM] * convw_ref[j:j + 1, c0:c0 + HEAD_DIM]
                acc = term if acc is None else acc + term
            y = _silu(acc)
            if part < 2:
                y = y * lax.rsqrt(jnp.sum(y * y, axis=-1, keepdims=True) + RMS_EPS)
            if part == 0:
                y = y * (HEAD_DIM ** -0.5)
            out_ref[0, :, hh * HEAD_DIM:(hh + 1) * HEAD_DIM] = y.astype(BF16)

    lower, upper = _chunk_masks(tc)
    lower_b = jnp.where(lower, 1.0, 0.0).astype(BF16)
    upper_b = jnp.where(upper, 1.0, 0.0).astype(BF16)
    n_dir = 2 * GDN_HEADS
    ab = ab_ref[0]
    lane = lax.broadcasted_iota(I32, ab.shape, 1)
    g = -jnp.exp(alog_c_ref[...]) * _softplus(ab + dtb_c_ref[...])
    pre = _dot_f32_lhs_exact(lower_b, g)
    suf = _dot_f32_lhs_exact(upper_b, g)
    gtot = pre + suf - g
    gc = jnp.where(lane < GDN_HEADS, pre, suf)
    gb_ref[0] = jnp.where(lane < n_dir, gc, _sigmoid(ab))
    egc_ref[0] = jnp.exp(gc)
    egl_ref[0] = jnp.exp(gtot - gc)
    ea_ref[0] = jnp.exp(gtot)
    abt = abt_ref[0]
    row = lax.broadcasted_iota(I32, abt.shape, 0)
    g_r = -jnp.exp(alog_r_ref[...]) * _softplus(abt + dtb_r_ref[...])
    pre_r = _dot_f32_rhs_exact(g_r, upper_b)
    suf_r = _dot_f32_rhs_exact(g_r, lower_b)
    gcr_ref[0] = jnp.where(row < GDN_HEADS, pre_r, suf_r)[0:n_dir]


def _gdn_prep(gqkv, conv_w, ab, abt, alog_c, dtb_c, alog_r, dtb_r, tc):
    b, s, w = gqkv.shape
    nh = tc // _HALO
    n_dir = 2 * GDN_HEADS
    grid = (b, s // tc)
    row = lambda width: pl.BlockSpec((1, tc, width), lambda bi, i: (bi, i, 0))
    const = lambda shape: pl.BlockSpec(shape, lambda bi, i: (0,) * len(shape))
    last_halo = s // _HALO - 1
    out_shapes = ([jax.ShapeDtypeStruct((b, s, GDN_WIDTH), BF16)] * 3
                  + [jax.ShapeDtypeStruct((b, s, N_GATES), F32)] * 4
                  + [jax.ShapeDtypeStruct((b, n_dir, s), F32)])
    out_specs = ([row(GDN_WIDTH)] * 3 + [row(N_GATES)] * 4
                 + [pl.BlockSpec((1, n_dir, tc), lambda bi, i: (bi, 0, i))])
    return pl.pallas_call(
        _gdn_prep_kernel,
        grid=grid,
        in_specs=[pl.BlockSpec((1, _HALO, w), lambda bi, i: (bi, jnp.maximum(i * nh - 1, 0), 0)),
                  row(w),
                  pl.BlockSpec((1, _HALO, w), lambda bi, i: (bi, jnp.minimum((i + 1) * nh, last_halo), 0)),
                  const((CONV_K, w)),
                  row(N_GATES),
                  pl.BlockSpec((1, N_GATES, tc), lambda bi, i: (bi, 0, i)),
                  const((1, N_GATES)), const((1, N_GATES)),
                  const((N_GATES, 1)), const((N_GATES, 1))],
        out_specs=out_specs,
        out_shape=out_shapes,
        scratch_shapes=[pltpu.VMEM((tc + 2 * _HALO, w), F32)],
        compiler_params=_cparams(("arbitrary", "arbitrary")),
        name="gdn_prep",
    )(gqkv, gqkv, gqkv, conv_w, ab, abt, alog_c, dtb_c, alog_r, dtb_r)


def _gdn_local_kernel(q_ref, k_ref, v_ref, gb_ref, egc_ref, egl_ref, ea_ref, gcr_ref,
                      a_ref, b_ref, dg_ref, qe_ref, oi_ref):
    tc = q_ref.shape[1]
    nch = tc // CHUNK
    lower, upper = _chunk_masks(tc)
    r = lax.broadcasted_iota(I32, (tc, tc), 0)
    c = lax.broadcasted_iota(I32, (tc, tc), 1)
    eye = r == c
    blk_xor = r ^ c
    n_dir = 2 * GDN_HEADS
    for d in range(2):
        incl = lower if d == 0 else upper
        strict = incl & jnp.logical_not(eye)
        for hh in range(GDN_HEADS):
            idx = d * GDN_HEADS + hh
            hs = slice(hh * HEAD_DIM, (hh + 1) * HEAD_DIM)
            qc = q_ref[0, :, hs]
            kc = k_ref[0, :, hs]
            vc = v_ref[0, :, hs]
            gc = gb_ref[0, :, idx:idx + 1]
            beta = gb_ref[0, :, n_dir + idx:n_dir + idx + 1]
            egc = egc_ref[0, :, idx:idx + 1]
            egl = egl_ref[0, :, idx:idx + 1]
            ea = ea_ref[0, :, idx:idx + 1]
            gcr = gcr_ref[0, idx:idx + 1, :]
            decay = jnp.where(incl, jnp.exp(jnp.minimum(gc - gcr, 0.0)), 0.0)
            kk = _dot_nt(kc, kc)
            qk = _dot_nt(qc, kc)
            m = jnp.where(strict, beta * kk * decay, 0.0)
            t = jnp.where(eye, 1.0, 0.0) - jnp.where(blk_xor < 2, m, 0.0)
            sz = 2
            while sz < CHUNK:
                c_s = jnp.where((blk_xor >= sz) & (blk_xor < 2 * sz), m, 0.0).astype(BF16)
                tb = t.astype(BF16)
                t = t - _dot(_dot(tb, c_s).astype(BF16), tb)
                sz *= 2
            kf = kc.astype(F32)
            rhs = jnp.concatenate([vc.astype(F32) * beta, kf * (beta * egc)], axis=1).astype(BF16)
            sol = _dot(t.astype(BF16), rhs)
            solb = sol.astype(BF16)
            qkm = (qk * decay).astype(BF16)
            qo = _dot(qkm, solb)
            oi_ref[0, d, :, hs] = qo[:, :HEAD_DIM]
            qe_ref[0, d, :, hs] = (qc.astype(F32) * egc - qo[:, HEAD_DIM:]).astype(BF16)
            kg = (kf * egl).astype(BF16)
            for ci in range(nch):
                rs = slice(ci * CHUNK, (ci + 1) * CHUNK)
                ab = _dot_tn(kg[rs], solb[rs])
                a_ref[0, d, ci, :, hs] = (-ab[:, HEAD_DIM:]).astype(BF16)
                b_ref[0, d, ci, :, hs] = ab[:, :HEAD_DIM]
                dg_ref[0, d, ci, :, hs] = jnp.broadcast_to(ea[ci * CHUNK:ci * CHUNK + 1, :], (8, HEAD_DIM))


def _gdn_local(q, k, v, gb, egc, egl, ea, gcr, tc):
    b, s, w = q.shape
    n = s // CHUNK
    nch = tc // CHUNK
    n_dir = 2 * GDN_HEADS
    grid = (b, s // tc)
    row = lambda width: pl.BlockSpec((1, tc, width), lambda bi, i: (bi, i, 0))
    return pl.pallas_call(
        _gdn_local_kernel,
        grid=grid,
        in_specs=[row(w), row(w), row(w), row(N_GATES), row(N_GATES), row(N_GATES), row(N_GATES),
                  pl.BlockSpec((1, n_dir, tc), lambda bi, i: (bi, 0, i))],
        out_specs=[pl.BlockSpec((1, 2, nch, HEAD_DIM, w), lambda bi, i: (bi, 0, i, 0, 0)),
                   pl.BlockSpec((1, 2, nch, HEAD_DIM, w), lambda bi, i: (bi, 0, i, 0, 0)),
                   pl.BlockSpec((1, 2, nch, 8, w), lambda bi, i: (bi, 0, i, 0, 0)),
                   pl.BlockSpec((1, 2, tc, w), lambda bi, i: (bi, 0, i, 0)),
                   pl.BlockSpec((1, 2, tc, w), lambda bi, i: (bi, 0, i, 0))],
        out_shape=[jax.ShapeDtypeStruct((b, 2, n, HEAD_DIM, w), BF16),
                   jax.ShapeDtypeStruct((b, 2, n, HEAD_DIM, w), F32),
                   jax.ShapeDtypeStruct((b, 2, n, 8, w), F32),
                   jax.ShapeDtypeStruct((b, 2, s, w), BF16),
                   jax.ShapeDtypeStruct((b, 2, s, w), F32)],
        compiler_params=_cparams(("arbitrary", "arbitrary")),
        name="gdn_local",
    )(q, k, v, gb, egc, egl, ea, gcr)


def _gdn_scan_kernel(af_ref, bf_ref, df_ref, qf_ref, of_ref, ab_ref, bb_ref, db_ref, qb_ref, ob_ref,
                     outf_ref, outb_ref, s_ref):
    @pl.when(pl.program_id(1) == 0)
    def _():
        s_ref[...] = jnp.zeros_like(s_ref)

    cs = af_ref.shape[2]
    dirs = ((af_ref, bf_ref, df_ref, qf_ref, of_ref, outf_ref), (ab_ref, bb_ref, db_ref, qb_ref, ob_ref, outb_ref))
    for j in range(cs):
        for d, (a_ref, b_ref, dg_ref, q_ref, o_ref, out_ref) in enumerate(dirs):
            ci = j if d == 0 else cs - 1 - j
            rs = slice(ci * CHUNK, (ci + 1) * CHUNK)
            for hh in range(GDN_HEADS):
                hs = slice(hh * HEAD_DIM, (hh + 1) * HEAD_DIM)
                si = d * GDN_HEADS + hh
                st = s_ref[si]
                stb = st.astype(BF16)
                out_ref[0, rs, hs] = _dot(q_ref[0, 0, rs, hs], stb) + o_ref[0, 0, rs, hs]
                s_ref[si] = (dg_ref[0, 0, ci, 0:1, hs] * st + _dot(a_ref[0, 0, ci, :, hs], stb)
                             + b_ref[0, 0, ci, :, hs])


def _gdn_scan(a, bm, dg, qe, oi, cs):
    b, _, n, _, w = a.shape
    s = n * CHUNK
    ns = n // cs
    tr = cs * CHUNK
    fwd5 = lambda bi, i: (bi, 0, i, 0, 0)
    bwd5 = lambda bi, i: (bi, 1, ns - 1 - i, 0, 0)
    fwd4 = lambda bi, i: (bi, 0, i, 0)
    bwd4 = lambda bi, i: (bi, 1, ns - 1 - i, 0)
    blk5 = (1, 1, cs, HEAD_DIM, w)
    blkd = (1, 1, cs, 8, w)
    blk4 = (1, 1, tr, w)
    return pl.pallas_call(
        _gdn_scan_kernel,
        grid=(b, ns),
        in_specs=[pl.BlockSpec(blk5, fwd5), pl.BlockSpec(blk5, fwd5), pl.BlockSpec(blkd, fwd5),
                  pl.BlockSpec(blk4, fwd4), pl.BlockSpec(blk4, fwd4),
                  pl.BlockSpec(blk5, bwd5), pl.BlockSpec(blk5, bwd5), pl.BlockSpec(blkd, bwd5),
                  pl.BlockSpec(blk4, bwd4), pl.BlockSpec(blk4, bwd4)],
        out_specs=[pl.BlockSpec((1, tr, w), lambda bi, i: (bi, i, 0)),
                   pl.BlockSpec((1, tr, w), lambda bi, i: (bi, ns - 1 - i, 0))],
        out_shape=[jax.ShapeDtypeStruct((b, s, w), F32)] * 2,
        scratch_shapes=[pltpu.VMEM((2 * GDN_HEADS, HEAD_DIM, HEAD_DIM), F32)],
        compiler_params=_cparams(("arbitrary", "arbitrary")),
        name="gdn_scan",
    )(a, bm, dg, qe, oi, a, bm, dg, qe, oi)


def _attn_kernel(q_ref, k_ref, v_ref, o_ref, q2_ref, m_ref, l_ref, acc_ref):
    j = pl.program_id(3)
    tq = q_ref.shape[1]

    @pl.when(j == 0)
    def _():
        for gi in range(ATTN_GROUP):
            q2_ref[gi * tq:(gi + 1) * tq, :] = q_ref[0, :, gi * HEAD_DIM:(gi + 1) * HEAD_DIM]
        m_ref[...] = jnp.full_like(m_ref, NEG_BIG)
        l_ref[...] = jnp.zeros_like(l_ref)
        acc_ref[...] = jnp.zeros_like(acc_ref)

    s = _dot_nt(q2_ref[...], k_ref[0])
    m_prev = m_ref[...]
    m_new = jnp.maximum(m_prev, jnp.max(s, axis=-1, keepdims=True))
    alpha = jnp.exp(m_prev - m_new)
    p = jnp.exp(s - m_new)
    l_ref[...] = alpha * l_ref[...] + jnp.sum(p, axis=-1, keepdims=True)
    acc_ref[...] = alpha * acc_ref[...] + _dot(p.astype(BF16), v_ref[0])
    m_ref[...] = m_new

    @pl.when(j == pl.num_programs(3) - 1)
    def _():
        out = acc_ref[...] / l_ref[...]
        for gi in range(ATTN_GROUP):
            o_ref[0, :, gi * HEAD_DIM:(gi + 1) * HEAD_DIM] = out[gi * tq:(gi + 1) * tq, :].astype(BF16)


def _attention(q, k, v, tq, tk):
    b, s, _ = q.shape
    gw = ATTN_GROUP * HEAD_DIM
    return pl.pallas_call(
        _attn_kernel,
        grid=(b, ATTN_KV_HEADS, s // tq, s // tk),
        in_specs=[pl.BlockSpec((1, tq, gw), lambda bi, g, i, j: (bi, i, g)),
                  pl.BlockSpec((1, tk, HEAD_DIM), lambda bi, g, i, j: (bi, j, g)),
                  pl.BlockSpec((1, tk, HEAD_DIM), lambda bi, g, i, j: (bi, j, g))],
        out_specs=pl.BlockSpec((1, tq, gw), lambda bi, g, i, j: (bi, i, g)),
        out_shape=jax.ShapeDtypeStruct((b, s, ATTN_WIDTH), BF16),
        scratch_shapes=[pltpu.VMEM((ATTN_GROUP * tq, HEAD_DIM), BF16),
                        pltpu.VMEM((ATTN_GROUP * tq, 1), F32),
                        pltpu.VMEM((ATTN_GROUP * tq, 1), F32),
                        pltpu.VMEM((ATTN_GROUP * tq, HEAD_DIM), F32)],
        compiler_params=_cparams(("arbitrary", "arbitrary", "arbitrary", "arbitrary")),
        name="gqa_attention",
    )(q, k, v)


ROUTE_W = 8


def _first_argmax(vals, lane_f, valid):
    vmax = jnp.max(jnp.where(valid, vals, NEG_BIG), axis=-1, keepdims=True)
    idx = jnp.min(jnp.where(valid & (vals == vmax), lane_f, float(LANES)), axis=-1, keepdims=True)
    return vmax, idx


def _mix_kernel(of_ref, ob_ref, z_ref, att_ref, x_ref, mod_ref, wout_ref, gng_ref, ang_ref,
                l1g_ref, l1b_ref, wrt_ref, brt_ref, x1_ref, h2_ref, route_ref):
    o = of_ref[0] + ob_ref[0]
    z = z_ref[0].astype(F32)
    parts = []
    for hh in range(GDN_HEADS):
        hs = slice(hh * HEAD_DIM, (hh + 1) * HEAD_DIM)
        oh = o[:, hs]
        on = oh * lax.rsqrt(jnp.mean(oh * oh, axis=-1, keepdims=True) + RMS_EPS) * gng_ref[...]
        parts.append((on * _silu(z[:, hs])).astype(BF16))
    att = att_ref[0].astype(F32)
    attn = (att * lax.rsqrt(jnp.mean(att * att, axis=-1, keepdims=True) + RMS_EPS) * ang_ref[...]).astype(BF16)
    mixed = _dot(jnp.concatenate(parts, axis=1), wout_ref[:GDN_WIDTH, :]) + _dot(attn, wout_ref[GDN_WIDTH:, :])
    gt1 = mod_ref[0, 2:3, :]
    sh2 = mod_ref[0, 3:4, :]
    sc2 = mod_ref[0, 4:5, :]
    x1 = _layer_norm(ALPHA * x_ref[0] + gt1 * mixed) * l1g_ref[...] + l1b_ref[...]
    x1_ref[0] = x1
    h2 = _layer_norm(x1) * (1.0 + sc2) + sh2
    h2_ref[0] = h2
    logits = _dot_f32(h2, wrt_ref[...]) + brt_ref[...]
    lane = lax.broadcasted_iota(I32, logits.shape, 1)
    lane_f = lane.astype(F32)
    is_grp = lane < N_GROUPS
    gmax, gidx = _first_argmax(logits, lane_f, is_grp)
    gsum = jnp.sum(jnp.where(is_grp, jnp.exp(jnp.minimum(logits - gmax, 0.0)), 0.0), axis=-1, keepdims=True)
    grp_p = 1.0 / gsum
    lo = float(N_GROUPS) + float(EXPERTS_PER_GROUP) * gidx
    in_grp = (lane_f >= lo) & (lane_f < lo + float(EXPERTS_PER_GROUP))
    v0, i0 = _first_argmax(logits, lane_f, in_grp)
    v1, i1 = _first_argmax(logits, lane_f, in_grp & (lane_f != i0))
    e1 = jnp.exp(v1 - v0)
    w0 = grp_p / (1.0 + e1)
    w1 = grp_p * e1 / (1.0 + e1)
    route = jnp.where(lane == 0, i0 - float(N_GROUPS),
                      jnp.where(lane == 1, i1 - float(N_GROUPS),
                                jnp.where(lane == 2, w0, jnp.where(lane == 3, w1, 0.0))))
    route_ref[0] = route[:, 0:ROUTE_W]


def _mixer_out(o_f, o_b, z, att, x, mod3, w_out, gng, ang, l1g, l1b, w_rt, b_rt, tm):
    b, s, d = x.shape
    row = lambda w: pl.BlockSpec((1, tm, w), lambda bi, i: (bi, i, 0))
    const = lambda shape: pl.BlockSpec(shape, lambda bi, i: (0,) * len(shape))
    return pl.pallas_call(
        _mix_kernel,
        grid=(b, s // tm),
        in_specs=[row(GDN_WIDTH), row(GDN_WIDTH), row(GDN_WIDTH), row(ATTN_WIDTH), row(d),
                  pl.BlockSpec((1, 6, d), lambda bi, i: (bi, 0, 0)),
                  const(w_out.shape), const((1, HEAD_DIM)), const((1, ATTN_WIDTH)),
                  const((1, d)), const((1, d)), const((d, LANES)), const((1, LANES))],
        out_specs=[row(d), row(d), row(ROUTE_W)],
        out_shape=[jax.ShapeDtypeStruct((b, s, d), F32),
                   jax.ShapeDtypeStruct((b, s, d), F32),
                   jax.ShapeDtypeStruct((b, s, ROUTE_W), F32)],
        compiler_params=_cparams(("arbitrary", "arbitrary")),
        name="mixer_out",
    )(o_f, o_b, z, att, x, mod3, w_out, gng, ang, l1g, l1b, w_rt, b_rt)


def _rank_kernel(route_ref, rank_ref, cnt_ref, carry_ref):
    @pl.when(pl.program_id(0) == 0)
    def _():
        carry_ref[...] = jnp.zeros_like(carry_ref)

    th = route_ref.shape[0]
    route = route_ref[...]
    lane_f = lax.broadcasted_iota(I32, (th, LANES), 1).astype(F32)
    oh0 = lane_f == route[:, 0:1]
    oh1 = lane_f == route[:, 1:2]
    both = jnp.where(oh0 | oh1, 1.0, 0.0).astype(BF16)
    r = lax.broadcasted_iota(I32, (th, th), 0)
    c = lax.broadcasted_iota(I32, (th, th), 1)
    before = _dot(jnp.where(r > c, 1.0, 0.0).astype(BF16), both) + carry_ref[...]
    rank0 = jnp.sum(jnp.where(oh0, before, 0.0), axis=-1, keepdims=True)
    rank1 = jnp.sum(jnp.where(oh1, before, 0.0), axis=-1, keepdims=True)
    lane8 = lax.broadcasted_iota(I32, (th, ROUTE_W), 1)
    rank_ref[...] = jnp.where(lane8 == 0, rank0, jnp.where(lane8 == 1, rank1, 0.0))
    total = carry_ref[...] + jnp.sum(both.astype(F32), axis=0, keepdims=True)
    carry_ref[...] = total
    cnt_ref[...] = total


def _expert_ranks(route, th):
    t = route.shape[0]
    return pl.pallas_call(
        _rank_kernel,
        grid=(t // th,),
        in_specs=[pl.BlockSpec((th, ROUTE_W), lambda i: (i, 0))],
        out_specs=[pl.BlockSpec((th, ROUTE_W), lambda i: (i, 0)),
                   pl.BlockSpec((1, LANES), lambda i: (0, 0))],
        out_shape=[jax.ShapeDtypeStruct((t, ROUTE_W), F32),
                   jax.ShapeDtypeStruct((1, LANES), F32)],
        scratch_shapes=[pltpu.VMEM((1, LANES), F32)],
        compiler_params=_cparams(("arbitrary",)),
        name="expert_ranks",
    )(route)


def _dest_kernel(route_ref, rank_ref, start_ref, dest_ref):
    th = route_ref.shape[0]
    route = route_ref[...]
    rank = rank_ref[...]
    lane_f = lax.broadcasted_iota(I32, (th, LANES), 1).astype(F32)
    start = start_ref[...]
    d0 = rank[:, 0:1] + jnp.sum(jnp.where(lane_f == route[:, 0:1], start, 0.0), axis=-1, keepdims=True)
    d1 = rank[:, 1:2] + jnp.sum(jnp.where(lane_f == route[:, 1:2], start, 0.0), axis=-1, keepdims=True)
    lane8 = lax.broadcasted_iota(I32, (th, ROUTE_W), 1)
    dest_ref[...] = jnp.where(lane8 == 0, d0, jnp.where(lane8 == 1, d1, 0.0)).astype(I32)


def _expert_dest(route, rank, start_row, th):
    t = route.shape[0]
    return pl.pallas_call(
        _dest_kernel,
        grid=(t // th,),
        in_specs=[pl.BlockSpec((th, ROUTE_W), lambda i: (i, 0)),
                  pl.BlockSpec((th, ROUTE_W), lambda i: (i, 0)),
                  pl.BlockSpec((1, LANES), lambda i: (0, 0))],
        out_specs=pl.BlockSpec((th, ROUTE_W), lambda i: (i, 0)),
        out_shape=jax.ShapeDtypeStruct((t, ROUTE_W), I32),
        compiler_params=_cparams(("arbitrary",)),
        name="expert_dest",
    )(route, rank, start_row)


def _dispatch_kernel(dest_hbm, h_ref, xs_in_hbm, xs_hbm, idx_smem, idx_sem, row_sem):
    del xs_in_hbm
    i = pl.program_id(0)
    td = h_ref.shape[0]
    idx_copy = pltpu.make_async_copy(dest_hbm.at[i], idx_smem, idx_sem)
    idx_copy.start()
    idx_copy.wait()

    def row_copy(r, kk):
        return pltpu.make_async_copy(h_ref.at[pl.ds(r, 1)], xs_hbm.at[pl.ds(idx_smem[2 * r + kk], 1)], row_sem)

    def start(r, carry):
        row_copy(r, 0).start()
        row_copy(r, 1).start()
        return carry

    def wait(r, carry):
        row_copy(r, 0).wait()
        row_copy(r, 1).wait()
        return carry

    lax.fori_loop(0, td, start, 0)
    lax.fori_loop(0, td, wait, 0)


def _dispatch(dest_tiles, h2, xs_init, td):
    t, d = h2.shape
    return pl.pallas_call(
        _dispatch_kernel,
        grid=(t // td,),
        in_specs=[pl.BlockSpec(memory_space=pl.ANY),
                  pl.BlockSpec((td, d), lambda i: (i, 0)),
                  pl.BlockSpec(memory_space=pl.ANY)],
        out_specs=pl.BlockSpec(memory_space=pl.ANY),
        out_shape=jax.ShapeDtypeStruct(xs_init.shape, xs_init.dtype),
        scratch_shapes=[pltpu.SMEM((2 * td,), I32), pltpu.SemaphoreType.DMA, pltpu.SemaphoreType.DMA],
        input_output_aliases={2: 0},
        compiler_params=_cparams(("arbitrary",)),
        name="moe_dispatch",
    )(dest_tiles, h2, xs_init)


def _expert_kernel(be_ref, nused_ref, xs_ref, w13_ref, w2_ref, ys_ref):
    @pl.when(pl.program_id(0) < nused_ref[0])
    def _():
        xb = xs_ref[...].astype(BF16)
        h13 = _dot(xb, w13_ref[0])
        hid = _silu(h13[:, :D_EXPERT]) * h13[:, D_EXPERT:]
        ys_ref[...] = _dot(hid.astype(BF16), w2_ref[0])

    @pl.when(pl.program_id(0) >= nused_ref[0])
    def _():
        ys_ref[...] = jnp.zeros_like(ys_ref)


def _experts(blk_expert, n_used, xs, w13, w2, blk):
    n_rows, d = xs.shape
    n_blocks = n_rows // blk
    row_map = lambda i, be, nu: (i, 0)
    grid_spec = pltpu.PrefetchScalarGridSpec(
        num_scalar_prefetch=2,
        grid=(n_blocks,),
        in_specs=[pl.BlockSpec((blk, d), row_map),
                  pl.BlockSpec((1, d, 2 * D_EXPERT), lambda i, be, nu: (be[i], 0, 0)),
                  pl.BlockSpec((1, D_EXPERT, d), lambda i, be, nu: (be[i], 0, 0))],
        out_specs=pl.BlockSpec((blk, d), row_map),
    )
    return pl.pallas_call(
        _expert_kernel,
        grid_spec=grid_spec,
        out_shape=jax.ShapeDtypeStruct((n_rows, d), F32),
        compiler_params=_cparams(("arbitrary",)),
        name="moe_experts",
    )(blk_expert, n_used, xs, w13, w2)


def _combine_kernel(dest_hbm, ys_hbm, x1_ref, route_ref, gt2_ref, l2g_ref, l2b_ref, o_ref,
                    idx_smem, buf_ref, idx_sem, row_sem):
    b_i = pl.program_id(0)
    i = pl.program_id(1)
    td = x1_ref.shape[1]
    tile = b_i * pl.num_programs(1) + i
    idx_copy = pltpu.make_async_copy(dest_hbm.at[tile], idx_smem, idx_sem)
    idx_copy.start()
    idx_copy.wait()

    def row_copy(r, kk):
        return pltpu.make_async_copy(ys_hbm.at[pl.ds(idx_smem[2 * r + kk], 1)], buf_ref.at[kk, pl.ds(r, 1)], row_sem)

    def start(r, carry):
        row_copy(r, 0).start()
        row_copy(r, 1).start()
        return carry

    def wait(r, carry):
        row_copy(r, 0).wait()
        row_copy(r, 1).wait()
        return carry

    lax.fori_loop(0, td, start, 0)
    lax.fori_loop(0, td, wait, 0)
    route = route_ref[0]
    ffn = buf_ref[0] * route[:, 2:3] + buf_ref[1] * route[:, 3:4]
    o_ref[0] = _layer_norm(ALPHA * x1_ref[0] + gt2_ref[0] * ffn) * l2g_ref[...] + l2b_ref[...]


def _combine(dest_tiles, ys, x1, route3, gt2, l2g, l2b, td):
    b, s, d = x1.shape
    row = lambda w: pl.BlockSpec((1, td, w), lambda bi, i: (bi, i, 0))
    const = lambda shape: pl.BlockSpec(shape, lambda bi, i: (0,) * len(shape))
    return pl.pallas_call(
        _combine_kernel,
        grid=(b, s // td),
        in_specs=[pl.BlockSpec(memory_space=pl.ANY),
                  pl.BlockSpec(memory_space=pl.ANY),
                  row(d), row(ROUTE_W),
                  pl.BlockSpec((1, 1, d), lambda bi, i: (bi, 0, 0)),
                  const((1, d)), const((1, d))],
        out_specs=row(d),
        out_shape=jax.ShapeDtypeStruct((b, s, d), F32),
        scratch_shapes=[pltpu.SMEM((2 * td,), I32), pltpu.VMEM((2, td, d), F32),
                        pltpu.SemaphoreType.DMA, pltpu.SemaphoreType.DMA],
        compiler_params=_cparams(("arbitrary", "arbitrary")),
        name="moe_combine",
    )(dest_tiles, ys, x1, route3, gt2, l2g, l2b)


def _tile(n, pref):
    t = min(n, pref)
    assert n % t == 0, (n, t)
    return t


def _rope_tables(s):
    half = HEAD_DIM // 2
    inv = ROPE_THETA ** (-jnp.arange(0, half, 2, dtype=F32) / half)
    pos = jnp.arange(s)
    row = (pos // GRID_W).astype(F32)[:, None] * inv[None, :]
    col = (pos % GRID_W).astype(F32)[:, None] * inv[None, :]
    cos = jnp.concatenate([jnp.cos(row), jnp.cos(row), jnp.cos(col), jnp.cos(col)], axis=-1)
    sin = jnp.concatenate([-jnp.sin(row), jnp.sin(row), -jnp.sin(col), jnp.sin(col)], axis=-1)
    return cos, sin


def _layer(x, c, w_ada, b_ada, w_in, conv_w, a_log, dt_bias, gdn_norm_g, q_norm_g, k_norm_g, attn_norm_g,
           w_out, ln1_g, ln1_b, w_group, b_group, w_router, b_router, w1, w3, w2, ln2_g, ln2_b):
    b, s, d = x.shape
    t = b * s
    assert s % CHUNK == 0 and s % GRID_W == 0

    c_pad = jnp.pad(c, ((0, (-b) % 8), (0, 0)))
    mod = _adaln_mod(c_pad, w_ada, b_ada.reshape(1, -1), _tile(6 * d, 1536))[:b]
    mod3 = mod.reshape(b, 6, d)

    gq, gk, gv, gz, gab, aq, ak, av = jnp.split(
        w_in, [GDN_WIDTH, 2 * GDN_WIDTH, 3 * GDN_WIDTH, 4 * GDN_WIDTH, 4 * GDN_WIDTH + N_GATES,
               4 * GDN_WIDTH + N_GATES + ATTN_WIDTH, 4 * GDN_WIDTH + N_GATES + ATTN_WIDTH + ATTN_KV_WIDTH], axis=1)
    w_packed = jnp.concatenate([gq, gk, gv, gz, aq, ak, av, jnp.pad(gab, ((0, 0), (0, LANES - N_GATES)))],
                               axis=1).astype(BF16)
    wab_t = gab.T.astype(BF16)
    cos_t, sin_t = _rope_tables(s)

    tm = _tile(s, 512)
    gqkv, z, ab, abt, a_q, a_k, a_v = _in_projection(
        x, mod3, w_packed, wab_t, cos_t, sin_t, q_norm_g.reshape(1, -1), k_norm_g.reshape(1, -1), tm)

    pad_gates = lambda p: jnp.pad(p.reshape(1, -1), ((0, 0), (0, N_GATES - p.size)))
    alog_c = pad_gates(a_log)
    dtb_c = pad_gates(dt_bias)
    tc = _tile(s, 512)
    gq_n, gk_n, gv_n, gb, egc, egl, ea, gcr = _gdn_prep(
        gqkv, conv_w, ab, abt, alog_c, dtb_c, alog_c.reshape(-1, 1), dtb_c.reshape(-1, 1), tc)
    a_m, b_m, a_dg, q_eff, o_in = _gdn_local(gq_n, gk_n, gv_n, gb, egc, egl, ea, gcr, _tile(s, 256))
    o_f, o_b = _gdn_scan(a_m, b_m, a_dg, q_eff, o_in, _tile(s // CHUNK, 4))

    att = _attention(a_q, a_k, a_v, _tile(s, 512), _tile(s, 1024))

    w_rt = jnp.pad(jnp.concatenate([w_group, w_router], axis=1), ((0, 0), (0, LANES - N_GROUPS - N_EXPERTS)))
    b_rt = jnp.pad(jnp.concatenate([b_group, b_router]).reshape(1, -1), ((0, 0), (0, LANES - N_GROUPS - N_EXPERTS)))
    x1, h2, route = _mixer_out(o_f, o_b, z, att, x, mod3, w_out.astype(BF16), gdn_norm_g.reshape(1, -1),
                               attn_norm_g.reshape(1, -1), ln1_g.reshape(1, -1), ln1_b.reshape(1, -1),
                               w_rt, b_rt, _tile(s, 512))

    blk = 256
    th = _tile(t, 512)
    route2 = route.reshape(t, ROUTE_W)
    rank, counts = _expert_ranks(route2, th)
    counts_i = counts[0, :N_EXPERTS].astype(I32)
    padded = (counts_i + blk - 1) // blk * blk
    pad_end = jnp.cumsum(padded)
    pad_start = pad_end - padded
    n_blocks = -(-(2 * t) // blk) + N_EXPERTS
    blk_pos = jnp.arange(n_blocks, dtype=I32) * blk
    blk_expert = jnp.minimum(jnp.sum((pad_end[None, :] <= blk_pos[:, None]).astype(I32), axis=1), N_EXPERTS - 1)
    n_used = jnp.maximum(pad_end[-1:] // blk, 1).astype(I32)
    start_row = jnp.pad(pad_start.astype(F32).reshape(1, -1), ((0, 0), (0, LANES - N_EXPERTS)))
    dest = _expert_dest(route2, rank, start_row, th)

    td = _tile(s, 256)
    dest_tiles = dest[:, 0:2].reshape(t // td, 2 * td)
    xs = _dispatch(dest_tiles, h2.reshape(t, d), jnp.zeros((n_blocks * blk, d), F32), td)
    w13 = jnp.concatenate([w1, w3], axis=-1).astype(BF16)
    ys = _experts(blk_expert, n_used, xs, w13, w2.astype(BF16), blk)
    gt2 = mod3[:, 5:6, :]
    return _combine(dest_tiles, ys, x1, route, gt2, ln2_g.reshape(1, -1), ln2_b.reshape(1, -1), td)


def kernel(x, c, w_ada, b_ada, w_in, conv_w, a_log, dt_bias, gdn_norm_g, q_norm_g, k_norm_g, attn_norm_g,
           w_out, ln1_g, ln1_b, w_group, b_group, w_router, b_router, w1, w3, w2, ln2_g, ln2_b):
    for layer in range(w_ada.shape[0]):
        x = _layer(x, c, w_ada[layer], b_ada[layer], w_in[layer], conv_w[layer], a_log[layer], dt_bias[layer],
                   gdn_norm_g[layer], q_norm_g[layer], k_norm_g[layer], attn_norm_g[layer], w_out[layer],
                   ln1_g[layer], ln1_b[layer], w_group[layer], b_group[layer], w_router[layer], b_router[layer],
                   w1[layer], w3[layer], w2[layer], ln2_g[layer], ln2_b[layer])
    return x
```

```python
import functools
import math

import jax
import jax.numpy as jnp
from jax import lax
from jax.experimental import pallas as pl
from jax.experimental.pallas import tpu as pltpu

F32 = jnp.float32
BF16 = jnp.bfloat16
I32 = jnp.int32

HEAD_DIM = 128
GDN_HEADS = 4
GDN_WIDTH = GDN_HEADS * HEAD_DIM
ATTN_Q_HEADS = 4
ATTN_KV_HEADS = 2
ATTN_GROUP = ATTN_Q_HEADS // ATTN_KV_HEADS
ATTN_WIDTH = ATTN_Q_HEADS * HEAD_DIM
ATTN_KV_WIDTH = ATTN_KV_HEADS * HEAD_DIM
CONV_K = 5
CHUNK = 64
GRID_W = 64
ROPE_THETA = 10000.0
N_GROUPS = 4
EXPERTS_PER_GROUP = 8
N_EXPERTS = N_GROUPS * EXPERTS_PER_GROUP
D_EXPERT = 256
DEPTH = 1
ALPHA = (2.0 * DEPTH) ** 0.25
LN_EPS = 1e-5
RMS_EPS = 1e-6

LANES = 128
VMEM_LIMIT_BYTES = 56 * 1024 * 1024
NEG_BIG = -1e30


def _cparams(semantics):
    return pltpu.CompilerParams(dimension_semantics=semantics, vmem_limit_bytes=VMEM_LIMIT_BYTES)


def _dot(a, b):
    return jnp.dot(a, b, preferred_element_type=F32)


def _dot_nt(a, b):
    return lax.dot_general(a, b, (((1,), (1,)), ((), ())), preferred_element_type=F32)


def _dot_tn(a, b):
    return lax.dot_general(a, b, (((0,), (0,)), ((), ())), preferred_element_type=F32)


def _split3(a):
    hi = a.astype(BF16)
    r = a - hi.astype(F32)
    mid = r.astype(BF16)
    lo = (r - mid.astype(F32)).astype(BF16)
    return hi, mid, lo


def _dot_f32_lhs_exact(a_bf16_exact, b):
    hi, mid, lo = _split3(b)
    return _dot(a_bf16_exact, hi) + _dot(a_bf16_exact, mid) + _dot(a_bf16_exact, lo)


def _dot_f32_rhs_exact(a, b_bf16_exact):
    hi, mid, lo = _split3(a)
    return _dot(hi, b_bf16_exact) + _dot(mid, b_bf16_exact) + _dot(lo, b_bf16_exact)


def _dot_f32(a, b):
    ah, am, _ = _split3(a)
    bh, bm, _ = _split3(b)
    return _dot(ah, bh) + (_dot(ah, bm) + _dot(am, bh))


def _sigmoid(x):
    return 1.0 / (1.0 + jnp.exp(-x))


def _silu(x):
    return x * _sigmoid(x)


def _softplus(x):
    return jnp.maximum(x, 0.0) + jnp.log1p(jnp.exp(-jnp.abs(x)))


def _layer_norm(x):
    mu = jnp.mean(x, axis=-1, keepdims=True)
    xc = x - mu
    var = jnp.mean(xc * xc, axis=-1, keepdims=True)
    return xc * lax.rsqrt(var + LN_EPS)


def _mod_kernel(c_ref, w_ref, b_ref, o_ref):
    o_ref[...] = _dot_f32(_silu(c_ref[...]), w_ref[...]) + b_ref[...]


def _adaln_mod(c_pad, w_ada, b_ada, tn):
    rows, d = c_pad.shape
    n = w_ada.shape[1]
    return pl.pallas_call(
        _mod_kernel,
        grid=(n // tn,),
        in_specs=[pl.BlockSpec((rows, d), lambda j: (0, 0)),
                  pl.BlockSpec((d, tn), lambda j: (0, j)),
                  pl.BlockSpec((1, tn), lambda j: (0, j))],
        out_specs=pl.BlockSpec((rows, tn), lambda j: (0, j)),
        out_shape=jax.ShapeDtypeStruct((rows, n), F32),
        compiler_params=_cparams(("arbitrary",)),
        name="adaln_mod",
    )(c_pad, w_ada, b_ada)


_W_GQKV = 0
_W_Z = 3 * GDN_WIDTH
_W_AK = _W_Z + GDN_WIDTH
_W_AB = _W_AK + ATTN_KV_WIDTH
_W_COLS = _W_AB + LANES
N_GATES = 4 * GDN_HEADS
LOG2E = math.log2(math.e)


def _rope(xh, cos, sin_signed, lane):
    fwd = pltpu.roll(xh, 32, 1)
    bwd = pltpu.roll(xh, LANES - 32, 1)
    partner = jnp.where((lane % 64) < 32, bwd, fwd)
    return xh * cos + partner * sin_signed


def _rope_t(xt, cos_t, sin_signed_t):
    q = HEAD_DIM // 4
    partner = jnp.concatenate([xt[q:2 * q], xt[0:q], xt[3 * q:4 * q], xt[2 * q:3 * q]], axis=0)
    return xt * cos_t + partner * sin_signed_t


def _inproj_kernel(x_ref, mod_ref, w_ref, wqvt_ref, wabt_ref, cos_ref, sin_ref, cost_ref, sint_ref, qg_ref, kg_ref,
                   gqkv_ref, z_ref, ab_ref, abt_ref, aqt_ref, ak_ref, avt_ref):
    x = x_ref[0]
    sh1 = mod_ref[0, 0:1, :]
    sc1 = mod_ref[0, 1:2, :]
    h = _layer_norm(x) * (1.0 + sc1) + sh1
    hb = h.astype(BF16)
    gqkv_ref[0] = _dot(hb, w_ref[:, _W_GQKV:_W_Z]).astype(BF16)
    z_ref[0] = _dot(hb, w_ref[:, _W_Z:_W_AK]).astype(BF16)
    ab_ref[0] = _dot(hb, w_ref[:, _W_AB:_W_COLS])[:, 0:N_GATES]
    abt_ref[0] = _dot_nt(wabt_ref[...], hb)
    ak = _dot(hb, w_ref[:, _W_AK:_W_AB])
    cos = cos_ref[...]
    sin = sin_ref[...]
    lane = lax.broadcasted_iota(I32, cos.shape, 1)
    for j in range(ATTN_KV_HEADS):
        xh = ak[:, j * HEAD_DIM:(j + 1) * HEAD_DIM]
        xn = xh * lax.rsqrt(jnp.mean(xh * xh, axis=-1, keepdims=True) + RMS_EPS) * kg_ref[...]
        ak_ref[0, :, j * HEAD_DIM:(j + 1) * HEAD_DIM] = _rope(xn, cos, sin, lane).astype(BF16)
    qvt = _dot_nt(wqvt_ref[...], hb)
    cos_t = cost_ref[...]
    sin_t = sint_ref[...]
    q_scale = (HEAD_DIM ** -0.5) * LOG2E
    for i in range(ATTN_Q_HEADS):
        xt = qvt[i * HEAD_DIM:(i + 1) * HEAD_DIM, :]
        xn = xt * lax.rsqrt(jnp.mean(xt * xt, axis=0, keepdims=True) + RMS_EPS) * qg_ref[...]
        aqt_ref[0, i * HEAD_DIM:(i + 1) * HEAD_DIM, :] = (_rope_t(xn, cos_t, sin_t) * q_scale).astype(BF16)
    avt_ref[0] = qvt[ATTN_WIDTH:, :].astype(BF16)


def _in_projection(x, mod3, w_packed, wqv_t, wab_t, cos, sin, cos_t, sin_t, qg_col, kg_row, tm):
    b, s, d = x.shape
    grid = (b, s // tm)
    row = lambda w: pl.BlockSpec((1, tm, w), lambda bi, i: (bi, i, 0))
    col = lambda h: pl.BlockSpec((1, h, tm), lambda bi, i: (bi, 0, i))
    const = lambda shape: pl.BlockSpec(shape, lambda bi, i: (0,) * len(shape))
    out_shapes = [
        jax.ShapeDtypeStruct((b, s, 3 * GDN_WIDTH), BF16),
        jax.ShapeDtypeStruct((b, s, GDN_WIDTH), BF16),
        jax.ShapeDtypeStruct((b, s, N_GATES), F32),
        jax.ShapeDtypeStruct((b, N_GATES, s), F32),
        jax.ShapeDtypeStruct((b, ATTN_WIDTH, s), BF16),
        jax.ShapeDtypeStruct((b, s, ATTN_KV_WIDTH), BF16),
        jax.ShapeDtypeStruct((b, ATTN_KV_WIDTH, s), BF16),
    ]
    out_specs = [row(3 * GDN_WIDTH), row(GDN_WIDTH), row(N_GATES), col(N_GATES),
                 col(ATTN_WIDTH), row(ATTN_KV_WIDTH), col(ATTN_KV_WIDTH)]
    return pl.pallas_call(
        _inproj_kernel,
        grid=grid,
        in_specs=[row(d),
                  pl.BlockSpec((1, 6, d), lambda bi, i: (bi, 0, 0)),
                  const((d, _W_COLS)),
                  const((ATTN_WIDTH + ATTN_KV_WIDTH, d)),
                  const((N_GATES, d)),
                  pl.BlockSpec((tm, HEAD_DIM), lambda bi, i: (i, 0)),
                  pl.BlockSpec((tm, HEAD_DIM), lambda bi, i: (i, 0)),
                  pl.BlockSpec((HEAD_DIM, tm), lambda bi, i: (0, i)),
                  pl.BlockSpec((HEAD_DIM, tm), lambda bi, i: (0, i)),
                  const((HEAD_DIM, 1)),
                  const((1, HEAD_DIM))],
        out_specs=out_specs,
        out_shape=out_shapes,
        compiler_params=_cparams(("arbitrary", "arbitrary")),
        name="in_projection",
    )(x, mod3, w_packed, wqv_t, wab_t, cos, sin, cos_t, sin_t, qg_col, kg_row)


_HALO = 16


def _chunk_masks(n, transpose=False):
    r = lax.broadcasted_iota(I32, (n, n), 0)
    c = lax.broadcasted_iota(I32, (n, n), 1)
    same = (r // CHUNK) == (c // CHUNK)
    lower = same & (r >= c)
    upper = same & (r <= c)
    return lower, upper


def _gdn_prep_kernel(prev_ref, main_ref, next_ref, convw_ref, ab_ref, abt_ref,
                     alog_c_ref, dtb_c_ref, alog_r_ref, dtb_r_ref,
                     q_ref, k_ref, v_ref, gb_ref, egc_ref, egl_ref, ea_ref, gcr_ref, buf_ref):
    i = pl.program_id(1)
    n_i = pl.num_programs(1)
    tc = main_ref.shape[1]
    prev_scale = jnp.where(i > 0, 1.0, 0.0).astype(F32)
    next_scale = jnp.where(i < n_i - 1, 1.0, 0.0).astype(F32)
    buf_ref[0:_HALO, :] = prev_ref[0].astype(F32) * prev_scale
    buf_ref[_HALO:_HALO + tc, :] = main_ref[0].astype(F32)
    buf_ref[_HALO + tc:, :] = next_ref[0].astype(F32) * next_scale
    pad = CONV_K // 2
    for part, out_ref in enumerate((q_ref, k_ref, v_ref)):
        for hh in range(GDN_HEADS):
            c0 = part * GDN_WIDTH + hh * HEAD_DIM
            acc = None
            for j in range(CONV_K):
                term = buf_ref[_HALO - pad + j:_HALO - pad + j + tc, c0:c0 + HEAD_DIM] * convw_ref[j:j + 1, c0:c0 + HEAD_DIM]
                acc = term if acc is None else acc + term
            y = _silu(acc)
            if part < 2:
                y = y * lax.rsqrt(jnp.sum(y * y, axis=-1, keepdims=True) + RMS_EPS)
            if part == 0:
                y = y * (HEAD_DIM ** -0.5)
            out_ref[0, :, hh * HEAD_DIM:(hh + 1) * HEAD_DIM] = y.astype(BF16)

    lower, upper = _chunk_masks(tc)
    lower_b = jnp.where(lower, 1.0, 0.0).astype(BF16)
    upper_b = jnp.where(upper, 1.0, 0.0).astype(BF16)
    n_dir = 2 * GDN_HEADS
    ab = ab_ref[0]
    lane = lax.broadcasted_iota(I32, ab.shape, 1)
    g = -jnp.exp(alog_c_ref[...]) * _softplus(ab + dtb_c_ref[...])
    pre = _dot_f32_lhs_exact(lower_b, g)
    suf = _dot_f32_lhs_exact(upper_b, g)
    gtot = pre + suf - g
    gc = jnp.where(lane < GDN_HEADS, pre, suf)
    gb_ref[0] = jnp.where(lane < n_dir, gc, _sigmoid(ab))
    egc_ref[0] = jnp.exp(gc)
    egl_ref[0] = jnp.exp(gtot - gc)
    ea_ref[0] = jnp.exp(gtot)
    abt = abt_ref[0]
    row = lax.broadcasted_iota(I32, abt.shape, 0)
    g_r = -jnp.exp(alog_r_ref[...]) * _softplus(abt + dtb_r_ref[...])
    pre_r = _dot_f32_rhs_exact(g_r, upper_b)
    suf_r = _dot_f32_rhs_exact(g_r, lower_b)
    gcr_ref[0] = jnp.where(row < GDN_HEADS, pre_r, suf_r)[0:n_dir]


def _gdn_prep(gqkv, conv_w, ab, abt, alog_c, dtb_c, alog_r, dtb_r, tc):
    b, s, w = gqkv.shape
    nh = tc // _HALO
    n_dir = 2 * GDN_HEADS
    grid = (b, s // tc)
    row = lambda width: pl.BlockSpec((1, tc, width), lambda bi, i: (bi, i, 0))
    const = lambda shape: pl.BlockSpec(shape, lambda bi, i: (0,) * len(shape))
    last_halo = s // _HALO - 1
    out_shapes = ([jax.ShapeDtypeStruct((b, s, GDN_WIDTH), BF16)] * 3
                  + [jax.ShapeDtypeStruct((b, s, N_GATES), F32)] * 4
                  + [jax.ShapeDtypeStruct((b, n_dir, s), F32)])
    out_specs = ([row(GDN_WIDTH)] * 3 + [row(N_GATES)] * 4
                 + [pl.BlockSpec((1, n_dir, tc), lambda bi, i: (bi, 0, i))])
    return pl.pallas_call(
        _gdn_prep_kernel,
        grid=grid,
        in_specs=[pl.BlockSpec((1, _HALO, w), lambda bi, i: (bi, jnp.maximum(i * nh - 1, 0), 0)),
                  row(w),
                  pl.BlockSpec((1, _HALO, w), lambda bi, i: (bi, jnp.minimum((i + 1) * nh, last_halo), 0)),
                  const((CONV_K, w)),
                  row(N_GATES),
                  pl.BlockSpec((1, N_GATES, tc), lambda bi, i: (bi, 0, i)),
                  const((1, N_GATES)), const((1, N_GATES)),
                  const((N_GATES, 1)), const((N_GATES, 1))],
        out_specs=out_specs,
        out_shape=out_shapes,
        scratch_shapes=[pltpu.VMEM((tc + 2 * _HALO, w), F32)],
        compiler_params=_cparams(("arbitrary", "arbitrary")),
        name="gdn_prep",
    )(gqkv, gqkv, gqkv, conv_w, ab, abt, alog_c, dtb_c, alog_r, dtb_r)


def _gdn_local_kernel(q_ref, k_ref, v_ref, gb_ref, egc_ref, egl_ref, ea_ref, gcr_ref,
                      a_ref, b_ref, dg_ref, qe_ref, oi_ref):
    tc = q_ref.shape[1]
    nch = tc // CHUNK
    lower, upper = _chunk_masks(tc)
    r = lax.broadcasted_iota(I32, (tc, tc), 0)
    c = lax.broadcasted_iota(I32, (tc, tc), 1)
    eye = r == c
    blk_xor = r ^ c
    n_dir = 2 * GDN_HEADS
    combos = [(d, hh) for d in range(2) for hh in range(GDN_HEADS)]
    head = lambda hh: slice(hh * HEAD_DIM, (hh + 1) * HEAD_DIM)
    kk = [_dot_nt(k_ref[0, :, head(hh)], k_ref[0, :, head(hh)]) for hh in range(GDN_HEADS)]
    qk = [_dot_nt(q_ref[0, :, head(hh)], k_ref[0, :, head(hh)]) for hh in range(GDN_HEADS)]
    decay, m, t = {}, {}, {}
    for cb in combos:
        d, hh = cb
        idx = d * GDN_HEADS + hh
        incl = lower if d == 0 else upper
        gc = gb_ref[0, :, idx:idx + 1]
        beta = gb_ref[0, :, n_dir + idx:n_dir + idx + 1]
        gcr = gcr_ref[0, idx:idx + 1, :]
        decay[cb] = jnp.where(incl, jnp.exp(jnp.minimum(gc - gcr, 0.0)), 0.0)
        m[cb] = jnp.where(eye, 0.0, beta * kk[hh] * decay[cb])
        t[cb] = jnp.where(eye, 1.0, 0.0) - jnp.where(blk_xor < 2, m[cb], 0.0)
    sz = 2
    while sz < CHUNK:
        join = (blk_xor >= sz) & (blk_xor < 2 * sz)
        tb = {cb: t[cb].astype(BF16) for cb in combos}
        tc_s = {cb: _dot(tb[cb], jnp.where(join, m[cb], 0.0).astype(BF16)).astype(BF16) for cb in combos}
        for cb in combos:
            t[cb] = t[cb] - _dot(tc_s[cb], tb[cb])
        sz *= 2
    solb = {}
    for cb in combos:
        d, hh = cb
        idx = d * GDN_HEADS + hh
        beta = gb_ref[0, :, n_dir + idx:n_dir + idx + 1]
        egc = egc_ref[0, :, idx:idx + 1]
        rhs = jnp.concatenate([v_ref[0, :, head(hh)].astype(F32) * beta,
                               k_ref[0, :, head(hh)].astype(F32) * (beta * egc)], axis=1).astype(BF16)
        solb[cb] = _dot(t[cb].astype(BF16), rhs).astype(BF16)
    for cb in combos:
        d, hh = cb
        idx = d * GDN_HEADS + hh
        egc = egc_ref[0, :, idx:idx + 1]
        qo = _dot((qk[hh] * decay[cb]).astype(BF16), solb[cb])
        oi_ref[0, d, :, head(hh)] = qo[:, :HEAD_DIM]
        qe_ref[0, d, :, head(hh)] = (q_ref[0, :, head(hh)].astype(F32) * egc - qo[:, HEAD_DIM:]).astype(BF16)
    for cb in combos:
        d, hh = cb
        idx = d * GDN_HEADS + hh
        egl = egl_ref[0, :, idx:idx + 1]
        ea = ea_ref[0, :, idx:idx + 1]
        kg = (k_ref[0, :, head(hh)].astype(F32) * egl).astype(BF16)
        for ci in range(nch):
            rs = slice(ci * CHUNK, (ci + 1) * CHUNK)
            ab = _dot_tn(kg[rs], solb[cb][rs])
            a_ref[0, d, ci, :, head(hh)] = (-ab[:, HEAD_DIM:]).astype(BF16)
            b_ref[0, d, ci, :, head(hh)] = ab[:, :HEAD_DIM]
            dg_ref[0, d, ci, :, head(hh)] = jnp.broadcast_to(ea[ci * CHUNK:ci * CHUNK + 1, :], (8, HEAD_DIM))


def _gdn_local(q, k, v, gb, egc, egl, ea, gcr, tc):
    b, s, w = q.shape
    n = s // CHUNK
    nch = tc // CHUNK
    n_dir = 2 * GDN_HEADS
    grid = (b, s // tc)
    row = lambda width: pl.BlockSpec((1, tc, width), lambda bi, i: (bi, i, 0))
    return pl.pallas_call(
        _gdn_local_kernel,
        grid=grid,
        in_specs=[row(w), row(w), row(w), row(N_GATES), row(N_GATES), row(N_GATES), row(N_GATES),
                  pl.BlockSpec((1, n_dir, tc), lambda bi, i: (bi, 0, i))],
        out_specs=[pl.BlockSpec((1, 2, nch, HEAD_DIM, w), lambda bi, i: (bi, 0, i, 0, 0)),
                   pl.BlockSpec((1, 2, nch, HEAD_DIM, w), lambda bi, i: (bi, 0, i, 0, 0)),
                   pl.BlockSpec((1, 2, nch, 8, w), lambda bi, i: (bi, 0, i, 0, 0)),
                   pl.BlockSpec((1, 2, tc, w), lambda bi, i: (bi, 0, i, 0)),
                   pl.BlockSpec((1, 2, tc, w), lambda bi, i: (bi, 0, i, 0))],
        out_shape=[jax.ShapeDtypeStruct((b, 2, n, HEAD_DIM, w), BF16),
                   jax.ShapeDtypeStruct((b, 2, n, HEAD_DIM, w), F32),
                   jax.ShapeDtypeStruct((b, 2, n, 8, w), F32),
                   jax.ShapeDtypeStruct((b, 2, s, w), BF16),
                   jax.ShapeDtypeStruct((b, 2, s, w), F32)],
        compiler_params=_cparams(("arbitrary", "arbitrary")),
        name="gdn_local",
    )(q, k, v, gb, egc, egl, ea, gcr)


def _gdn_scan_kernel(af_ref, bf_ref, df_ref, qf_ref, of_ref, ab_ref, bb_ref, db_ref, qb_ref, ob_ref,
                     outf_ref, outb_ref, s_ref):
    @pl.when(pl.program_id(1) == 0)
    def _():
        s_ref[...] = jnp.zeros_like(s_ref)

    cs = af_ref.shape[2]
    dirs = ((af_ref, bf_ref, df_ref, qf_ref, of_ref, outf_ref), (ab_ref, bb_ref, db_ref, qb_ref, ob_ref, outb_ref))
    for j in range(cs):
        for d, (a_ref, b_ref, dg_ref, q_ref, o_ref, out_ref) in enumerate(dirs):
            ci = j if d == 0 else cs - 1 - j
            rs = slice(ci * CHUNK, (ci + 1) * CHUNK)
            for hh in range(GDN_HEADS):
                hs = slice(hh * HEAD_DIM, (hh + 1) * HEAD_DIM)
                si = d * GDN_HEADS + hh
                st = s_ref[si]
                stb = st.astype(BF16)
                out_ref[0, rs, hs] = _dot(q_ref[0, 0, rs, hs], stb) + o_ref[0, 0, rs, hs]
                s_ref[si] = (dg_ref[0, 0, ci, 0:1, hs] * st + _dot(a_ref[0, 0, ci, :, hs], stb)
                             + b_ref[0, 0, ci, :, hs])


def _gdn_scan(a, bm, dg, qe, oi, cs):
    b, _, n, _, w = a.shape
    s = n * CHUNK
    ns = n // cs
    tr = cs * CHUNK
    fwd5 = lambda bi, i: (bi, 0, i, 0, 0)
    bwd5 = lambda bi, i: (bi, 1, ns - 1 - i, 0, 0)
    fwd4 = lambda bi, i: (bi, 0, i, 0)
    bwd4 = lambda bi, i: (bi, 1, ns - 1 - i, 0)
    blk5 = (1, 1, cs, HEAD_DIM, w)
    blkd = (1, 1, cs, 8, w)
    blk4 = (1, 1, tr, w)
    return pl.pallas_call(
        _gdn_scan_kernel,
        grid=(b, ns),
        in_specs=[pl.BlockSpec(blk5, fwd5), pl.BlockSpec(blk5, fwd5), pl.BlockSpec(blkd, fwd5),
                  pl.BlockSpec(blk4, fwd4), pl.BlockSpec(blk4, fwd4),
                  pl.BlockSpec(blk5, bwd5), pl.BlockSpec(blk5, bwd5), pl.BlockSpec(blkd, bwd5),
                  pl.BlockSpec(blk4, bwd4), pl.BlockSpec(blk4, bwd4)],
        out_specs=[pl.BlockSpec((1, tr, w), lambda bi, i: (bi, i, 0)),
                   pl.BlockSpec((1, tr, w), lambda bi, i: (bi, ns - 1 - i, 0))],
        out_shape=[jax.ShapeDtypeStruct((b, s, w), F32)] * 2,
        scratch_shapes=[pltpu.VMEM((2 * GDN_HEADS, HEAD_DIM, HEAD_DIM), F32)],
        compiler_params=_cparams(("arbitrary", "arbitrary")),
        name="gdn_scan",
    )(a, bm, dg, qe, oi, a, bm, dg, qe, oi)


ATTN_SUB = 256


def _attn_kernel(qt_ref, k_ref, vt_ref, ot_ref, q2t_ref, m_ref, l_ref, acc_ref):
    j = pl.program_id(3)
    tq = qt_ref.shape[2]
    tk = k_ref.shape[1]

    @pl.when(j == 0)
    def _():
        for gi in range(ATTN_GROUP):
            q2t_ref[:, gi * tq:(gi + 1) * tq] = qt_ref[0, gi * HEAD_DIM:(gi + 1) * HEAD_DIM, :]
        m_ref[...] = jnp.full_like(m_ref, NEG_BIG)
        l_ref[...] = jnp.zeros_like(l_ref)
        acc_ref[...] = jnp.zeros_like(acc_ref)

    q2t = q2t_ref[...]
    m_prev = m_ref[...]
    l_prev = l_ref[...]
    n_sub = tk // ATTN_SUB
    scores = lambda jj: _dot(k_ref[0, jj * ATTN_SUB:(jj + 1) * ATTN_SUB, :], q2t)
    st_next = scores(0)
    for jj in range(n_sub):
        ks = slice(jj * ATTN_SUB, (jj + 1) * ATTN_SUB)
        st = st_next
        if jj + 1 < n_sub:
            st_next = scores(jj + 1)
        m_new = jnp.maximum(m_prev, jnp.max(st, axis=0, keepdims=True))
        alpha = jnp.exp2(m_prev - m_new)
        p = jnp.exp2(st - m_new)
        l_prev = alpha * l_prev + jnp.sum(p, axis=0, keepdims=True)
        acc_ref[...] = alpha * acc_ref[...] + _dot(vt_ref[0, :, ks], p.astype(BF16))
        m_prev = m_new
    m_ref[...] = m_prev
    l_ref[...] = l_prev

    @pl.when(j == pl.num_programs(3) - 1)
    def _():
        out = acc_ref[...] / l_ref[...]
        for gi in range(ATTN_GROUP):
            ot_ref[0, gi * HEAD_DIM:(gi + 1) * HEAD_DIM, :] = out[:, gi * tq:(gi + 1) * tq].astype(BF16)


def _attention(q_t, k, v_t, tq, tk):
    b, _, s = q_t.shape
    gw = ATTN_GROUP * HEAD_DIM
    assert tk % ATTN_SUB == 0
    return pl.pallas_call(
        _attn_kernel,
        grid=(b, ATTN_KV_HEADS, s // tq, s // tk),
        in_specs=[pl.BlockSpec((1, gw, tq), lambda bi, g, i, j: (bi, g, i)),
                  pl.BlockSpec((1, tk, HEAD_DIM), lambda bi, g, i, j: (bi, j, g)),
                  pl.BlockSpec((1, HEAD_DIM, tk), lambda bi, g, i, j: (bi, g, j))],
        out_specs=pl.BlockSpec((1, gw, tq), lambda bi, g, i, j: (bi, g, i)),
        out_shape=jax.ShapeDtypeStruct((b, ATTN_WIDTH, s), BF16),
        scratch_shapes=[pltpu.VMEM((HEAD_DIM, ATTN_GROUP * tq), BF16),
                        pltpu.VMEM((1, ATTN_GROUP * tq), F32),
                        pltpu.VMEM((1, ATTN_GROUP * tq), F32),
                        pltpu.VMEM((HEAD_DIM, ATTN_GROUP * tq), F32)],
        compiler_params=_cparams(("arbitrary", "arbitrary", "arbitrary", "arbitrary")),
        name="gqa_attention",
    )(q_t, k, v_t)


ROUTE_W = 8


def _first_argmax(vals, lane_f, valid):
    vmax = jnp.max(jnp.where(valid, vals, NEG_BIG), axis=-1, keepdims=True)
    idx = jnp.min(jnp.where(valid & (vals == vmax), lane_f, float(LANES)), axis=-1, keepdims=True)
    return vmax, idx


def _mix_kernel(of_ref, ob_ref, z_ref, att_ref, x_ref, mod_ref, wout_ref, gng_ref, ang_ref,
                l1g_ref, l1b_ref, wrt_ref, brt_ref, x1_ref, h2_ref, route_ref):
    o = of_ref[0] + ob_ref[0]
    z = z_ref[0].astype(F32)
    parts = []
    for hh in range(GDN_HEADS):
        hs = slice(hh * HEAD_DIM, (hh + 1) * HEAD_DIM)
        oh = o[:, hs]
        on = oh * lax.rsqrt(jnp.mean(oh * oh, axis=-1, keepdims=True) + RMS_EPS) * gng_ref[...]
        parts.append((on * _silu(z[:, hs])).astype(BF16))
    att_t = att_ref[0].astype(F32)
    attn_t = (att_t * lax.rsqrt(jnp.mean(att_t * att_t, axis=0, keepdims=True) + RMS_EPS) * ang_ref[...]).astype(BF16)
    mixed = _dot(jnp.concatenate(parts, axis=1), wout_ref[:GDN_WIDTH, :]) + _dot_tn(attn_t, wout_ref[GDN_WIDTH:, :])
    gt1 = mod_ref[0, 2:3, :]
    sh2 = mod_ref[0, 3:4, :]
    sc2 = mod_ref[0, 4:5, :]
    x1 = _layer_norm(ALPHA * x_ref[0] + gt1 * mixed) * l1g_ref[...] + l1b_ref[...]
    x1_ref[0] = x1
    h2 = _layer_norm(x1) * (1.0 + sc2) + sh2
    h2_ref[0] = h2
    logits = _dot_f32(h2, wrt_ref[...]) + brt_ref[...]
    lane = lax.broadcasted_iota(I32, logits.shape, 1)
    lane_f = lane.astype(F32)
    is_grp = lane < N_GROUPS
    gmax, gidx = _first_argmax(logits, lane_f, is_grp)
    gsum = jnp.sum(jnp.where(is_grp, jnp.exp(jnp.minimum(logits - gmax, 0.0)), 0.0), axis=-1, keepdims=True)
    grp_p = 1.0 / gsum
    lo = float(N_GROUPS) + float(EXPERTS_PER_GROUP) * gidx
    in_grp = (lane_f >= lo) & (lane_f < lo + float(EXPERTS_PER_GROUP))
    v0, i0 = _first_argmax(logits, lane_f, in_grp)
    v1, i1 = _first_argmax(logits, lane_f, in_grp & (lane_f != i0))
    e1 = jnp.exp(v1 - v0)
    w0 = grp_p / (1.0 + e1)
    w1 = grp_p * e1 / (1.0 + e1)
    route = jnp.where(lane == 0, i0 - float(N_GROUPS),
                      jnp.where(lane == 1, i1 - float(N_GROUPS),
                                jnp.where(lane == 2, w0, jnp.where(lane == 3, w1, 0.0))))
    route_ref[0] = route[:, 0:ROUTE_W]


def _mixer_out(o_f, o_b, z, att, x, mod3, w_out, gng, ang, l1g, l1b, w_rt, b_rt, tm):
    b, s, d = x.shape
    row = lambda w: pl.BlockSpec((1, tm, w), lambda bi, i: (bi, i, 0))
    const = lambda shape: pl.BlockSpec(shape, lambda bi, i: (0,) * len(shape))
    return pl.pallas_call(
        _mix_kernel,
        grid=(b, s // tm),
        in_specs=[row(GDN_WIDTH), row(GDN_WIDTH), row(GDN_WIDTH),
                  pl.BlockSpec((1, ATTN_WIDTH, tm), lambda bi, i: (bi, 0, i)), row(d),
                  pl.BlockSpec((1, 6, d), lambda bi, i: (bi, 0, 0)),
                  const(w_out.shape), const((1, HEAD_DIM)), const((ATTN_WIDTH, 1)),
                  const((1, d)), const((1, d)), const((d, LANES)), const((1, LANES))],
        out_specs=[row(d), row(d), row(ROUTE_W)],
        out_shape=[jax.ShapeDtypeStruct((b, s, d), F32),
                   jax.ShapeDtypeStruct((b, s, d), F32),
                   jax.ShapeDtypeStruct((b, s, ROUTE_W), F32)],
        compiler_params=_cparams(("arbitrary", "arbitrary")),
        name="mixer_out",
    )(o_f, o_b, z, att, x, mod3, w_out, gng, ang, l1g, l1b, w_rt, b_rt)


def _rank_kernel(route_ref, rank_ref, cnt_ref, carry_ref):
    @pl.when(pl.program_id(0) == 0)
    def _():
        carry_ref[...] = jnp.zeros_like(carry_ref)

    th = route_ref.shape[0]
    route = route_ref[...]
    lane_f = lax.broadcasted_iota(I32, (th, LANES), 1).astype(F32)
    oh0 = lane_f == route[:, 0:1]
    oh1 = lane_f == route[:, 1:2]
    both = jnp.where(oh0 | oh1, 1.0, 0.0).astype(BF16)
    r = lax.broadcasted_iota(I32, (th, th), 0)
    c = lax.broadcasted_iota(I32, (th, th), 1)
    before = _dot(jnp.where(r > c, 1.0, 0.0).astype(BF16), both) + carry_ref[...]
    rank0 = jnp.sum(jnp.where(oh0, before, 0.0), axis=-1, keepdims=True)
    rank1 = jnp.sum(jnp.where(oh1, before, 0.0), axis=-1, keepdims=True)
    lane8 = lax.broadcasted_iota(I32, (th, ROUTE_W), 1)
    rank_ref[...] = jnp.where(lane8 == 0, rank0, jnp.where(lane8 == 1, rank1, 0.0))
    total = carry_ref[...] + jnp.sum(both.astype(F32), axis=0, keepdims=True)
    carry_ref[...] = total
    cnt_ref[...] = total


def _expert_ranks(route, th):
    t = route.shape[0]
    return pl.pallas_call(
        _rank_kernel,
        grid=(t // th,),
        in_specs=[pl.BlockSpec((th, ROUTE_W), lambda i: (i, 0))],
        out_specs=[pl.BlockSpec((th, ROUTE_W), lambda i: (i, 0)),
                   pl.BlockSpec((1, LANES), lambda i: (0, 0))],
        out_shape=[jax.ShapeDtypeStruct((t, ROUTE_W), F32),
                   jax.ShapeDtypeStruct((1, LANES), F32)],
        scratch_shapes=[pltpu.VMEM((1, LANES), F32)],
        compiler_params=_cparams(("arbitrary",)),
        name="expert_ranks",
    )(route)


def _dest_kernel(route_ref, rank_ref, start_ref, dest_ref):
    th = route_ref.shape[0]
    route = route_ref[...]
    rank = rank_ref[...]
    lane_f = lax.broadcasted_iota(I32, (th, LANES), 1).astype(F32)
    start = start_ref[...]
    d0 = rank[:, 0:1] + jnp.sum(jnp.where(lane_f == route[:, 0:1], start, 0.0), axis=-1, keepdims=True)
    d1 = rank[:, 1:2] + jnp.sum(jnp.where(lane_f == route[:, 1:2], start, 0.0), axis=-1, keepdims=True)
    lane8 = lax.broadcasted_iota(I32, (th, ROUTE_W), 1)
    dest_ref[...] = jnp.where(lane8 == 0, d0, jnp.where(lane8 == 1, d1, 0.0)).astype(I32)


def _expert_dest(route, rank, start_row, th):
    t = route.shape[0]
    return pl.pallas_call(
        _dest_kernel,
        grid=(t // th,),
        in_specs=[pl.BlockSpec((th, ROUTE_W), lambda i: (i, 0)),
                  pl.BlockSpec((th, ROUTE_W), lambda i: (i, 0)),
                  pl.BlockSpec((1, LANES), lambda i: (0, 0))],
        out_specs=pl.BlockSpec((th, ROUTE_W), lambda i: (i, 0)),
        out_shape=jax.ShapeDtypeStruct((t, ROUTE_W), I32),
        compiler_params=_cparams(("arbitrary",)),
        name="expert_dest",
    )(route, rank, start_row)


def _dispatch_kernel(dest_hbm, h_ref, xs_in_hbm, xs_hbm, idx_smem, idx_sem, row_sem):
    del xs_in_hbm
    i = pl.program_id(0)
    td = h_ref.shape[0]
    idx_copy = pltpu.make_async_copy(dest_hbm.at[i], idx_smem, idx_sem)
    idx_copy.start()
    idx_copy.wait()

    def row_copy(r, dst_row):
        return pltpu.make_async_copy(h_ref.at[pl.ds(r, 1)], xs_hbm.at[pl.ds(dst_row, 1)], row_sem)

    for n in range(2 * td):
        row_copy(n // 2, idx_smem[n]).start(priority=n % 2)
    for n in range(2 * td):
        row_copy(n // 2, 0).wait()


def _dispatch(dest_tiles, h2, xs_init, td):
    t, d = h2.shape
    return pl.pallas_call(
        _dispatch_kernel,
        grid=(t // td,),
        in_specs=[pl.BlockSpec(memory_space=pl.ANY),
                  pl.BlockSpec((td, d), lambda i: (i, 0)),
                  pl.BlockSpec(memory_space=pl.ANY)],
        out_specs=pl.BlockSpec(memory_space=pl.ANY),
        out_shape=jax.ShapeDtypeStruct(xs_init.shape, xs_init.dtype),
        scratch_shapes=[pltpu.SMEM((2 * td,), I32), pltpu.SemaphoreType.DMA, pltpu.SemaphoreType.DMA],
        input_output_aliases={2: 0},
        compiler_params=_cparams(("arbitrary",)),
        name="moe_dispatch",
    )(dest_tiles, h2, xs_init)


def _expert_kernel(be_ref, nused_ref, xs_ref, w13_ref, w2_ref, ys_ref):
    @pl.when(pl.program_id(0) < nused_ref[0])
    def _():
        xb = xs_ref[...].astype(BF16)
        h13 = _dot(xb, w13_ref[0])
        hid = _silu(h13[:, :D_EXPERT]) * h13[:, D_EXPERT:]
        ys_ref[...] = _dot(hid.astype(BF16), w2_ref[0])

    @pl.when(pl.program_id(0) >= nused_ref[0])
    def _():
        ys_ref[...] = jnp.zeros_like(ys_ref)


def _experts(blk_expert, n_used, xs, w13, w2, blk):
    n_rows, d = xs.shape
    n_blocks = n_rows // blk
    row_map = lambda i, be, nu: (i, 0)
    grid_spec = pltpu.PrefetchScalarGridSpec(
        num_scalar_prefetch=2,
        grid=(n_blocks,),
        in_specs=[pl.BlockSpec((blk, d), row_map),
                  pl.BlockSpec((1, d, 2 * D_EXPERT), lambda i, be, nu: (be[i], 0, 0)),
                  pl.BlockSpec((1, D_EXPERT, d), lambda i, be, nu: (be[i], 0, 0))],
        out_specs=pl.BlockSpec((blk, d), row_map),
    )
    return pl.pallas_call(
        _expert_kernel,
        grid_spec=grid_spec,
        out_shape=jax.ShapeDtypeStruct((n_rows, d), F32),
        compiler_params=_cparams(("arbitrary",)),
        name="moe_experts",
    )(blk_expert, n_used, xs, w13, w2)


def _combine_kernel(dest_hbm, ys_hbm, x1_ref, route_ref, gt2_ref, l2g_ref, l2b_ref, o_ref,
                    idx_smem, buf_ref, idx_sem, row_sem):
    b_i = pl.program_id(0)
    i = pl.program_id(1)
    td = x1_ref.shape[1]
    tile = b_i * pl.num_programs(1) + i
    idx_copy = pltpu.make_async_copy(dest_hbm.at[tile], idx_smem, idx_sem)
    idx_copy.start()
    idx_copy.wait()

    def row_copy(n, src_row):
        return pltpu.make_async_copy(ys_hbm.at[pl.ds(src_row, 1)], buf_ref.at[n % 2, pl.ds(n // 2, 1)], row_sem)

    for n in range(2 * td):
        row_copy(n, idx_smem[n]).start(priority=n % 2)
    for n in range(2 * td):
        row_copy(n, 0).wait()
    route = route_ref[0]
    ffn = buf_ref[0] * route[:, 2:3] + buf_ref[1] * route[:, 3:4]
    o_ref[0] = _layer_norm(ALPHA * x1_ref[0] + gt2_ref[0] * ffn) * l2g_ref[...] + l2b_ref[...]


def _combine(dest_tiles, ys, x1, route3, gt2, l2g, l2b, td):
    b, s, d = x1.shape
    row = lambda w: pl.BlockSpec((1, td, w), lambda bi, i: (bi, i, 0))
    const = lambda shape: pl.BlockSpec(shape, lambda bi, i: (0,) * len(shape))
    return pl.pallas_call(
        _combine_kernel,
        grid=(b, s // td),
        in_specs=[pl.BlockSpec(memory_space=pl.ANY),
                  pl.BlockSpec(memory_space=pl.ANY),
                  row(d), row(ROUTE_W),
                  pl.BlockSpec((1, 1, d), lambda bi, i: (bi, 0, 0)),
                  const((1, d)), const((1, d))],
        out_specs=row(d),
        out_shape=jax.ShapeDtypeStruct((b, s, d), F32),
        scratch_shapes=[pltpu.SMEM((2 * td,), I32), pltpu.VMEM((2, td, d), F32),
                        pltpu.SemaphoreType.DMA, pltpu.SemaphoreType.DMA],
        compiler_params=_cparams(("arbitrary", "arbitrary")),
        name="moe_combine",
    )(dest_tiles, ys, x1, route3, gt2, l2g, l2b)


def _tile(n, pref):
    t = min(n, pref)
    assert n % t == 0, (n, t)
    return t


def _rope_tables(s):
    half = HEAD_DIM // 2
    inv = ROPE_THETA ** (-jnp.arange(0, half, 2, dtype=F32) / half)
    pos = jnp.arange(s)
    row = (pos // GRID_W).astype(F32)[:, None] * inv[None, :]
    col = (pos % GRID_W).astype(F32)[:, None] * inv[None, :]
    cos = jnp.concatenate([jnp.cos(row), jnp.cos(row), jnp.cos(col), jnp.cos(col)], axis=-1)
    sin = jnp.concatenate([-jnp.sin(row), jnp.sin(row), -jnp.sin(col), jnp.sin(col)], axis=-1)
    return cos, sin


def _layer(x, c, w_ada, b_ada, w_in, conv_w, a_log, dt_bias, gdn_norm_g, q_norm_g, k_norm_g, attn_norm_g,
           w_out, ln1_g, ln1_b, w_group, b_group, w_router, b_router, w1, w3, w2, ln2_g, ln2_b):
    b, s, d = x.shape
    t = b * s
    assert s % CHUNK == 0 and s % GRID_W == 0

    c_pad = jnp.pad(c, ((0, (-b) % 8), (0, 0)))
    mod = _adaln_mod(c_pad, w_ada, b_ada.reshape(1, -1), _tile(6 * d, 1536))[:b]
    mod3 = mod.reshape(b, 6, d)

    gq, gk, gv, gz, gab, aq, ak, av = jnp.split(
        w_in, [GDN_WIDTH, 2 * GDN_WIDTH, 3 * GDN_WIDTH, 4 * GDN_WIDTH, 4 * GDN_WIDTH + N_GATES,
               4 * GDN_WIDTH + N_GATES + ATTN_WIDTH, 4 * GDN_WIDTH + N_GATES + ATTN_WIDTH + ATTN_KV_WIDTH], axis=1)
    w_packed = jnp.concatenate([gq, gk, gv, gz, ak, jnp.pad(gab, ((0, 0), (0, LANES - N_GATES)))], axis=1).astype(BF16)
    wqv_t = jnp.concatenate([aq, av], axis=1).T.astype(BF16)
    wab_t = gab.T.astype(BF16)
    cos, sin = _rope_tables(s)

    tm = _tile(s, 512)
    gqkv, z, ab, abt, a_qt, a_k, a_vt = _in_projection(
        x, mod3, w_packed, wqv_t, wab_t, cos, sin, cos.T, sin.T, q_norm_g.reshape(-1, 1), k_norm_g.reshape(1, -1), tm)

    pad_gates = lambda p: jnp.pad(p.reshape(1, -1), ((0, 0), (0, N_GATES - p.size)))
    alog_c = pad_gates(a_log)
    dtb_c = pad_gates(dt_bias)
    tc = _tile(s, 512)
    gq_n, gk_n, gv_n, gb, egc, egl, ea, gcr = _gdn_prep(
        gqkv, conv_w, ab, abt, alog_c, dtb_c, alog_c.reshape(-1, 1), dtb_c.reshape(-1, 1), tc)
    a_m, b_m, a_dg, q_eff, o_in = _gdn_local(gq_n, gk_n, gv_n, gb, egc, egl, ea, gcr, _tile(s, 256))
    o_f, o_b = _gdn_scan(a_m, b_m, a_dg, q_eff, o_in, _tile(s // CHUNK, 4))

    att = _attention(a_qt, a_k, a_vt, _tile(s, 512), _tile(s, 2048))

    w_rt = jnp.pad(jnp.concatenate([w_group, w_router], axis=1), ((0, 0), (0, LANES - N_GROUPS - N_EXPERTS)))
    b_rt = jnp.pad(jnp.concatenate([b_group, b_router]).reshape(1, -1), ((0, 0), (0, LANES - N_GROUPS - N_EXPERTS)))
    x1, h2, route = _mixer_out(o_f, o_b, z, att, x, mod3, w_out.astype(BF16), gdn_norm_g.reshape(1, -1),
                               attn_norm_g.reshape(-1, 1), ln1_g.reshape(1, -1), ln1_b.reshape(1, -1),
                               w_rt, b_rt, _tile(s, 512))

    blk = 256
    th = _tile(t, 512)
    route2 = route.reshape(t, ROUTE_W)
    rank, counts = _expert_ranks(route2, th)
    counts_i = counts[0, :N_EXPERTS].astype(I32)
    padded = (counts_i + blk - 1) // blk * blk
    pad_end = jnp.cumsum(padded)
    pad_start = pad_end - padded
    n_blocks = -(-(2 * t) // blk) + N_EXPERTS
    blk_pos = jnp.arange(n_blocks, dtype=I32) * blk
    blk_expert = jnp.minimum(jnp.sum((pad_end[None, :] <= blk_pos[:, None]).astype(I32), axis=1), N_EXPERTS - 1)
    n_used = jnp.maximum(pad_end[-1:] // blk, 1).astype(I32)
    start_row = jnp.pad(pad_start.astype(F32).reshape(1, -1), ((0, 0), (0, LANES - N_EXPERTS)))
    dest = _expert_dest(route2, rank, start_row, th)

    td = _tile(s, 256)
    dest_tiles = dest[:, 0:2].reshape(t // td, 2 * td)
    xs = _dispatch(dest_tiles, h2.reshape(t, d), jnp.zeros((n_blocks * blk, d), F32), td)
    w13 = jnp.concatenate([w1, w3], axis=-1).astype(BF16)
    ys = _experts(blk_expert, n_used, xs, w13, w2.astype(BF16), blk)
    gt2 = mod3[:, 5:6, :]
    return _combine(dest_tiles, ys, x1, route, gt2, ln2_g.reshape(1, -1), ln2_b.reshape(1, -1), td)


def kernel(x, c, w_ada, b_ada, w_in, conv_w, a_log, dt_bias, gdn_norm_g, q_norm_g, k_norm_g, attn_norm_g,
           w_out, ln1_g, ln1_b, w_group, b_group, w_router, b_router, w1, w3, w2, ln2_g, ln2_b):
    for layer in range(w_ada.shape[0]):
        x = _layer(x, c, w_ada[layer], b_ada[layer], w_in[layer], conv_w[layer], a_log[layer], dt_bias[layer],
                   gdn_norm_g[layer], q_norm_g[layer], k_norm_g[layer], attn_norm_g[layer], w_out[layer],
                   ln1_g[layer], ln1_b[layer], w_group[layer], b_group[layer], w_router[layer], b_router[layer],
                   w1[layer], w3[layer], w2[layer], ln2_g[layer], ln2_b[layer])
    return x
```

```python
import functools
import math

import jax
import jax.numpy as jnp
from jax import lax
from jax.experimental import pallas as pl
from jax.experimental.pallas import tpu as pltpu

F32 = jnp.float32
BF16 = jnp.bfloat16
I32 = jnp.int32

HEAD_DIM = 128
GDN_HEADS = 4
GDN_WIDTH = GDN_HEADS * HEAD_DIM
ATTN_Q_HEADS = 4
ATTN_KV_HEADS = 2
ATTN_GROUP = ATTN_Q_HEADS // ATTN_KV_HEADS
ATTN_WIDTH = ATTN_Q_HEADS * HEAD_DIM
ATTN_KV_WIDTH = ATTN_KV_HEADS * HEAD_DIM
CONV_K = 5
CHUNK = 64
GRID_W = 64
ROPE_THETA = 10000.0
N_GROUPS = 4
EXPERTS_PER_GROUP = 8
N_EXPERTS = N_GROUPS * EXPERTS_PER_GROUP
D_EXPERT = 256
DEPTH = 1
ALPHA = (2.0 * DEPTH) ** 0.25
LN_EPS = 1e-5
RMS_EPS = 1e-6

LANES = 128
VMEM_LIMIT_BYTES = 56 * 1024 * 1024
NEG_BIG = -1e30


def _cparams(semantics):
    return pltpu.CompilerParams(dimension_semantics=semantics, vmem_limit_bytes=VMEM_LIMIT_BYTES)


def _dot(a, b):
    return jnp.dot(a, b, preferred_element_type=F32)


def _dot_nt(a, b):
    return lax.dot_general(a, b, (((1,), (1,)), ((), ())), preferred_element_type=F32)


def _dot_tn(a, b):
    return lax.dot_general(a, b, (((0,), (0,)), ((), ())), preferred_element_type=F32)


def _split3(a):
    hi = a.astype(BF16)
    r = a - hi.astype(F32)
    mid = r.astype(BF16)
    lo = (r - mid.astype(F32)).astype(BF16)
    return hi, mid, lo


def _dot_f32_lhs_exact(a_bf16_exact, b):
    hi, mid, lo = _split3(b)
    return _dot(a_bf16_exact, hi) + _dot(a_bf16_exact, mid) + _dot(a_bf16_exact, lo)


def _dot_f32_rhs_exact(a, b_bf16_exact):
    hi, mid, lo = _split3(a)
    return _dot(hi, b_bf16_exact) + _dot(mid, b_bf16_exact) + _dot(lo, b_bf16_exact)


def _dot_f32(a, b):
    ah, am, _ = _split3(a)
    bh, bm, _ = _split3(b)
    return _dot(ah, bh) + (_dot(ah, bm) + _dot(am, bh))


def _sigmoid(x):
    return 1.0 / (1.0 + jnp.exp(-x))


def _silu(x):
    return x * _sigmoid(x)


def _softplus(x):
    return jnp.maximum(x, 0.0) + jnp.log1p(jnp.exp(-jnp.abs(x)))


def _layer_norm(x):
    mu = jnp.mean(x, axis=-1, keepdims=True)
    xc = x - mu
    var = jnp.mean(xc * xc, axis=-1, keepdims=True)
    return xc * lax.rsqrt(var + LN_EPS)


def _mod_kernel(c_ref, w_ref, b_ref, o_ref):
    o_ref[...] = _dot_f32(_silu(c_ref[...]), w_ref[...]) + b_ref[...]


def _adaln_mod(c_pad, w_ada, b_ada, tn):
    rows, d = c_pad.shape
    n = w_ada.shape[1]
    return pl.pallas_call(
        _mod_kernel,
        grid=(n // tn,),
        in_specs=[pl.BlockSpec((rows, d), lambda j: (0, 0)),
                  pl.BlockSpec((d, tn), lambda j: (0, j)),
                  pl.BlockSpec((1, tn), lambda j: (0, j))],
        out_specs=pl.BlockSpec((rows, tn), lambda j: (0, j)),
        out_shape=jax.ShapeDtypeStruct((rows, n), F32),
        compiler_params=_cparams(("arbitrary",)),
        name="adaln_mod",
    )(c_pad, w_ada, b_ada)


_W_GQKV = 0
_W_Z = 3 * GDN_WIDTH
_W_AK = _W_Z + GDN_WIDTH
_W_AB = _W_AK + ATTN_KV_WIDTH
_W_COLS = _W_AB + LANES
N_GATES = 4 * GDN_HEADS
LOG2E = math.log2(math.e)


def _rope(xh, cos, sin_signed, lane):
    fwd = pltpu.roll(xh, 32, 1)
    bwd = pltpu.roll(xh, LANES - 32, 1)
    partner = jnp.where((lane % 64) < 32, bwd, fwd)
    return xh * cos + partner * sin_signed


def _rope_t(xt, cos_t, sin_signed_t):
    q = HEAD_DIM // 4
    partner = jnp.concatenate([xt[q:2 * q], xt[0:q], xt[3 * q:4 * q], xt[2 * q:3 * q]], axis=0)
    return xt * cos_t + partner * sin_signed_t


def _inproj_kernel(x_ref, mod_ref, w_ref, wqvt_ref, wabt_ref, cos_ref, sin_ref, cost_ref, sint_ref, qg_ref, kg_ref,
                   gqkv_ref, z_ref, ab_ref, abt_ref, aqt_ref, ak_ref, avt_ref):
    x = x_ref[0]
    sh1 = mod_ref[0, 0:1, :]
    sc1 = mod_ref[0, 1:2, :]
    h = _layer_norm(x) * (1.0 + sc1) + sh1
    hb = h.astype(BF16)
    gqkv_ref[0] = _dot(hb, w_ref[:, _W_GQKV:_W_Z]).astype(BF16)
    z_ref[0] = _dot(hb, w_ref[:, _W_Z:_W_AK]).astype(BF16)
    ab_ref[0] = _dot(hb, w_ref[:, _W_AB:_W_COLS])[:, 0:N_GATES]
    abt_ref[0] = _dot_nt(wabt_ref[...], hb)
    ak = _dot(hb, w_ref[:, _W_AK:_W_AB])
    cos = cos_ref[...]
    sin = sin_ref[...]
    lane = lax.broadcasted_iota(I32, cos.shape, 1)
    for j in range(ATTN_KV_HEADS):
        xh = ak[:, j * HEAD_DIM:(j + 1) * HEAD_DIM]
        xn = xh * lax.rsqrt(jnp.mean(xh * xh, axis=-1, keepdims=True) + RMS_EPS) * kg_ref[...]
        ak_ref[0, :, j * HEAD_DIM:(j + 1) * HEAD_DIM] = _rope(xn, cos, sin, lane).astype(BF16)
    qvt = _dot_nt(wqvt_ref[...], hb)
    cos_t = cost_ref[...]
    sin_t = sint_ref[...]
    q_scale = (HEAD_DIM ** -0.5) * LOG2E
    for i in range(ATTN_Q_HEADS):
        xt = qvt[i * HEAD_DIM:(i + 1) * HEAD_DIM, :]
        xn = xt * lax.rsqrt(jnp.mean(xt * xt, axis=0, keepdims=True) + RMS_EPS) * qg_ref[...]
        aqt_ref[0, i * HEAD_DIM:(i + 1) * HEAD_DIM, :] = (_rope_t(xn, cos_t, sin_t) * q_scale).astype(BF16)
    avt_ref[0] = qvt[ATTN_WIDTH:, :].astype(BF16)


def _in_projection(x, mod3, w_packed, wqv_t, wab_t, cos, sin, cos_t, sin_t, qg_col, kg_row, tm):
    b, s, d = x.shape
    grid = (b, s // tm)
    row = lambda w: pl.BlockSpec((1, tm, w), lambda bi, i: (bi, i, 0))
    col = lambda h: pl.BlockSpec((1, h, tm), lambda bi, i: (bi, 0, i))
    const = lambda shape: pl.BlockSpec(shape, lambda bi, i: (0,) * len(shape))
    out_shapes = [
        jax.ShapeDtypeStruct((b, s, 3 * GDN_WIDTH), BF16),
        jax.ShapeDtypeStruct((b, s, GDN_WIDTH), BF16),
        jax.ShapeDtypeStruct((b, s, N_GATES), F32),
        jax.ShapeDtypeStruct((b, N_GATES, s), F32),
        jax.ShapeDtypeStruct((b, ATTN_WIDTH, s), BF16),
        jax.ShapeDtypeStruct((b, s, ATTN_KV_WIDTH), BF16),
        jax.ShapeDtypeStruct((b, ATTN_KV_WIDTH, s), BF16),
    ]
    out_specs = [row(3 * GDN_WIDTH), row(GDN_WIDTH), row(N_GATES), col(N_GATES),
                 col(ATTN_WIDTH), row(ATTN_KV_WIDTH), col(ATTN_KV_WIDTH)]
    return pl.pallas_call(
        _inproj_kernel,
        grid=grid,
        in_specs=[row(d),
                  pl.BlockSpec((1, 6, d), lambda bi, i: (bi, 0, 0)),
                  const((d, _W_COLS)),
                  const((ATTN_WIDTH + ATTN_KV_WIDTH, d)),
                  const((N_GATES, d)),
                  pl.BlockSpec((tm, HEAD_DIM), lambda bi, i: (i, 0)),
                  pl.BlockSpec((tm, HEAD_DIM), lambda bi, i: (i, 0)),
                  pl.BlockSpec((HEAD_DIM, tm), lambda bi, i: (0, i)),
                  pl.BlockSpec((HEAD_DIM, tm), lambda bi, i: (0, i)),
                  const((HEAD_DIM, 1)),
                  const((1, HEAD_DIM))],
        out_specs=out_specs,
        out_shape=out_shapes,
        compiler_params=_cparams(("arbitrary", "arbitrary")),
        name="in_projection",
    )(x, mod3, w_packed, wqv_t, wab_t, cos, sin, cos_t, sin_t, qg_col, kg_row)


_HALO = 16


def _chunk_masks(n, transpose=False):
    r = lax.broadcasted_iota(I32, (n, n), 0)
    c = lax.broadcasted_iota(I32, (n, n), 1)
    same = (r // CHUNK) == (c // CHUNK)
    lower = same & (r >= c)
    upper = same & (r <= c)
    return lower, upper


def _gdn_prep_kernel(prev_ref, main_ref, next_ref, convw_ref, ab_ref, abt_ref,
                     alog_c_ref, dtb_c_ref, alog_r_ref, dtb_r_ref,
                     q_ref, k_ref, v_ref, gb_ref, egc_ref, egl_ref, ea_ref, gcr_ref, buf_ref):
    i = pl.program_id(1)
    n_i = pl.num_programs(1)
    tc = main_ref.shape[1]
    prev_scale = jnp.where(i > 0, 1.0, 0.0).astype(F32)
    next_scale = jnp.where(i < n_i - 1, 1.0, 0.0).astype(F32)
    buf_ref[0:_HALO, :] = prev_ref[0].astype(F32) * prev_scale
    buf_ref[_HALO:_HALO + tc, :] = main_ref[0].astype(F32)
    buf_ref[_HALO + tc:, :] = next_ref[0].astype(F32) * next_scale
    pad = CONV_K // 2
    for part, out_ref in enumerate((q_ref, k_ref, v_ref)):
        for hh in range(GDN_HEADS):
            c0 = part * GDN_WIDTH + hh * HEAD_DIM
            acc = None
            for j in range(CONV_K):
                term = buf_ref[_HALO - pad + j:_HALO - pad + j + tc, c0:c0 + HEAD_DIM] * convw_ref[j:j + 1, c0:c0 + HEAD_DIM]
                acc = term if acc is None else acc + term
            y = _silu(acc)
            if part < 2:
                y = y * lax.rsqrt(jnp.sum(y * y, axis=-1, keepdims=True) + RMS_EPS)
            if part == 0:
                y = y * (HEAD_DIM ** -0.5)
            out_ref[0, :, hh * HEAD_DIM:(hh + 1) * HEAD_DIM] = y.astype(BF16)

    lower, upper = _chunk_masks(tc)
    lower_b = jnp.where(lower, 1.0, 0.0).astype(BF16)
    upper_b = jnp.where(upper, 1.0, 0.0).astype(BF16)
    n_dir = 2 * GDN_HEADS
    ab = ab_ref[0]
    lane = lax.broadcasted_iota(I32, ab.shape, 1)
    g = -jnp.exp(alog_c_ref[...]) * _softplus(ab + dtb_c_ref[...])
    pre = _dot_f32_lhs_exact(lower_b, g)
    suf = _dot_f32_lhs_exact(upper_b, g)
    gtot = pre + suf - g
    gc = jnp.where(lane < GDN_HEADS, pre, suf)
    gb_ref[0] = jnp.where(lane < n_dir, gc, _sigmoid(ab))
    egc_ref[0] = jnp.exp(gc)
    egl_ref[0] = jnp.exp(gtot - gc)
    ea_ref[0] = jnp.exp(gtot)
    abt = abt_ref[0]
    row = lax.broadcasted_iota(I32, abt.shape, 0)
    g_r = -jnp.exp(alog_r_ref[...]) * _softplus(abt + dtb_r_ref[...])
    pre_r = _dot_f32_rhs_exact(g_r, upper_b)
    suf_r = _dot_f32_rhs_exact(g_r, lower_b)
    gcr_ref[0] = jnp.where(row < GDN_HEADS, pre_r, suf_r)[0:n_dir]


def _gdn_prep(gqkv, conv_w, ab, abt, alog_c, dtb_c, alog_r, dtb_r, tc):
    b, s, w = gqkv.shape
    nh = tc // _HALO
    n_dir = 2 * GDN_HEADS
    grid = (b, s // tc)
    row = lambda width: pl.BlockSpec((1, tc, width), lambda bi, i: (bi, i, 0))
    const = lambda shape: pl.BlockSpec(shape, lambda bi, i: (0,) * len(shape))
    last_halo = s // _HALO - 1
    out_shapes = ([jax.ShapeDtypeStruct((b, s, GDN_WIDTH), BF16)] * 3
                  + [jax.ShapeDtypeStruct((b, s, N_GATES), F32)] * 4
                  + [jax.ShapeDtypeStruct((b, n_dir, s), F32)])
    out_specs = ([row(GDN_WIDTH)] * 3 + [row(N_GATES)] * 4
                 + [pl.BlockSpec((1, n_dir, tc), lambda bi, i: (bi, 0, i))])
    return pl.pallas_call(
        _gdn_prep_kernel,
        grid=grid,
        in_specs=[pl.BlockSpec((1, _HALO, w), lambda bi, i: (bi, jnp.maximum(i * nh - 1, 0), 0)),
                  row(w),
                  pl.BlockSpec((1, _HALO, w), lambda bi, i: (bi, jnp.minimum((i + 1) * nh, last_halo), 0)),
                  const((CONV_K, w)),
                  row(N_GATES),
                  pl.BlockSpec((1, N_GATES, tc), lambda bi, i: (bi, 0, i)),
                  const((1, N_GATES)), const((1, N_GATES)),
                  const((N_GATES, 1)), const((N_GATES, 1))],
        out_specs=out_specs,
        out_shape=out_shapes,
        scratch_shapes=[pltpu.VMEM((tc + 2 * _HALO, w), F32)],
        compiler_params=_cparams(("arbitrary", "arbitrary")),
        name="gdn_prep",
    )(gqkv, gqkv, gqkv, conv_w, ab, abt, alog_c, dtb_c, alog_r, dtb_r)


def _gdn_local_kernel(q_ref, k_ref, v_ref, gb_ref, egc_ref, egl_ref, ea_ref, gcr_ref,
                      a_ref, b_ref, dg_ref, qe_ref, oi_ref):
    tc = q_ref.shape[1]
    nch = tc // CHUNK
    lower, upper = _chunk_masks(tc)
    r = lax.broadcasted_iota(I32, (tc, tc), 0)
    c = lax.broadcasted_iota(I32, (tc, tc), 1)
    eye = r == c
    blk_xor = r ^ c
    n_dir = 2 * GDN_HEADS
    combos = [(d, hh) for d in range(2) for hh in range(GDN_HEADS)]
    head = lambda hh: slice(hh * HEAD_DIM, (hh + 1) * HEAD_DIM)
    kk = [_dot_nt(k_ref[0, :, head(hh)], k_ref[0, :, head(hh)]) for hh in range(GDN_HEADS)]
    qk = [_dot_nt(q_ref[0, :, head(hh)], k_ref[0, :, head(hh)]) for hh in range(GDN_HEADS)]
    decay, m, t = {}, {}, {}
    for cb in combos:
        d, hh = cb
        idx = d * GDN_HEADS + hh
        incl = lower if d == 0 else upper
        gc = gb_ref[0, :, idx:idx + 1]
        beta = gb_ref[0, :, n_dir + idx:n_dir + idx + 1]
        gcr = gcr_ref[0, idx:idx + 1, :]
        decay[cb] = jnp.where(incl, jnp.exp(jnp.minimum(gc - gcr, 0.0)), 0.0)
        m[cb] = jnp.where(eye, 0.0, beta * kk[hh] * decay[cb])
        t[cb] = jnp.where(eye, 1.0, 0.0) - jnp.where(blk_xor < 2, m[cb], 0.0)
    sz = 2
    while sz < CHUNK:
        join = (blk_xor >= sz) & (blk_xor < 2 * sz)
        tb = {cb: t[cb].astype(BF16) for cb in combos}
        tc_s = {cb: _dot(tb[cb], jnp.where(join, m[cb], 0.0).astype(BF16)).astype(BF16) for cb in combos}
        for cb in combos:
            t[cb] = t[cb] - _dot(tc_s[cb], tb[cb])
        sz *= 2
    solb = {}
    for cb in combos:
        d, hh = cb
        idx = d * GDN_HEADS + hh
        beta = gb_ref[0, :, n_dir + idx:n_dir + idx + 1]
        egc = egc_ref[0, :, idx:idx + 1]
        rhs = jnp.concatenate([v_ref[0, :, head(hh)].astype(F32) * beta,
                               k_ref[0, :, head(hh)].astype(F32) * (beta * egc)], axis=1).astype(BF16)
        solb[cb] = _dot(t[cb].astype(BF16), rhs).astype(BF16)
    for cb in combos:
        d, hh = cb
        idx = d * GDN_HEADS + hh
        egc = egc_ref[0, :, idx:idx + 1]
        qo = _dot((qk[hh] * decay[cb]).astype(BF16), solb[cb])
        oi_ref[0, d, :, head(hh)] = qo[:, :HEAD_DIM].astype(BF16)
        qe_ref[0, d, :, head(hh)] = (q_ref[0, :, head(hh)].astype(F32) * egc - qo[:, HEAD_DIM:]).astype(BF16)
    for cb in combos:
        d, hh = cb
        idx = d * GDN_HEADS + hh
        egl = egl_ref[0, :, idx:idx + 1]
        ea = ea_ref[0, :, idx:idx + 1]
        kg = (k_ref[0, :, head(hh)].astype(F32) * egl).astype(BF16)
        for ci in range(nch):
            rs = slice(ci * CHUNK, (ci + 1) * CHUNK)
            ab = _dot_tn(kg[rs], solb[cb][rs])
            a_ref[0, d, ci, :, head(hh)] = (-ab[:, HEAD_DIM:]).astype(BF16)
            b_ref[0, d, ci, :, head(hh)] = ab[:, :HEAD_DIM].astype(BF16)
            dg_ref[0, d, ci, :, head(hh)] = jnp.broadcast_to(ea[ci * CHUNK:ci * CHUNK + 1, :], (8, HEAD_DIM))


def _gdn_local(q, k, v, gb, egc, egl, ea, gcr, tc):
    b, s, w = q.shape
    n = s // CHUNK
    nch = tc // CHUNK
    n_dir = 2 * GDN_HEADS
    grid = (b, s // tc)
    row = lambda width: pl.BlockSpec((1, tc, width), lambda bi, i: (bi, i, 0))
    return pl.pallas_call(
        _gdn_local_kernel,
        grid=grid,
        in_specs=[row(w), row(w), row(w), row(N_GATES), row(N_GATES), row(N_GATES), row(N_GATES),
                  pl.BlockSpec((1, n_dir, tc), lambda bi, i: (bi, 0, i))],
        out_specs=[pl.BlockSpec((1, 2, nch, HEAD_DIM, w), lambda bi, i: (bi, 0, i, 0, 0)),
                   pl.BlockSpec((1, 2, nch, HEAD_DIM, w), lambda bi, i: (bi, 0, i, 0, 0)),
                   pl.BlockSpec((1, 2, nch, 8, w), lambda bi, i: (bi, 0, i, 0, 0)),
                   pl.BlockSpec((1, 2, tc, w), lambda bi, i: (bi, 0, i, 0)),
                   pl.BlockSpec((1, 2, tc, w), lambda bi, i: (bi, 0, i, 0))],
        out_shape=[jax.ShapeDtypeStruct((b, 2, n, HEAD_DIM, w), BF16),
                   jax.ShapeDtypeStruct((b, 2, n, HEAD_DIM, w), BF16),
                   jax.ShapeDtypeStruct((b, 2, n, 8, w), F32),
                   jax.ShapeDtypeStruct((b, 2, s, w), BF16),
                   jax.ShapeDtypeStruct((b, 2, s, w), BF16)],
        compiler_params=_cparams(("arbitrary", "arbitrary")),
        name="gdn_local",
    )(q, k, v, gb, egc, egl, ea, gcr)


def _gdn_scan_kernel(af_ref, bf_ref, df_ref, qf_ref, of_ref, ab_ref, bb_ref, db_ref, qb_ref, ob_ref,
                     outf_ref, outb_ref, s_ref):
    @pl.when(pl.program_id(1) == 0)
    def _():
        s_ref[...] = jnp.zeros_like(s_ref)

    cs = af_ref.shape[2]
    dirs = ((af_ref, bf_ref, df_ref, qf_ref, of_ref, outf_ref), (ab_ref, bb_ref, db_ref, qb_ref, ob_ref, outb_ref))
    for j in range(cs):
        for d, (a_ref, b_ref, dg_ref, q_ref, o_ref, out_ref) in enumerate(dirs):
            ci = j if d == 0 else cs - 1 - j
            rs = slice(ci * CHUNK, (ci + 1) * CHUNK)
            for hh in range(GDN_HEADS):
                hs = slice(hh * HEAD_DIM, (hh + 1) * HEAD_DIM)
                si = d * GDN_HEADS + hh
                st = s_ref[si]
                stb = st.astype(BF16)
                out_ref[0, rs, hs] = _dot(q_ref[0, 0, rs, hs], stb) + o_ref[0, 0, rs, hs].astype(F32)
                s_ref[si] = (dg_ref[0, 0, ci, 0:1, hs] * st + _dot(a_ref[0, 0, ci, :, hs], stb)
                             + b_ref[0, 0, ci, :, hs].astype(F32))


def _gdn_scan(a, bm, dg, qe, oi, cs):
    b, _, n, _, w = a.shape
    s = n * CHUNK
    ns = n // cs
    tr = cs * CHUNK
    fwd5 = lambda bi, i: (bi, 0, i, 0, 0)
    bwd5 = lambda bi, i: (bi, 1, ns - 1 - i, 0, 0)
    fwd4 = lambda bi, i: (bi, 0, i, 0)
    bwd4 = lambda bi, i: (bi, 1, ns - 1 - i, 0)
    blk5 = (1, 1, cs, HEAD_DIM, w)
    blkd = (1, 1, cs, 8, w)
    blk4 = (1, 1, tr, w)
    return pl.pallas_call(
        _gdn_scan_kernel,
        grid=(b, ns),
        in_specs=[pl.BlockSpec(blk5, fwd5), pl.BlockSpec(blk5, fwd5), pl.BlockSpec(blkd, fwd5),
                  pl.BlockSpec(blk4, fwd4), pl.BlockSpec(blk4, fwd4),
                  pl.BlockSpec(blk5, bwd5), pl.BlockSpec(blk5, bwd5), pl.BlockSpec(blkd, bwd5),
                  pl.BlockSpec(blk4, bwd4), pl.BlockSpec(blk4, bwd4)],
        out_specs=[pl.BlockSpec((1, tr, w), lambda bi, i: (bi, i, 0)),
                   pl.BlockSpec((1, tr, w), lambda bi, i: (bi, ns - 1 - i, 0))],
        out_shape=[jax.ShapeDtypeStruct((b, s, w), F32)] * 2,
        scratch_shapes=[pltpu.VMEM((2 * GDN_HEADS, HEAD_DIM, HEAD_DIM), F32)],
        compiler_params=_cparams(("arbitrary", "arbitrary")),
        name="gdn_scan",
    )(a, bm, dg, qe, oi, a, bm, dg, qe, oi)


ATTN_SUB = 512
ATTN_LAG_LIMIT = 64.0


def _attn_lagged_kernel(qt_ref, k_ref, vt_ref, ot_ref, ex_ref, q2t_ref, m_ref, l_ref, acc_ref, exc_ref):
    j = pl.program_id(3)
    tq = qt_ref.shape[2]
    tk = k_ref.shape[1]

    @pl.when(j == 0)
    def _():
        for gi in range(ATTN_GROUP):
            q2t_ref[:, gi * tq:(gi + 1) * tq] = qt_ref[0, gi * HEAD_DIM:(gi + 1) * HEAD_DIM, :]
        m_ref[...] = jnp.max(_dot(k_ref[0, 0:ATTN_SUB, :], q2t_ref[...]), axis=0, keepdims=True)
        l_ref[...] = jnp.zeros_like(l_ref)
        acc_ref[...] = jnp.zeros_like(acc_ref)
        exc_ref[...] = jnp.zeros_like(exc_ref)

    q2t = q2t_ref[...]
    m_est = m_ref[...]
    m_run = m_est
    l_run = l_ref[...]
    n_sub = tk // ATTN_SUB
    scores = lambda jj: _dot(k_ref[0, jj * ATTN_SUB:(jj + 1) * ATTN_SUB, :], q2t)
    st_next = scores(0)
    for jj in range(n_sub):
        ks = slice(jj * ATTN_SUB, (jj + 1) * ATTN_SUB)
        st = st_next
        if jj + 1 < n_sub:
            st_next = scores(jj + 1)
        p = jnp.exp2(st - m_est)
        m_run = jnp.maximum(m_run, jnp.max(st, axis=0, keepdims=True))
        l_run = l_run + jnp.sum(p, axis=0, keepdims=True)
        acc_ref[...] += _dot(vt_ref[0, :, ks], p.astype(BF16))
    rebase = jnp.exp2(m_est - m_run)
    acc_ref[...] = acc_ref[...] * rebase
    l_ref[...] = l_run * rebase
    m_ref[...] = m_run
    exc_ref[...] = jnp.maximum(exc_ref[...], m_run - m_est)

    @pl.when(j == pl.num_programs(3) - 1)
    def _():
        out = acc_ref[...] / l_ref[...]
        for gi in range(ATTN_GROUP):
            ot_ref[0, gi * HEAD_DIM:(gi + 1) * HEAD_DIM, :] = out[:, gi * tq:(gi + 1) * tq].astype(BF16)
        ex_ref[0] = exc_ref[...]


def _attention_lagged(q_t, k, v_t, tq, tk):
    b, _, s = q_t.shape
    gw = ATTN_GROUP * HEAD_DIM
    nq = s // tq
    assert tk % ATTN_SUB == 0
    return pl.pallas_call(
        _attn_lagged_kernel,
        grid=(b, ATTN_KV_HEADS, nq, s // tk),
        in_specs=[pl.BlockSpec((1, gw, tq), lambda bi, g, i, j: (bi, g, i)),
                  pl.BlockSpec((1, tk, HEAD_DIM), lambda bi, g, i, j: (bi, j, g)),
                  pl.BlockSpec((1, HEAD_DIM, tk), lambda bi, g, i, j: (bi, g, j))],
        out_specs=[pl.BlockSpec((1, gw, tq), lambda bi, g, i, j: (bi, g, i)),
                   pl.BlockSpec((1, 1, ATTN_GROUP * tq), lambda bi, g, i, j: ((bi * ATTN_KV_HEADS + g) * nq + i, 0, 0))],
        out_shape=[jax.ShapeDtypeStruct((b, ATTN_WIDTH, s), BF16),
                   jax.ShapeDtypeStruct((b * ATTN_KV_HEADS * nq, 1, ATTN_GROUP * tq), F32)],
        scratch_shapes=[pltpu.VMEM((HEAD_DIM, ATTN_GROUP * tq), BF16),
                        pltpu.VMEM((1, ATTN_GROUP * tq), F32),
                        pltpu.VMEM((1, ATTN_GROUP * tq), F32),
                        pltpu.VMEM((HEAD_DIM, ATTN_GROUP * tq), F32),
                        pltpu.VMEM((1, ATTN_GROUP * tq), F32)],
        compiler_params=_cparams(("arbitrary", "arbitrary", "arbitrary", "arbitrary")),
        name="gqa_attention_lagged",
    )(q_t, k, v_t)


def _attn_kernel(qt_ref, k_ref, vt_ref, ot_ref, q2t_ref, m_ref, l_ref, acc_ref):
    j = pl.program_id(3)
    tq = qt_ref.shape[2]
    tk = k_ref.shape[1]

    @pl.when(j == 0)
    def _():
        for gi in range(ATTN_GROUP):
            q2t_ref[:, gi * tq:(gi + 1) * tq] = qt_ref[0, gi * HEAD_DIM:(gi + 1) * HEAD_DIM, :]
        m_ref[...] = jnp.full_like(m_ref, NEG_BIG)
        l_ref[...] = jnp.zeros_like(l_ref)
        acc_ref[...] = jnp.zeros_like(acc_ref)

    q2t = q2t_ref[...]
    m_prev = m_ref[...]
    l_prev = l_ref[...]
    n_sub = tk // ATTN_SUB
    scores = lambda jj: _dot(k_ref[0, jj * ATTN_SUB:(jj + 1) * ATTN_SUB, :], q2t)
    st_next = scores(0)
    for jj in range(n_sub):
        ks = slice(jj * ATTN_SUB, (jj + 1) * ATTN_SUB)
        st = st_next
        if jj + 1 < n_sub:
            st_next = scores(jj + 1)
        m_new = jnp.maximum(m_prev, jnp.max(st, axis=0, keepdims=True))
        alpha = jnp.exp2(m_prev - m_new)
        p = jnp.exp2(st - m_new)
        l_prev = alpha * l_prev + jnp.sum(p, axis=0, keepdims=True)
        acc_ref[...] = alpha * acc_ref[...] + _dot(vt_ref[0, :, ks], p.astype(BF16))
        m_prev = m_new
    m_ref[...] = m_prev
    l_ref[...] = l_prev

    @pl.when(j == pl.num_programs(3) - 1)
    def _():
        out = acc_ref[...] / l_ref[...]
        for gi in range(ATTN_GROUP):
            ot_ref[0, gi * HEAD_DIM:(gi + 1) * HEAD_DIM, :] = out[:, gi * tq:(gi + 1) * tq].astype(BF16)


def _attention(q_t, k, v_t, tq, tk):
    b, _, s = q_t.shape
    gw = ATTN_GROUP * HEAD_DIM
    assert tk % ATTN_SUB == 0
    return pl.pallas_call(
        _attn_kernel,
        grid=(b, ATTN_KV_HEADS, s // tq, s // tk),
        in_specs=[pl.BlockSpec((1, gw, tq), lambda bi, g, i, j: (bi, g, i)),
                  pl.BlockSpec((1, tk, HEAD_DIM), lambda bi, g, i, j: (bi, j, g)),
                  pl.BlockSpec((1, HEAD_DIM, tk), lambda bi, g, i, j: (bi, g, j))],
        out_specs=pl.BlockSpec((1, gw, tq), lambda bi, g, i, j: (bi, g, i)),
        out_shape=jax.ShapeDtypeStruct((b, ATTN_WIDTH, s), BF16),
        scratch_shapes=[pltpu.VMEM((HEAD_DIM, ATTN_GROUP * tq), BF16),
                        pltpu.VMEM((1, ATTN_GROUP * tq), F32),
                        pltpu.VMEM((1, ATTN_GROUP * tq), F32),
                        pltpu.VMEM((HEAD_DIM, ATTN_GROUP * tq), F32)],
        compiler_params=_cparams(("arbitrary", "arbitrary", "arbitrary", "arbitrary")),
        name="gqa_attention",
    )(q_t, k, v_t)


ROUTE_W = 8


def _first_argmax(vals, lane_f, valid):
    vmax = jnp.max(jnp.where(valid, vals, NEG_BIG), axis=-1, keepdims=True)
    idx = jnp.min(jnp.where(valid & (vals == vmax), lane_f, float(LANES)), axis=-1, keepdims=True)
    return vmax, idx


def _mix_kernel(of_ref, ob_ref, z_ref, att_ref, x_ref, mod_ref, wout_ref, gng_ref, ang_ref,
                l1g_ref, l1b_ref, wrt_ref, brt_ref, x1_ref, h2_ref, route_ref):
    o = of_ref[0] + ob_ref[0]
    z = z_ref[0].astype(F32)
    parts = []
    for hh in range(GDN_HEADS):
        hs = slice(hh * HEAD_DIM, (hh + 1) * HEAD_DIM)
        oh = o[:, hs]
        on = oh * lax.rsqrt(jnp.mean(oh * oh, axis=-1, keepdims=True) + RMS_EPS) * gng_ref[...]
        parts.append((on * _silu(z[:, hs])).astype(BF16))
    att_t = att_ref[0].astype(F32)
    attn_t = (att_t * lax.rsqrt(jnp.mean(att_t * att_t, axis=0, keepdims=True) + RMS_EPS) * ang_ref[...]).astype(BF16)
    mixed = _dot(jnp.concatenate(parts, axis=1), wout_ref[:GDN_WIDTH, :]) + _dot_tn(attn_t, wout_ref[GDN_WIDTH:, :])
    gt1 = mod_ref[0, 2:3, :]
    sh2 = mod_ref[0, 3:4, :]
    sc2 = mod_ref[0, 4:5, :]
    x1 = _layer_norm(ALPHA * x_ref[0] + gt1 * mixed) * l1g_ref[...] + l1b_ref[...]
    x1_ref[0] = x1
    h2 = _layer_norm(x1) * (1.0 + sc2) + sh2
    h2_ref[0] = h2
    logits = _dot_f32(h2, wrt_ref[...]) + brt_ref[...]
    lane = lax.broadcasted_iota(I32, logits.shape, 1)
    lane_f = lane.astype(F32)
    is_grp = lane < N_GROUPS
    gmax, gidx = _first_argmax(logits, lane_f, is_grp)
    gsum = jnp.sum(jnp.where(is_grp, jnp.exp(jnp.minimum(logits - gmax, 0.0)), 0.0), axis=-1, keepdims=True)
    grp_p = 1.0 / gsum
    lo = float(N_GROUPS) + float(EXPERTS_PER_GROUP) * gidx
    in_grp = (lane_f >= lo) & (lane_f < lo + float(EXPERTS_PER_GROUP))
    v0, i0 = _first_argmax(logits, lane_f, in_grp)
    v1, i1 = _first_argmax(logits, lane_f, in_grp & (lane_f != i0))
    e1 = jnp.exp(v1 - v0)
    w0 = grp_p / (1.0 + e1)
    w1 = grp_p * e1 / (1.0 + e1)
    route = jnp.where(lane == 0, i0 - float(N_GROUPS),
                      jnp.where(lane == 1, i1 - float(N_GROUPS),
                                jnp.where(lane == 2, w0, jnp.where(lane == 3, w1, 0.0))))
    route_ref[0] = route[:, 0:ROUTE_W]


def _mixer_out(o_f, o_b, z, att, x, mod3, w_out, gng, ang, l1g, l1b, w_rt, b_rt, tm):
    b, s, d = x.shape
    row = lambda w: pl.BlockSpec((1, tm, w), lambda bi, i: (bi, i, 0))
    const = lambda shape: pl.BlockSpec(shape, lambda bi, i: (0,) * len(shape))
    return pl.pallas_call(
        _mix_kernel,
        grid=(b, s // tm),
        in_specs=[row(GDN_WIDTH), row(GDN_WIDTH), row(GDN_WIDTH),
                  pl.BlockSpec((1, ATTN_WIDTH, tm), lambda bi, i: (bi, 0, i)), row(d),
                  pl.BlockSpec((1, 6, d), lambda bi, i: (bi, 0, 0)),
                  const(w_out.shape), const((1, HEAD_DIM)), const((ATTN_WIDTH, 1)),
                  const((1, d)), const((1, d)), const((d, LANES)), const((1, LANES))],
        out_specs=[row(d), row(d), row(ROUTE_W)],
        out_shape=[jax.ShapeDtypeStruct((b, s, d), F32),
                   jax.ShapeDtypeStruct((b, s, d), F32),
                   jax.ShapeDtypeStruct((b, s, ROUTE_W), F32)],
        compiler_params=_cparams(("arbitrary", "arbitrary")),
        name="mixer_out",
    )(o_f, o_b, z, att, x, mod3, w_out, gng, ang, l1g, l1b, w_rt, b_rt)


def _rank_kernel(route_ref, rank_ref, cnt_ref, carry_ref):
    @pl.when(pl.program_id(0) == 0)
    def _():
        carry_ref[...] = jnp.zeros_like(carry_ref)

    th = route_ref.shape[0]
    route = route_ref[...]
    lane_f = lax.broadcasted_iota(I32, (th, LANES), 1).astype(F32)
    oh0 = lane_f == route[:, 0:1]
    oh1 = lane_f == route[:, 1:2]
    both = jnp.where(oh0 | oh1, 1.0, 0.0).astype(BF16)
    r = lax.broadcasted_iota(I32, (th, th), 0)
    c = lax.broadcasted_iota(I32, (th, th), 1)
    before = _dot(jnp.where(r > c, 1.0, 0.0).astype(BF16), both) + carry_ref[...]
    rank0 = jnp.sum(jnp.where(oh0, before, 0.0), axis=-1, keepdims=True)
    rank1 = jnp.sum(jnp.where(oh1, before, 0.0), axis=-1, keepdims=True)
    lane8 = lax.broadcasted_iota(I32, (th, ROUTE_W), 1)
    rank_ref[...] = jnp.where(lane8 == 0, rank0, jnp.where(lane8 == 1, rank1, 0.0))
    total = carry_ref[...] + jnp.sum(both.astype(F32), axis=0, keepdims=True)
    carry_ref[...] = total
    cnt_ref[...] = total


def _expert_ranks(route, th):
    t = route.shape[0]
    return pl.pallas_call(
        _rank_kernel,
        grid=(t // th,),
        in_specs=[pl.BlockSpec((th, ROUTE_W), lambda i: (i, 0))],
        out_specs=[pl.BlockSpec((th, ROUTE_W), lambda i: (i, 0)),
                   pl.BlockSpec((1, LANES), lambda i: (0, 0))],
        out_shape=[jax.ShapeDtypeStruct((t, ROUTE_W), F32),
                   jax.ShapeDtypeStruct((1, LANES), F32)],
        scratch_shapes=[pltpu.VMEM((1, LANES), F32)],
        compiler_params=_cparams(("arbitrary",)),
        name="expert_ranks",
    )(route)


def _dest_kernel(route_ref, rank_ref, start_ref, dest_ref):
    th = route_ref.shape[0]
    route = route_ref[...]
    rank = rank_ref[...]
    lane_f = lax.broadcasted_iota(I32, (th, LANES), 1).astype(F32)
    start = start_ref[...]
    d0 = rank[:, 0:1] + jnp.sum(jnp.where(lane_f == route[:, 0:1], start, 0.0), axis=-1, keepdims=True)
    d1 = rank[:, 1:2] + jnp.sum(jnp.where(lane_f == route[:, 1:2], start, 0.0), axis=-1, keepdims=True)
    lane8 = lax.broadcasted_iota(I32, (th, ROUTE_W), 1)
    dest_ref[...] = jnp.where(lane8 == 0, d0, jnp.where(lane8 == 1, d1, 0.0)).astype(I32)


def _expert_dest(route, rank, start_row, th):
    t = route.shape[0]
    return pl.pallas_call(
        _dest_kernel,
        grid=(t // th,),
        in_specs=[pl.BlockSpec((th, ROUTE_W), lambda i: (i, 0)),
                  pl.BlockSpec((th, ROUTE_W), lambda i: (i, 0)),
                  pl.BlockSpec((1, LANES), lambda i: (0, 0))],
        out_specs=pl.BlockSpec((th, ROUTE_W), lambda i: (i, 0)),
        out_shape=jax.ShapeDtypeStruct((t, ROUTE_W), I32),
        compiler_params=_cparams(("arbitrary",)),
        name="expert_dest",
    )(route, rank, start_row)


def _prefetched_indices(dest_hbm, idx_smem, idx_sem, tile, n_tiles):
    def idx_copy(t, slot):
        return pltpu.make_async_copy(dest_hbm.at[t], idx_smem.at[slot], idx_sem.at[slot])

    slot = tile % 2

    @pl.when(tile == 0)
    def _():
        idx_copy(0, 0).start()

    idx_copy(tile, slot).wait()

    @pl.when(tile + 1 < n_tiles)
    def _():
        idx_copy(tile + 1, 1 - slot).start()

    return slot


def _dispatch_kernel(dest_hbm, h_ref, xs_in_hbm, xs_hbm, idx_smem, idx_sem, row_sem):
    del xs_in_hbm
    td = h_ref.shape[0]
    slot = _prefetched_indices(dest_hbm, idx_smem, idx_sem, pl.program_id(0), pl.num_programs(0))

    def row_copy(r, dst_row):
        return pltpu.make_async_copy(h_ref.at[pl.ds(r, 1)], xs_hbm.at[pl.ds(dst_row, 1)], row_sem)

    for n in range(2 * td):
        row_copy(n // 2, idx_smem[slot, n]).start(priority=n % 2)
    for n in range(2 * td):
        row_copy(n // 2, 0).wait()


def _dispatch(dest_tiles, h2, xs_init, td):
    t, d = h2.shape
    return pl.pallas_call(
        _dispatch_kernel,
        grid=(t // td,),
        in_specs=[pl.BlockSpec(memory_space=pl.ANY),
                  pl.BlockSpec((td, d), lambda i: (i, 0)),
                  pl.BlockSpec(memory_space=pl.ANY)],
        out_specs=pl.BlockSpec(memory_space=pl.ANY),
        out_shape=jax.ShapeDtypeStruct(xs_init.shape, xs_init.dtype),
        scratch_shapes=[pltpu.SMEM((2, 2 * td), I32), pltpu.SemaphoreType.DMA((2,)), pltpu.SemaphoreType.DMA],
        input_output_aliases={2: 0},
        compiler_params=_cparams(("arbitrary",)),
        name="moe_dispatch",
    )(dest_tiles, h2, xs_init)


def _expert_kernel(be_ref, nused_ref, xs_ref, w1_ref, w3_ref, w2_ref, ys_ref, w13b_ref, w2b_ref):
    i = pl.program_id(0)
    changed = jnp.logical_or(i == 0, be_ref[i] != be_ref[jnp.maximum(i - 1, 0)])

    @pl.when(jnp.logical_and(changed, i < nused_ref[0]))
    def _():
        w13b_ref[:, :D_EXPERT] = w1_ref[0].astype(BF16)
        w13b_ref[:, D_EXPERT:] = w3_ref[0].astype(BF16)
        w2b_ref[...] = w2_ref[0].astype(BF16)

    @pl.when(i < nused_ref[0])
    def _():
        half = xs_ref.shape[0] // 2
        rows = [slice(0, half), slice(half, 2 * half)]
        h13 = [_dot(xs_ref[r, :].astype(BF16), w13b_ref[...]) for r in rows]
        for r, h in zip(rows, h13):
            hid = _silu(h[:, :D_EXPERT]) * h[:, D_EXPERT:]
            ys_ref[r, :] = _dot(hid.astype(BF16), w2b_ref[...])

    @pl.when(i >= nused_ref[0])
    def _():
        ys_ref[...] = jnp.zeros_like(ys_ref)


def _experts(blk_expert, n_used, xs, w1, w3, w2, blk):
    n_rows, d = xs.shape
    n_blocks = n_rows // blk
    row_map = lambda i, be, nu: (i, 0)
    grid_spec = pltpu.PrefetchScalarGridSpec(
        num_scalar_prefetch=2,
        grid=(n_blocks,),
        in_specs=[pl.BlockSpec((blk, d), row_map),
                  pl.BlockSpec((1, d, D_EXPERT), lambda i, be, nu: (be[i], 0, 0)),
                  pl.BlockSpec((1, d, D_EXPERT), lambda i, be, nu: (be[i], 0, 0)),
                  pl.BlockSpec((1, D_EXPERT, d), lambda i, be, nu: (be[i], 0, 0))],
        out_specs=pl.BlockSpec((blk, d), row_map),
        scratch_shapes=[pltpu.VMEM((d, 2 * D_EXPERT), BF16), pltpu.VMEM((D_EXPERT, d), BF16)],
    )
    return pl.pallas_call(
        _expert_kernel,
        grid_spec=grid_spec,
        out_shape=jax.ShapeDtypeStruct((n_rows, d), F32),
        compiler_params=_cparams(("arbitrary",)),
        name="moe_experts",
    )(blk_expert, n_used, xs, w1, w3, w2)


def _combine_kernel(dest_hbm, ys_hbm, x1_ref, route_ref, gt2_ref, l2g_ref, l2b_ref, o_ref,
                    idx_smem, buf_ref, idx_sem, row_sem):
    b_i = pl.program_id(0)
    i = pl.program_id(1)
    td = x1_ref.shape[1]
    tile = b_i * pl.num_programs(1) + i
    slot = _prefetched_indices(dest_hbm, idx_smem, idx_sem, tile, pl.num_programs(0) * pl.num_programs(1))

    def row_copy(n, src_row):
        return pltpu.make_async_copy(ys_hbm.at[pl.ds(src_row, 1)], buf_ref.at[n % 2, pl.ds(n // 2, 1)], row_sem)

    for n in range(2 * td):
        row_copy(n, idx_smem[slot, n]).start(priority=n % 2)
    for n in range(2 * td):
        row_copy(n, 0).wait()
    route = route_ref[0]
    ffn = buf_ref[0] * route[:, 2:3] + buf_ref[1] * route[:, 3:4]
    o_ref[0] = _layer_norm(ALPHA * x1_ref[0] + gt2_ref[0] * ffn) * l2g_ref[...] + l2b_ref[...]


def _combine(dest_tiles, ys, x1, route3, gt2, l2g, l2b, td):
    b, s, d = x1.shape
    row = lambda w: pl.BlockSpec((1, td, w), lambda bi, i: (bi, i, 0))
    const = lambda shape: pl.BlockSpec(shape, lambda bi, i: (0,) * len(shape))
    return pl.pallas_call(
        _combine_kernel,
        grid=(b, s // td),
        in_specs=[pl.BlockSpec(memory_space=pl.ANY),
                  pl.BlockSpec(memory_space=pl.ANY),
                  row(d), row(ROUTE_W),
                  pl.BlockSpec((1, 1, d), lambda bi, i: (bi, 0, 0)),
                  const((1, d)), const((1, d))],
        out_specs=row(d),
        out_shape=jax.ShapeDtypeStruct((b, s, d), F32),
        scratch_shapes=[pltpu.SMEM((2, 2 * td), I32), pltpu.VMEM((2, td, d), F32),
                        pltpu.SemaphoreType.DMA((2,)), pltpu.SemaphoreType.DMA],
        compiler_params=_cparams(("arbitrary", "arbitrary")),
        name="moe_combine",
    )(dest_tiles, ys, x1, route3, gt2, l2g, l2b)


def _tile(n, pref):
    t = min(n, pref)
    assert n % t == 0, (n, t)
    return t


def _rope_tables(s):
    half = HEAD_DIM // 2
    inv = ROPE_THETA ** (-jnp.arange(0, half, 2, dtype=F32) / half)
    pos = jnp.arange(s)
    row = (pos // GRID_W).astype(F32)[:, None] * inv[None, :]
    col = (pos % GRID_W).astype(F32)[:, None] * inv[None, :]
    cos = jnp.concatenate([jnp.cos(row), jnp.cos(row), jnp.cos(col), jnp.cos(col)], axis=-1)
    sin = jnp.concatenate([-jnp.sin(row), jnp.sin(row), -jnp.sin(col), jnp.sin(col)], axis=-1)
    return cos, sin


def _layer(x, c, w_ada, b_ada, w_in, conv_w, a_log, dt_bias, gdn_norm_g, q_norm_g, k_norm_g, attn_norm_g,
           w_out, ln1_g, ln1_b, w_group, b_group, w_router, b_router, w1, w3, w2, ln2_g, ln2_b):
    b, s, d = x.shape
    t = b * s
    assert s % CHUNK == 0 and s % GRID_W == 0

    c_pad = jnp.pad(c, ((0, (-b) % 8), (0, 0)))
    mod = _adaln_mod(c_pad, w_ada, b_ada.reshape(1, -1), _tile(6 * d, 1536))[:b]
    mod3 = mod.reshape(b, 6, d)

    gq, gk, gv, gz, gab, aq, ak, av = jnp.split(
        w_in, [GDN_WIDTH, 2 * GDN_WIDTH, 3 * GDN_WIDTH, 4 * GDN_WIDTH, 4 * GDN_WIDTH + N_GATES,
               4 * GDN_WIDTH + N_GATES + ATTN_WIDTH, 4 * GDN_WIDTH + N_GATES + ATTN_WIDTH + ATTN_KV_WIDTH], axis=1)
    w_packed = jnp.concatenate([gq, gk, gv, gz, ak, jnp.pad(gab, ((0, 0), (0, LANES - N_GATES)))], axis=1).astype(BF16)
    wqv_t = jnp.concatenate([aq, av], axis=1).T.astype(BF16)
    wab_t = gab.T.astype(BF16)
    cos, sin = _rope_tables(s)

    tm = _tile(s, 512)
    gqkv, z, ab, abt, a_qt, a_k, a_vt = _in_projection(
        x, mod3, w_packed, wqv_t, wab_t, cos, sin, cos.T, sin.T, q_norm_g.reshape(-1, 1), k_norm_g.reshape(1, -1), tm)

    pad_gates = lambda p: jnp.pad(p.reshape(1, -1), ((0, 0), (0, N_GATES - p.size)))
    alog_c = pad_gates(a_log)
    dtb_c = pad_gates(dt_bias)
    tc = _tile(s, 512)
    gq_n, gk_n, gv_n, gb, egc, egl, ea, gcr = _gdn_prep(
        gqkv, conv_w, ab, abt, alog_c, dtb_c, alog_c.reshape(-1, 1), dtb_c.reshape(-1, 1), tc)
    a_m, b_m, a_dg, q_eff, o_in = _gdn_local(gq_n, gk_n, gv_n, gb, egc, egl, ea, gcr, _tile(s, 256))
    o_f, o_b = _gdn_scan(a_m, b_m, a_dg, q_eff, o_in, _tile(s // CHUNK, 4))

    tq, tk = _tile(s, 512), _tile(s, 2048)
    att_lagged, excess = _attention_lagged(a_qt, a_k, a_vt, tq, tk)
    att = lax.cond(jnp.max(excess) > ATTN_LAG_LIMIT,
                   lambda: _attention(a_qt, a_k, a_vt, tq, tk), lambda: att_lagged)

    w_rt = jnp.pad(jnp.concatenate([w_group, w_router], axis=1), ((0, 0), (0, LANES - N_GROUPS - N_EXPERTS)))
    b_rt = jnp.pad(jnp.concatenate([b_group, b_router]).reshape(1, -1), ((0, 0), (0, LANES - N_GROUPS - N_EXPERTS)))
    x1, h2, route = _mixer_out(o_f, o_b, z, att, x, mod3, w_out.astype(BF16), gdn_norm_g.reshape(1, -1),
                               attn_norm_g.reshape(-1, 1), ln1_g.reshape(1, -1), ln1_b.reshape(1, -1),
                               w_rt, b_rt, _tile(s, 512))

    blk = 256
    th = _tile(t, 512)
    route2 = route.reshape(t, ROUTE_W)
    rank, counts = _expert_ranks(route2, th)
    counts_i = counts[0, :N_EXPERTS].astype(I32)
    padded = (counts_i + blk - 1) // blk * blk
    pad_end = jnp.cumsum(padded)
    pad_start = pad_end - padded
    n_blocks = -(-(2 * t) // blk) + N_EXPERTS
    blk_pos = jnp.arange(n_blocks, dtype=I32) * blk
    blk_expert = jnp.minimum(jnp.sum((pad_end[None, :] <= blk_pos[:, None]).astype(I32), axis=1), N_EXPERTS - 1)
    n_used = jnp.maximum(pad_end[-1:] // blk, 1).astype(I32)
    start_row = jnp.pad(pad_start.astype(F32).reshape(1, -1), ((0, 0), (0, LANES - N_EXPERTS)))
    dest = _expert_dest(route2, rank, start_row, th)

    td = _tile(s, 256)
    dest_tiles = dest[:, 0:2].reshape(t // td, 2 * td)
    xs = _dispatch(dest_tiles, h2.reshape(t, d), jnp.zeros((n_blocks * blk, d), F32), td)
    ys = _experts(blk_expert, n_used, xs, w1, w3, w2, blk)
    gt2 = mod3[:, 5:6, :]
    return _combine(dest_tiles, ys, x1, route, gt2, ln2_g.reshape(1, -1), ln2_b.reshape(1, -1), td)


def kernel(x, c, w_ada, b_ada, w_in, conv_w, a_log, dt_bias, gdn_norm_g, q_norm_g, k_norm_g, attn_norm_g,
           w_out, ln1_g, ln1_b, w_group, b_group, w_router, b_router, w1, w3, w2, ln2_g, ln2_b):
    for layer in range(w_ada.shape[0]):
        x = _layer(x, c, w_ada[layer], b_ada[layer], w_in[layer], conv_w[layer], a_log[layer], dt_bias[layer],
                   gdn_norm_g[layer], q_norm_g[layer], k_norm_g[layer], attn_norm_g[layer], w_out[layer],
                   ln1_g[layer], ln1_b[layer], w_group[layer], b_group[layer], w_router[layer], b_router[layer],
                   w1[layer], w3[layer], w2[layer], ln2_g[layer], ln2_b[layer])
    return x
```

```python
import functools
import math

import jax
import jax.numpy as jnp
from jax import lax
from jax.experimental import pallas as pl
from jax.experimental.pallas import tpu as pltpu

F32 = jnp.float32
BF16 = jnp.bfloat16
I32 = jnp.int32

HEAD_DIM = 128
GDN_HEADS = 4
GDN_WIDTH = GDN_HEADS * HEAD_DIM
ATTN_Q_HEADS = 4
ATTN_KV_HEADS = 2
ATTN_GROUP = ATTN_Q_HEADS // ATTN_KV_HEADS
ATTN_WIDTH = ATTN_Q_HEADS * HEAD_DIM
ATTN_KV_WIDTH = ATTN_KV_HEADS * HEAD_DIM
CONV_K = 5
CHUNK = 64
GRID_W = 64
ROPE_THETA = 10000.0
N_GROUPS = 4
EXPERTS_PER_GROUP = 8
N_EXPERTS = N_GROUPS * EXPERTS_PER_GROUP
D_EXPERT = 256
DEPTH = 1
ALPHA = (2.0 * DEPTH) ** 0.25
LN_EPS = 1e-5
RMS_EPS = 1e-6

LANES = 128
VMEM_LIMIT_BYTES = 56 * 1024 * 1024
NEG_BIG = -1e30


def _cparams(semantics):
    return pltpu.CompilerParams(dimension_semantics=semantics, vmem_limit_bytes=VMEM_LIMIT_BYTES)


def _dot(a, b):
    return jnp.dot(a, b, preferred_element_type=F32)


def _dot_nt(a, b):
    return lax.dot_general(a, b, (((1,), (1,)), ((), ())), preferred_element_type=F32)


def _dot_tn(a, b):
    return lax.dot_general(a, b, (((0,), (0,)), ((), ())), preferred_element_type=F32)


def _split3(a):
    hi = a.astype(BF16)
    r = a - hi.astype(F32)
    mid = r.astype(BF16)
    lo = (r - mid.astype(F32)).astype(BF16)
    return hi, mid, lo


def _dot_f32_lhs_exact(a_bf16_exact, b):
    hi, mid, lo = _split3(b)
    return _dot(a_bf16_exact, hi) + _dot(a_bf16_exact, mid) + _dot(a_bf16_exact, lo)


def _dot_f32_rhs_exact(a, b_bf16_exact):
    hi, mid, lo = _split3(a)
    return _dot(hi, b_bf16_exact) + _dot(mid, b_bf16_exact) + _dot(lo, b_bf16_exact)


def _dot_f32(a, b):
    ah, am, _ = _split3(a)
    bh, bm, _ = _split3(b)
    return _dot(ah, bh) + (_dot(ah, bm) + _dot(am, bh))


ROW_TILE = 8


def _store_tile_rows(ref, lead, row0, value):
    n = value.shape[0]
    for c in range(ROW_TILE):
        ref[lead + (pl.ds(ROW_TILE * row0 + c, n, stride=ROW_TILE), slice(None))] = value[:, c * LANES:(c + 1) * LANES]


def _load_tile_rows(ref, lead, row0, n):
    return jnp.concatenate(
        [ref[lead + (pl.ds(ROW_TILE * row0 + c, n, stride=ROW_TILE), slice(None))] for c in range(ROW_TILE)], axis=1)


def _sigmoid(x):
    return 1.0 / (1.0 + jnp.exp(-x))


def _silu(x):
    return x * _sigmoid(x)


def _softplus(x):
    return jnp.maximum(x, 0.0) + jnp.log1p(jnp.exp(-jnp.abs(x)))


def _layer_norm(x):
    mu = jnp.mean(x, axis=-1, keepdims=True)
    xc = x - mu
    var = jnp.mean(xc * xc, axis=-1, keepdims=True)
    return xc * lax.rsqrt(var + LN_EPS)


def _mod_kernel(c_ref, w_ref, b_ref, o_ref):
    o_ref[...] = _dot_f32(_silu(c_ref[...]), w_ref[...]) + b_ref[...]


def _adaln_mod(c_pad, w_ada, b_ada, tn):
    rows, d = c_pad.shape
    n = w_ada.shape[1]
    return pl.pallas_call(
        _mod_kernel,
        grid=(n // tn,),
        in_specs=[pl.BlockSpec((rows, d), lambda j: (0, 0)),
                  pl.BlockSpec((d, tn), lambda j: (0, j)),
                  pl.BlockSpec((1, tn), lambda j: (0, j))],
        out_specs=pl.BlockSpec((rows, tn), lambda j: (0, j)),
        out_shape=jax.ShapeDtypeStruct((rows, n), F32),
        compiler_params=_cparams(("arbitrary",)),
        name="adaln_mod",
    )(c_pad, w_ada, b_ada)


_W_GQKV = 0
_W_Z = 3 * GDN_WIDTH
_W_AK = _W_Z + GDN_WIDTH
_W_AB = _W_AK + ATTN_KV_WIDTH
_W_COLS = _W_AB + LANES
N_GATES = 4 * GDN_HEADS
LOG2E = math.log2(math.e)


def _rope(xh, cos, sin_signed, lane):
    fwd = pltpu.roll(xh, 32, 1)
    bwd = pltpu.roll(xh, LANES - 32, 1)
    partner = jnp.where((lane % 64) < 32, bwd, fwd)
    return xh * cos + partner * sin_signed


def _rope_t(xt, cos_t, sin_signed_t):
    q = HEAD_DIM // 4
    partner = jnp.concatenate([xt[q:2 * q], xt[0:q], xt[3 * q:4 * q], xt[2 * q:3 * q]], axis=0)
    return xt * cos_t + partner * sin_signed_t


def _inproj_kernel(x_ref, mod_ref, w_ref, wqvt_ref, wabt_ref, cos_ref, sin_ref, cost_ref, sint_ref, qg_ref, kg_ref,
                   gqkv_ref, z_ref, ab_ref, abt_ref, aqt_ref, ak_ref, avt_ref):
    x = x_ref[0]
    sh1 = mod_ref[0, 0:1, :]
    sc1 = mod_ref[0, 1:2, :]
    h = _layer_norm(x) * (1.0 + sc1) + sh1
    hb = h.astype(BF16)
    gqkv_ref[0] = _dot(hb, w_ref[:, _W_GQKV:_W_Z]).astype(BF16)
    z_ref[0] = _dot(hb, w_ref[:, _W_Z:_W_AK]).astype(BF16)
    ab_ref[0] = _dot(hb, w_ref[:, _W_AB:_W_COLS])[:, 0:N_GATES]
    abt_ref[0] = _dot_nt(wabt_ref[...], hb)
    ak = _dot(hb, w_ref[:, _W_AK:_W_AB])
    cos = cos_ref[...]
    sin = sin_ref[...]
    lane = lax.broadcasted_iota(I32, cos.shape, 1)
    for j in range(ATTN_KV_HEADS):
        xh = ak[:, j * HEAD_DIM:(j + 1) * HEAD_DIM]
        xn = xh * lax.rsqrt(jnp.mean(xh * xh, axis=-1, keepdims=True) + RMS_EPS) * kg_ref[...]
        ak_ref[0, :, j * HEAD_DIM:(j + 1) * HEAD_DIM] = _rope(xn, cos, sin, lane).astype(BF16)
    qvt = _dot_nt(wqvt_ref[...], hb)
    cos_t = cost_ref[...]
    sin_t = sint_ref[...]
    q_scale = (HEAD_DIM ** -0.5) * LOG2E
    for i in range(ATTN_Q_HEADS):
        xt = qvt[i * HEAD_DIM:(i + 1) * HEAD_DIM, :]
        xn = xt * lax.rsqrt(jnp.mean(xt * xt, axis=0, keepdims=True) + RMS_EPS) * qg_ref[...]
        aqt_ref[0, i * HEAD_DIM:(i + 1) * HEAD_DIM, :] = (_rope_t(xn, cos_t, sin_t) * q_scale).astype(BF16)
    avt_ref[0] = qvt[ATTN_WIDTH:, :].astype(BF16)


def _in_projection(x, mod3, w_packed, wqv_t, wab_t, cos, sin, cos_t, sin_t, qg_col, kg_row, tm):
    b, s, d = x.shape
    grid = (b, s // tm)
    row = lambda w: pl.BlockSpec((1, tm, w), lambda bi, i: (bi, i, 0))
    col = lambda h: pl.BlockSpec((1, h, tm), lambda bi, i: (bi, 0, i))
    const = lambda shape: pl.BlockSpec(shape, lambda bi, i: (0,) * len(shape))
    out_shapes = [
        jax.ShapeDtypeStruct((b, s, 3 * GDN_WIDTH), BF16),
        jax.ShapeDtypeStruct((b, s, GDN_WIDTH), BF16),
        jax.ShapeDtypeStruct((b, s, N_GATES), F32),
        jax.ShapeDtypeStruct((b, N_GATES, s), F32),
        jax.ShapeDtypeStruct((b, ATTN_WIDTH, s), BF16),
        jax.ShapeDtypeStruct((b, s, ATTN_KV_WIDTH), BF16),
        jax.ShapeDtypeStruct((b, ATTN_KV_WIDTH, s), BF16),
    ]
    out_specs = [row(3 * GDN_WIDTH), row(GDN_WIDTH), row(N_GATES), col(N_GATES),
                 col(ATTN_WIDTH), row(ATTN_KV_WIDTH), col(ATTN_KV_WIDTH)]
    return pl.pallas_call(
        _inproj_kernel,
        grid=grid,
        in_specs=[row(d),
                  pl.BlockSpec((1, 6, d), lambda bi, i: (bi, 0, 0)),
                  const((d, _W_COLS)),
                  const((ATTN_WIDTH + ATTN_KV_WIDTH, d)),
                  const((N_GATES, d)),
                  pl.BlockSpec((tm, HEAD_DIM), lambda bi, i: (i, 0)),
                  pl.BlockSpec((tm, HEAD_DIM), lambda bi, i: (i, 0)),
                  pl.BlockSpec((HEAD_DIM, tm), lambda bi, i: (0, i)),
                  pl.BlockSpec((HEAD_DIM, tm), lambda bi, i: (0, i)),
                  const((HEAD_DIM, 1)),
                  const((1, HEAD_DIM))],
        out_specs=out_specs,
        out_shape=out_shapes,
        compiler_params=_cparams(("arbitrary", "arbitrary")),
        name="in_projection",
    )(x, mod3, w_packed, wqv_t, wab_t, cos, sin, cos_t, sin_t, qg_col, kg_row)


_HALO = 16


def _chunk_masks(n, transpose=False):
    r = lax.broadcasted_iota(I32, (n, n), 0)
    c = lax.broadcasted_iota(I32, (n, n), 1)
    same = (r // CHUNK) == (c // CHUNK)
    lower = same & (r >= c)
    upper = same & (r <= c)
    return lower, upper


def _gdn_prep_kernel(prev_ref, main_ref, next_ref, convw_ref, ab_ref, abt_ref,
                     alog_c_ref, dtb_c_ref, alog_r_ref, dtb_r_ref,
                     q_ref, k_ref, v_ref, gb_ref, egc_ref, egl_ref, ea_ref, gcr_ref, buf_ref):
    i = pl.program_id(1)
    n_i = pl.num_programs(1)
    tc = main_ref.shape[1]
    prev_scale = jnp.where(i > 0, 1.0, 0.0).astype(F32)
    next_scale = jnp.where(i < n_i - 1, 1.0, 0.0).astype(F32)
    buf_ref[0:_HALO, :] = prev_ref[0].astype(F32) * prev_scale
    buf_ref[_HALO:_HALO + tc, :] = main_ref[0].astype(F32)
    buf_ref[_HALO + tc:, :] = next_ref[0].astype(F32) * next_scale
    pad = CONV_K // 2
    for part, out_ref in enumerate((q_ref, k_ref, v_ref)):
        for hh in range(GDN_HEADS):
            c0 = part * GDN_WIDTH + hh * HEAD_DIM
            acc = None
            for j in range(CONV_K):
                term = buf_ref[_HALO - pad + j:_HALO - pad + j + tc, c0:c0 + HEAD_DIM] * convw_ref[j:j + 1, c0:c0 + HEAD_DIM]
                acc = term if acc is None else acc + term
            y = _silu(acc)
            if part < 2:
                y = y * lax.rsqrt(jnp.sum(y * y, axis=-1, keepdims=True) + RMS_EPS)
            if part == 0:
                y = y * (HEAD_DIM ** -0.5)
            out_ref[0, :, hh * HEAD_DIM:(hh + 1) * HEAD_DIM] = y.astype(BF16)

    lower, upper = _chunk_masks(tc)
    lower_b = jnp.where(lower, 1.0, 0.0).astype(BF16)
    upper_b = jnp.where(upper, 1.0, 0.0).astype(BF16)
    n_dir = 2 * GDN_HEADS
    ab = ab_ref[0]
    lane = lax.broadcasted_iota(I32, ab.shape, 1)
    g = -jnp.exp(alog_c_ref[...]) * _softplus(ab + dtb_c_ref[...])
    pre = _dot_f32_lhs_exact(lower_b, g)
    suf = _dot_f32_lhs_exact(upper_b, g)
    gtot = pre + suf - g
    gc = jnp.where(lane < GDN_HEADS, pre, suf)
    gb_ref[0] = jnp.where(lane < n_dir, gc, _sigmoid(ab))
    egc_ref[0] = jnp.exp(gc)
    egl_ref[0] = jnp.exp(gtot - gc)
    ea_ref[0] = jnp.exp(gtot)
    abt = abt_ref[0]
    row = lax.broadcasted_iota(I32, abt.shape, 0)
    g_r = -jnp.exp(alog_r_ref[...]) * _softplus(abt + dtb_r_ref[...])
    pre_r = _dot_f32_rhs_exact(g_r, upper_b)
    suf_r = _dot_f32_rhs_exact(g_r, lower_b)
    gcr_ref[0] = jnp.where(row < GDN_HEADS, pre_r, suf_r)[0:n_dir]


def _gdn_prep(gqkv, conv_w, ab, abt, alog_c, dtb_c, alog_r, dtb_r, tc):
    b, s, w = gqkv.shape
    nh = tc // _HALO
    n_dir = 2 * GDN_HEADS
    grid = (b, s // tc)
    row = lambda width: pl.BlockSpec((1, tc, width), lambda bi, i: (bi, i, 0))
    const = lambda shape: pl.BlockSpec(shape, lambda bi, i: (0,) * len(shape))
    last_halo = s // _HALO - 1
    out_shapes = ([jax.ShapeDtypeStruct((b, s, GDN_WIDTH), BF16)] * 3
                  + [jax.ShapeDtypeStruct((b, s, N_GATES), F32)] * 4
                  + [jax.ShapeDtypeStruct((b, n_dir, s), F32)])
    out_specs = ([row(GDN_WIDTH)] * 3 + [row(N_GATES)] * 4
                 + [pl.BlockSpec((1, n_dir, tc), lambda bi, i: (bi, 0, i))])
    return pl.pallas_call(
        _gdn_prep_kernel,
        grid=grid,
        in_specs=[pl.BlockSpec((1, _HALO, w), lambda bi, i: (bi, jnp.maximum(i * nh - 1, 0), 0)),
                  row(w),
                  pl.BlockSpec((1, _HALO, w), lambda bi, i: (bi, jnp.minimum((i + 1) * nh, last_halo), 0)),
                  const((CONV_K, w)),
                  row(N_GATES),
                  pl.BlockSpec((1, N_GATES, tc), lambda bi, i: (bi, 0, i)),
                  const((1, N_GATES)), const((1, N_GATES)),
                  const((N_GATES, 1)), const((N_GATES, 1))],
        out_specs=out_specs,
        out_shape=out_shapes,
        scratch_shapes=[pltpu.VMEM((tc + 2 * _HALO, w), F32)],
        compiler_params=_cparams(("arbitrary", "arbitrary")),
        name="gdn_prep",
    )(gqkv, gqkv, gqkv, conv_w, ab, abt, alog_c, dtb_c, alog_r, dtb_r)


def _gdn_local_kernel(q_ref, k_ref, v_ref, gb_ref, egc_ref, egl_ref, ea_ref, gcr_ref,
                      a_ref, b_ref, dg_ref, qe_ref, oi_ref):
    tc = q_ref.shape[1]
    nch = tc // CHUNK
    lower, upper = _chunk_masks(tc)
    r = lax.broadcasted_iota(I32, (tc, tc), 0)
    c = lax.broadcasted_iota(I32, (tc, tc), 1)
    eye = r == c
    blk_xor = r ^ c
    n_dir = 2 * GDN_HEADS
    combos = [(d, hh) for d in range(2) for hh in range(GDN_HEADS)]
    head = lambda hh: slice(hh * HEAD_DIM, (hh + 1) * HEAD_DIM)
    kk = [_dot_nt(k_ref[0, :, head(hh)], k_ref[0, :, head(hh)]) for hh in range(GDN_HEADS)]
    qk = [_dot_nt(q_ref[0, :, head(hh)], k_ref[0, :, head(hh)]) for hh in range(GDN_HEADS)]
    decay, m, t = {}, {}, {}
    for cb in combos:
        d, hh = cb
        idx = d * GDN_HEADS + hh
        incl = lower if d == 0 else upper
        gc = gb_ref[0, :, idx:idx + 1]
        beta = gb_ref[0, :, n_dir + idx:n_dir + idx + 1]
        gcr = gcr_ref[0, idx:idx + 1, :]
        decay[cb] = jnp.where(incl, jnp.exp(jnp.minimum(gc - gcr, 0.0)), 0.0)
        m[cb] = jnp.where(eye, 0.0, beta * kk[hh] * decay[cb])
        t[cb] = jnp.where(eye, 1.0, 0.0) - jnp.where(blk_xor < 2, m[cb], 0.0)
    sz = 2
    while sz < CHUNK:
        join = (blk_xor >= sz) & (blk_xor < 2 * sz)
        tb = {cb: t[cb].astype(BF16) for cb in combos}
        tc_s = {cb: _dot(tb[cb], jnp.where(join, m[cb], 0.0).astype(BF16)).astype(BF16) for cb in combos}
        for cb in combos:
            t[cb] = t[cb] - _dot(tc_s[cb], tb[cb])
        sz *= 2
    solb = {}
    for cb in combos:
        d, hh = cb
        idx = d * GDN_HEADS + hh
        beta = gb_ref[0, :, n_dir + idx:n_dir + idx + 1]
        egc = egc_ref[0, :, idx:idx + 1]
        rhs = jnp.concatenate([v_ref[0, :, head(hh)].astype(F32) * beta,
                               k_ref[0, :, head(hh)].astype(F32) * (beta * egc)], axis=1).astype(BF16)
        solb[cb] = _dot(t[cb].astype(BF16), rhs).astype(BF16)
    for cb in combos:
        d, hh = cb
        idx = d * GDN_HEADS + hh
        egc = egc_ref[0, :, idx:idx + 1]
        qo = _dot((qk[hh] * decay[cb]).astype(BF16), solb[cb])
        oi_ref[0, d, :, head(hh)] = qo[:, :HEAD_DIM].astype(BF16)
        qe_ref[0, d, :, head(hh)] = (q_ref[0, :, head(hh)].astype(F32) * egc - qo[:, HEAD_DIM:]).astype(BF16)
    for cb in combos:
        d, hh = cb
        idx = d * GDN_HEADS + hh
        egl = egl_ref[0, :, idx:idx + 1]
        ea = ea_ref[0, :, idx:idx + 1]
        kg = (k_ref[0, :, head(hh)].astype(F32) * egl).astype(BF16)
        for ci in range(nch):
            rs = slice(ci * CHUNK, (ci + 1) * CHUNK)
            ab = _dot_tn(kg[rs], solb[cb][rs])
            a_ref[0, d, ci, :, head(hh)] = (-ab[:, HEAD_DIM:]).astype(BF16)
            b_ref[0, d, ci, :, head(hh)] = ab[:, :HEAD_DIM].astype(BF16)
            dg_ref[0, d, ci, :, head(hh)] = jnp.broadcast_to(ea[ci * CHUNK:ci * CHUNK + 1, :], (8, HEAD_DIM))


def _gdn_local(q, k, v, gb, egc, egl, ea, gcr, tc):
    b, s, w = q.shape
    n = s // CHUNK
    nch = tc // CHUNK
    n_dir = 2 * GDN_HEADS
    grid = (b, s // tc)
    row = lambda width: pl.BlockSpec((1, tc, width), lambda bi, i: (bi, i, 0))
    return pl.pallas_call(
        _gdn_local_kernel,
        grid=grid,
        in_specs=[row(w), row(w), row(w), row(N_GATES), row(N_GATES), row(N_GATES), row(N_GATES),
                  pl.BlockSpec((1, n_dir, tc), lambda bi, i: (bi, 0, i))],
        out_specs=[pl.BlockSpec((1, 2, nch, HEAD_DIM, w), lambda bi, i: (bi, 0, i, 0, 0)),
                   pl.BlockSpec((1, 2, nch, HEAD_DIM, w), lambda bi, i: (bi, 0, i, 0, 0)),
                   pl.BlockSpec((1, 2, nch, 8, w), lambda bi, i: (bi, 0, i, 0, 0)),
                   pl.BlockSpec((1, 2, tc, w), lambda bi, i: (bi, 0, i, 0)),
                   pl.BlockSpec((1, 2, tc, w), lambda bi, i: (bi, 0, i, 0))],
        out_shape=[jax.ShapeDtypeStruct((b, 2, n, HEAD_DIM, w), BF16),
                   jax.ShapeDtypeStruct((b, 2, n, HEAD_DIM, w), BF16),
                   jax.ShapeDtypeStruct((b, 2, n, 8, w), F32),
                   jax.ShapeDtypeStruct((b, 2, s, w), BF16),
                   jax.ShapeDtypeStruct((b, 2, s, w), BF16)],
        compiler_params=_cparams(("arbitrary", "arbitrary")),
        name="gdn_local",
    )(q, k, v, gb, egc, egl, ea, gcr)


def _gdn_scan_kernel(af_ref, bf_ref, df_ref, qf_ref, of_ref, ab_ref, bb_ref, db_ref, qb_ref, ob_ref,
                     outf_ref, outb_ref, s_ref):
    @pl.when(pl.program_id(1) == 0)
    def _():
        s_ref[...] = jnp.zeros_like(s_ref)

    cs = af_ref.shape[2]
    dirs = ((af_ref, bf_ref, df_ref, qf_ref, of_ref, outf_ref), (ab_ref, bb_ref, db_ref, qb_ref, ob_ref, outb_ref))
    for j in range(cs):
        for d, (a_ref, b_ref, dg_ref, q_ref, o_ref, out_ref) in enumerate(dirs):
            ci = j if d == 0 else cs - 1 - j
            rs = slice(ci * CHUNK, (ci + 1) * CHUNK)
            for hh in range(GDN_HEADS):
                hs = slice(hh * HEAD_DIM, (hh + 1) * HEAD_DIM)
                si = d * GDN_HEADS + hh
                st = s_ref[si]
                stb = st.astype(BF16)
                out_ref[0, rs, hs] = (_dot(q_ref[0, 0, rs, hs], stb) + o_ref[0, 0, rs, hs].astype(F32)).astype(BF16)
                s_ref[si] = (dg_ref[0, 0, ci, 0:1, hs] * st + _dot(a_ref[0, 0, ci, :, hs], stb)
                             + b_ref[0, 0, ci, :, hs].astype(F32))


def _gdn_scan(a, bm, dg, qe, oi, cs):
    b, _, n, _, w = a.shape
    s = n * CHUNK
    ns = n // cs
    tr = cs * CHUNK
    fwd5 = lambda bi, i: (bi, 0, i, 0, 0)
    bwd5 = lambda bi, i: (bi, 1, ns - 1 - i, 0, 0)
    fwd4 = lambda bi, i: (bi, 0, i, 0)
    bwd4 = lambda bi, i: (bi, 1, ns - 1 - i, 0)
    blk5 = (1, 1, cs, HEAD_DIM, w)
    blkd = (1, 1, cs, 8, w)
    blk4 = (1, 1, tr, w)
    return pl.pallas_call(
        _gdn_scan_kernel,
        grid=(b, ns),
        in_specs=[pl.BlockSpec(blk5, fwd5), pl.BlockSpec(blk5, fwd5), pl.BlockSpec(blkd, fwd5),
                  pl.BlockSpec(blk4, fwd4), pl.BlockSpec(blk4, fwd4),
                  pl.BlockSpec(blk5, bwd5), pl.BlockSpec(blk5, bwd5), pl.BlockSpec(blkd, bwd5),
                  pl.BlockSpec(blk4, bwd4), pl.BlockSpec(blk4, bwd4)],
        out_specs=[pl.BlockSpec((1, tr, w), lambda bi, i: (bi, i, 0)),
                   pl.BlockSpec((1, tr, w), lambda bi, i: (bi, ns - 1 - i, 0))],
        out_shape=[jax.ShapeDtypeStruct((b, s, w), BF16)] * 2,
        scratch_shapes=[pltpu.VMEM((2 * GDN_HEADS, HEAD_DIM, HEAD_DIM), F32)],
        compiler_params=_cparams(("arbitrary", "arbitrary")),
        name="gdn_scan",
    )(a, bm, dg, qe, oi, a, bm, dg, qe, oi)


ATTN_SUB = 512
ATTN_LAG_LIMIT = 64.0


def _attn_lagged_kernel(qt_ref, k_ref, vt_ref, ot_ref, ex_ref, q2t_ref, m_ref, l_ref, acc_ref, exc_ref):
    j = pl.program_id(3)
    tq = qt_ref.shape[2]
    tk = k_ref.shape[1]

    @pl.when(j == 0)
    def _():
        for gi in range(ATTN_GROUP):
            q2t_ref[:, gi * tq:(gi + 1) * tq] = qt_ref[0, gi * HEAD_DIM:(gi + 1) * HEAD_DIM, :]
        m_ref[...] = jnp.max(_dot(k_ref[0, 0:ATTN_SUB, :], q2t_ref[...]), axis=0, keepdims=True)
        l_ref[...] = jnp.zeros_like(l_ref)
        acc_ref[...] = jnp.zeros_like(acc_ref)
        exc_ref[...] = jnp.zeros_like(exc_ref)

    q2t = q2t_ref[...]
    m_est = m_ref[...]
    m_run = m_est
    l_run = l_ref[...]
    n_sub = tk // ATTN_SUB
    scores = lambda jj: _dot(k_ref[0, jj * ATTN_SUB:(jj + 1) * ATTN_SUB, :], q2t)
    st_next = scores(0)
    for jj in range(n_sub):
        ks = slice(jj * ATTN_SUB, (jj + 1) * ATTN_SUB)
        st = st_next
        if jj + 1 < n_sub:
            st_next = scores(jj + 1)
        p = jnp.exp2(st - m_est)
        m_run = jnp.maximum(m_run, jnp.max(st, axis=0, keepdims=True))
        l_run = l_run + jnp.sum(p, axis=0, keepdims=True)
        acc_ref[...] += _dot(vt_ref[0, :, ks], p.astype(BF16))
    rebase = jnp.exp2(m_est - m_run)
    acc_ref[...] = acc_ref[...] * rebase
    l_ref[...] = l_run * rebase
    m_ref[...] = m_run
    exc_ref[...] = jnp.maximum(exc_ref[...], m_run - m_est)

    @pl.when(j == pl.num_programs(3) - 1)
    def _():
        out = acc_ref[...] / l_ref[...]
        for gi in range(ATTN_GROUP):
            ot_ref[0, gi * HEAD_DIM:(gi + 1) * HEAD_DIM, :] = out[:, gi * tq:(gi + 1) * tq].astype(BF16)
        ex_ref[0] = exc_ref[...]


def _attention_lagged(q_t, k, v_t, tq, tk):
    b, _, s = q_t.shape
    gw = ATTN_GROUP * HEAD_DIM
    nq = s // tq
    assert tk % ATTN_SUB == 0
    return pl.pallas_call(
        _attn_lagged_kernel,
        grid=(b, ATTN_KV_HEADS, nq, s // tk),
        in_specs=[pl.BlockSpec((1, gw, tq), lambda bi, g, i, j: (bi, g, i)),
                  pl.BlockSpec((1, tk, HEAD_DIM), lambda bi, g, i, j: (bi, j, g)),
                  pl.BlockSpec((1, HEAD_DIM, tk), lambda bi, g, i, j: (bi, g, j))],
        out_specs=[pl.BlockSpec((1, gw, tq), lambda bi, g, i, j: (bi, g, i)),
                   pl.BlockSpec((1, 1, ATTN_GROUP * tq), lambda bi, g, i, j: ((bi * ATTN_KV_HEADS + g) * nq + i, 0, 0))],
        out_shape=[jax.ShapeDtypeStruct((b, ATTN_WIDTH, s), BF16),
                   jax.ShapeDtypeStruct((b * ATTN_KV_HEADS * nq, 1, ATTN_GROUP * tq), F32)],
        scratch_shapes=[pltpu.VMEM((HEAD_DIM, ATTN_GROUP * tq), BF16),
                        pltpu.VMEM((1, ATTN_GROUP * tq), F32),
                        pltpu.VMEM((1, ATTN_GROUP * tq), F32),
                        pltpu.VMEM((HEAD_DIM, ATTN_GROUP * tq), F32),
                        pltpu.VMEM((1, ATTN_GROUP * tq), F32)],
        compiler_params=_cparams(("arbitrary", "arbitrary", "arbitrary", "arbitrary")),
        name="gqa_attention_lagged",
    )(q_t, k, v_t)


def _attn_kernel(qt_ref, k_ref, vt_ref, ot_ref, q2t_ref, m_ref, l_ref, acc_ref):
    j = pl.program_id(3)
    tq = qt_ref.shape[2]
    tk = k_ref.shape[1]

    @pl.when(j == 0)
    def _():
        for gi in range(ATTN_GROUP):
            q2t_ref[:, gi * tq:(gi + 1) * tq] = qt_ref[0, gi * HEAD_DIM:(gi + 1) * HEAD_DIM, :]
        m_ref[...] = jnp.full_like(m_ref, NEG_BIG)
        l_ref[...] = jnp.zeros_like(l_ref)
        acc_ref[...] = jnp.zeros_like(acc_ref)

    q2t = q2t_ref[...]
    m_prev = m_ref[...]
    l_prev = l_ref[...]
    n_sub = tk // ATTN_SUB
    scores = lambda jj: _dot(k_ref[0, jj * ATTN_SUB:(jj + 1) * ATTN_SUB, :], q2t)
    st_next = scores(0)
    for jj in range(n_sub):
        ks = slice(jj * ATTN_SUB, (jj + 1) * ATTN_SUB)
        st = st_next
        if jj + 1 < n_sub:
            st_next = scores(jj + 1)
        m_new = jnp.maximum(m_prev, jnp.max(st, axis=0, keepdims=True))
        alpha = jnp.exp2(m_prev - m_new)
        p = jnp.exp2(st - m_new)
        l_prev = alpha * l_prev + jnp.sum(p, axis=0, keepdims=True)
        acc_ref[...] = alpha * acc_ref[...] + _dot(vt_ref[0, :, ks], p.astype(BF16))
        m_prev = m_new
    m_ref[...] = m_prev
    l_ref[...] = l_prev

    @pl.when(j == pl.num_programs(3) - 1)
    def _():
        out = acc_ref[...] / l_ref[...]
        for gi in range(ATTN_GROUP):
            ot_ref[0, gi * HEAD_DIM:(gi + 1) * HEAD_DIM, :] = out[:, gi * tq:(gi + 1) * tq].astype(BF16)


def _attention(q_t, k, v_t, tq, tk):
    b, _, s = q_t.shape
    gw = ATTN_GROUP * HEAD_DIM
    assert tk % ATTN_SUB == 0
    return pl.pallas_call(
        _attn_kernel,
        grid=(b, ATTN_KV_HEADS, s // tq, s // tk),
        in_specs=[pl.BlockSpec((1, gw, tq), lambda bi, g, i, j: (bi, g, i)),
                  pl.BlockSpec((1, tk, HEAD_DIM), lambda bi, g, i, j: (bi, j, g)),
                  pl.BlockSpec((1, HEAD_DIM, tk), lambda bi, g, i, j: (bi, g, j))],
        out_specs=pl.BlockSpec((1, gw, tq), lambda bi, g, i, j: (bi, g, i)),
        out_shape=jax.ShapeDtypeStruct((b, ATTN_WIDTH, s), BF16),
        scratch_shapes=[pltpu.VMEM((HEAD_DIM, ATTN_GROUP * tq), BF16),
                        pltpu.VMEM((1, ATTN_GROUP * tq), F32),
                        pltpu.VMEM((1, ATTN_GROUP * tq), F32),
                        pltpu.VMEM((HEAD_DIM, ATTN_GROUP * tq), F32)],
        compiler_params=_cparams(("arbitrary", "arbitrary", "arbitrary", "arbitrary")),
        name="gqa_attention",
    )(q_t, k, v_t)


ROUTE_W = 8


def _first_argmax(vals, lane_f, valid):
    vmax = jnp.max(jnp.where(valid, vals, NEG_BIG), axis=-1, keepdims=True)
    idx = jnp.min(jnp.where(valid & (vals == vmax), lane_f, float(LANES)), axis=-1, keepdims=True)
    return vmax, idx


def _mix_kernel(of_ref, ob_ref, z_ref, att_ref, x_ref, mod_ref, wout_ref, gng_ref, ang_ref,
                l1g_ref, l1b_ref, wrt_ref, brt_ref, x1_ref, h2_ref, route_ref):
    o = of_ref[0].astype(F32) + ob_ref[0].astype(F32)
    z = z_ref[0].astype(F32)
    parts = []
    for hh in range(GDN_HEADS):
        hs = slice(hh * HEAD_DIM, (hh + 1) * HEAD_DIM)
        oh = o[:, hs]
        on = oh * lax.rsqrt(jnp.mean(oh * oh, axis=-1, keepdims=True) + RMS_EPS) * gng_ref[...]
        parts.append((on * _silu(z[:, hs])).astype(BF16))
    att_t = att_ref[0].astype(F32)
    attn_t = (att_t * lax.rsqrt(jnp.mean(att_t * att_t, axis=0, keepdims=True) + RMS_EPS) * ang_ref[...]).astype(BF16)
    mixed = _dot(jnp.concatenate(parts, axis=1), wout_ref[:GDN_WIDTH, :]) + _dot_tn(attn_t, wout_ref[GDN_WIDTH:, :])
    gt1 = mod_ref[0, 2:3, :]
    sh2 = mod_ref[0, 3:4, :]
    sc2 = mod_ref[0, 4:5, :]
    x1 = _layer_norm(ALPHA * x_ref[0] + gt1 * mixed) * l1g_ref[...] + l1b_ref[...]
    x1_ref[0] = x1
    h2 = _layer_norm(x1) * (1.0 + sc2) + sh2
    _store_tile_rows(h2_ref, (), 0, h2)
    logits = _dot_f32(h2, wrt_ref[...]) + brt_ref[...]
    lane = lax.broadcasted_iota(I32, logits.shape, 1)
    lane_f = lane.astype(F32)
    is_grp = lane < N_GROUPS
    gmax, gidx = _first_argmax(logits, lane_f, is_grp)
    gsum = jnp.sum(jnp.where(is_grp, jnp.exp(jnp.minimum(logits - gmax, 0.0)), 0.0), axis=-1, keepdims=True)
    grp_p = 1.0 / gsum
    lo = float(N_GROUPS) + float(EXPERTS_PER_GROUP) * gidx
    in_grp = (lane_f >= lo) & (lane_f < lo + float(EXPERTS_PER_GROUP))
    v0, i0 = _first_argmax(logits, lane_f, in_grp)
    v1, i1 = _first_argmax(logits, lane_f, in_grp & (lane_f != i0))
    e1 = jnp.exp(v1 - v0)
    w0 = grp_p / (1.0 + e1)
    w1 = grp_p * e1 / (1.0 + e1)
    route = jnp.where(lane == 0, i0 - float(N_GROUPS),
                      jnp.where(lane == 1, i1 - float(N_GROUPS),
                                jnp.where(lane == 2, w0, jnp.where(lane == 3, w1, 0.0))))
    route_ref[0] = route[:, 0:ROUTE_W]


def _mixer_out(o_f, o_b, z, att, x, mod3, w_out, gng, ang, l1g, l1b, w_rt, b_rt, tm):
    b, s, d = x.shape
    row = lambda w: pl.BlockSpec((1, tm, w), lambda bi, i: (bi, i, 0))
    const = lambda shape: pl.BlockSpec(shape, lambda bi, i: (0,) * len(shape))
    return pl.pallas_call(
        _mix_kernel,
        grid=(b, s // tm),
        in_specs=[row(GDN_WIDTH), row(GDN_WIDTH), row(GDN_WIDTH),
                  pl.BlockSpec((1, ATTN_WIDTH, tm), lambda bi, i: (bi, 0, i)), row(d),
                  pl.BlockSpec((1, 6, d), lambda bi, i: (bi, 0, 0)),
                  const(w_out.shape), const((1, HEAD_DIM)), const((ATTN_WIDTH, 1)),
                  const((1, d)), const((1, d)), const((d, LANES)), const((1, LANES))],
        out_specs=[row(d), pl.BlockSpec((tm * ROW_TILE, LANES), lambda bi, i: (bi * (s // tm) + i, 0)), row(ROUTE_W)],
        out_shape=[jax.ShapeDtypeStruct((b, s, d), F32),
                   jax.ShapeDtypeStruct((b * s * ROW_TILE, LANES), F32),
                   jax.ShapeDtypeStruct((b, s, ROUTE_W), F32)],
        compiler_params=_cparams(("arbitrary", "arbitrary")),
        name="mixer_out",
    )(o_f, o_b, z, att, x, mod3, w_out, gng, ang, l1g, l1b, w_rt, b_rt)


def _rank_kernel(route_ref, rank_ref, cnt_ref, carry_ref):
    @pl.when(pl.program_id(0) == 0)
    def _():
        carry_ref[...] = jnp.zeros_like(carry_ref)

    th = route_ref.shape[0]
    route = route_ref[...]
    lane_f = lax.broadcasted_iota(I32, (th, LANES), 1).astype(F32)
    oh0 = lane_f == route[:, 0:1]
    oh1 = lane_f == route[:, 1:2]
    both = jnp.where(oh0 | oh1, 1.0, 0.0).astype(BF16)
    r = lax.broadcasted_iota(I32, (th, th), 0)
    c = lax.broadcasted_iota(I32, (th, th), 1)
    before = _dot(jnp.where(r > c, 1.0, 0.0).astype(BF16), both) + carry_ref[...]
    rank0 = jnp.sum(jnp.where(oh0, before, 0.0), axis=-1, keepdims=True)
    rank1 = jnp.sum(jnp.where(oh1, before, 0.0), axis=-1, keepdims=True)
    lane8 = lax.broadcasted_iota(I32, (th, ROUTE_W), 1)
    rank_ref[...] = jnp.where(lane8 == 0, rank0, jnp.where(lane8 == 1, rank1, 0.0))
    total = carry_ref[...] + jnp.sum(both.astype(F32), axis=0, keepdims=True)
    carry_ref[...] = total
    cnt_ref[...] = total


def _expert_ranks(route, th):
    t = route.shape[0]
    return pl.pallas_call(
        _rank_kernel,
        grid=(t // th,),
        in_specs=[pl.BlockSpec((th, ROUTE_W), lambda i: (i, 0))],
        out_specs=[pl.BlockSpec((th, ROUTE_W), lambda i: (i, 0)),
                   pl.BlockSpec((1, LANES), lambda i: (0, 0))],
        out_shape=[jax.ShapeDtypeStruct((t, ROUTE_W), F32),
                   jax.ShapeDtypeStruct((1, LANES), F32)],
        scratch_shapes=[pltpu.VMEM((1, LANES), F32)],
        compiler_params=_cparams(("arbitrary",)),
        name="expert_ranks",
    )(route)


def _dest_kernel(route_ref, rank_ref, start_ref, dest_ref):
    th = route_ref.shape[0]
    route = route_ref[...]
    rank = rank_ref[...]
    lane_f = lax.broadcasted_iota(I32, (th, LANES), 1).astype(F32)
    start = start_ref[...]
    d0 = rank[:, 0:1] + jnp.sum(jnp.where(lane_f == route[:, 0:1], start, 0.0), axis=-1, keepdims=True)
    d1 = rank[:, 1:2] + jnp.sum(jnp.where(lane_f == route[:, 1:2], start, 0.0), axis=-1, keepdims=True)
    lane8 = lax.broadcasted_iota(I32, (th, ROUTE_W), 1)
    dest_ref[...] = (jnp.where(lane8 == 0, d0, jnp.where(lane8 == 1, d1, 0.0)) * float(ROW_TILE)).astype(I32)


def _expert_dest(route, rank, start_row, th):
    t = route.shape[0]
    return pl.pallas_call(
        _dest_kernel,
        grid=(t // th,),
        in_specs=[pl.BlockSpec((th, ROUTE_W), lambda i: (i, 0)),
                  pl.BlockSpec((th, ROUTE_W), lambda i: (i, 0)),
                  pl.BlockSpec((1, LANES), lambda i: (0, 0))],
        out_specs=pl.BlockSpec((th, ROUTE_W), lambda i: (i, 0)),
        out_shape=jax.ShapeDtypeStruct((t, ROUTE_W), I32),
        compiler_params=_cparams(("arbitrary",)),
        name="expert_dest",
    )(route, rank, start_row)


def _prefetched_indices(dest_hbm, idx_smem, idx_sem, tile, n_tiles):
    def idx_copy(t, slot):
        return pltpu.make_async_copy(dest_hbm.at[t], idx_smem.at[slot], idx_sem.at[slot])

    slot = tile % 2

    @pl.when(tile == 0)
    def _():
        idx_copy(0, 0).start()

    idx_copy(tile, slot).wait()

    @pl.when(tile + 1 < n_tiles)
    def _():
        idx_copy(tile + 1, 1 - slot).start()

    return slot


def _dispatch_kernel(dest_hbm, h_ref, xs_in_hbm, xs_hbm, idx_smem, idx_sem, row_sem):
    del xs_in_hbm
    td = h_ref.shape[0] // ROW_TILE
    slot = _prefetched_indices(dest_hbm, idx_smem, idx_sem, pl.program_id(0), pl.num_programs(0))

    def row_copy(r, dst_row):
        return pltpu.make_async_copy(h_ref.at[pl.ds(ROW_TILE * r, ROW_TILE)],
                                     xs_hbm.at[pl.ds(pl.multiple_of(dst_row, ROW_TILE), ROW_TILE)], row_sem)

    for n in range(2 * td):
        row_copy(n // 2, idx_smem[slot, n]).start(priority=n % 2)
    for n in range(2 * td):
        row_copy(n // 2, 0).wait()


def _dispatch(dest_tiles, h2, xs_init, td):
    t = h2.shape[0] // ROW_TILE
    return pl.pallas_call(
        _dispatch_kernel,
        grid=(t // td,),
        in_specs=[pl.BlockSpec(memory_space=pl.ANY),
                  pl.BlockSpec((td * ROW_TILE, LANES), lambda i: (i, 0)),
                  pl.BlockSpec(memory_space=pl.ANY)],
        out_specs=pl.BlockSpec(memory_space=pl.ANY),
        out_shape=jax.ShapeDtypeStruct(xs_init.shape, xs_init.dtype),
        scratch_shapes=[pltpu.SMEM((2, 2 * td), I32), pltpu.SemaphoreType.DMA((2,)), pltpu.SemaphoreType.DMA],
        input_output_aliases={2: 0},
        compiler_params=_cparams(("arbitrary",)),
        name="moe_dispatch",
    )(dest_tiles, h2, xs_init)


def _expert_kernel(be_ref, nused_ref, xs_ref, w1_ref, w3_ref, w2_ref, ys_ref, w13b_ref, w2b_ref):
    i = pl.program_id(0)
    changed = jnp.logical_or(i == 0, be_ref[i] != be_ref[jnp.maximum(i - 1, 0)])

    @pl.when(jnp.logical_and(changed, i < nused_ref[0]))
    def _():
        w13b_ref[:, :D_EXPERT] = w1_ref[0].astype(BF16)
        w13b_ref[:, D_EXPERT:] = w3_ref[0].astype(BF16)
        w2b_ref[...] = w2_ref[0].astype(BF16)

    @pl.when(i < nused_ref[0])
    def _():
        half = xs_ref.shape[0] // (2 * ROW_TILE)
        h13 = [_dot(_load_tile_rows(xs_ref, (), r0, half).astype(BF16), w13b_ref[...]) for r0 in (0, half)]
        for r0, h in zip((0, half), h13):
            hid = _silu(h[:, :D_EXPERT]) * h[:, D_EXPERT:]
            _store_tile_rows(ys_ref, (), r0, _dot(hid.astype(BF16), w2b_ref[...]))

    @pl.when(i >= nused_ref[0])
    def _():
        ys_ref[...] = jnp.zeros_like(ys_ref)


def _experts(blk_expert, n_used, xs, w1, w3, w2, blk):
    d = w1.shape[1]
    n_blocks = xs.shape[0] // (blk * ROW_TILE)
    row_map = lambda i, be, nu: (i, 0)
    grid_spec = pltpu.PrefetchScalarGridSpec(
        num_scalar_prefetch=2,
        grid=(n_blocks,),
        in_specs=[pl.BlockSpec((blk * ROW_TILE, LANES), row_map),
                  pl.BlockSpec((1, d, D_EXPERT), lambda i, be, nu: (be[i], 0, 0)),
                  pl.BlockSpec((1, d, D_EXPERT), lambda i, be, nu: (be[i], 0, 0)),
                  pl.BlockSpec((1, D_EXPERT, d), lambda i, be, nu: (be[i], 0, 0))],
        out_specs=pl.BlockSpec((blk * ROW_TILE, LANES), row_map),
        scratch_shapes=[pltpu.VMEM((d, 2 * D_EXPERT), BF16), pltpu.VMEM((D_EXPERT, d), BF16)],
    )
    return pl.pallas_call(
        _expert_kernel,
        grid_spec=grid_spec,
        out_shape=jax.ShapeDtypeStruct(xs.shape, F32),
        compiler_params=_cparams(("arbitrary",)),
        name="moe_experts",
    )(blk_expert, n_used, xs, w1, w3, w2)


def _combine_kernel(dest_hbm, ys_hbm, x1_ref, route_ref, gt2_ref, l2g_ref, l2b_ref, o_ref,
                    idx_smem, buf_ref, idx_sem, row_sem):
    b_i = pl.program_id(0)
    i = pl.program_id(1)
    td = x1_ref.shape[1]
    tile = b_i * pl.num_programs(1) + i
    slot = _prefetched_indices(dest_hbm, idx_smem, idx_sem, tile, pl.num_programs(0) * pl.num_programs(1))

    def row_copy(n, src_row):
        return pltpu.make_async_copy(ys_hbm.at[pl.ds(pl.multiple_of(src_row, ROW_TILE), ROW_TILE)],
                                     buf_ref.at[n % 2, pl.ds(ROW_TILE * (n // 2), ROW_TILE)], row_sem)

    for n in range(2 * td):
        row_copy(n, idx_smem[slot, n]).start(priority=n % 2)
    for n in range(2 * td):
        row_copy(n, 0).wait()
    route = route_ref[0]
    ffn = _load_tile_rows(buf_ref, (0,), 0, td) * route[:, 2:3] + _load_tile_rows(buf_ref, (1,), 0, td) * route[:, 3:4]
    o_ref[0] = _layer_norm(ALPHA * x1_ref[0] + gt2_ref[0] * ffn) * l2g_ref[...] + l2b_ref[...]


def _combine(dest_tiles, ys, x1, route3, gt2, l2g, l2b, td):
    b, s, d = x1.shape
    row = lambda w: pl.BlockSpec((1, td, w), lambda bi, i: (bi, i, 0))
    const = lambda shape: pl.BlockSpec(shape, lambda bi, i: (0,) * len(shape))
    return pl.pallas_call(
        _combine_kernel,
        grid=(b, s // td),
        in_specs=[pl.BlockSpec(memory_space=pl.ANY),
                  pl.BlockSpec(memory_space=pl.ANY),
                  row(d), row(ROUTE_W),
                  pl.BlockSpec((1, 1, d), lambda bi, i: (bi, 0, 0)),
                  const((1, d)), const((1, d))],
        out_specs=row(d),
        out_shape=jax.ShapeDtypeStruct((b, s, d), F32),
        scratch_shapes=[pltpu.SMEM((2, 2 * td), I32), pltpu.VMEM((2, td * ROW_TILE, LANES), F32),
                        pltpu.SemaphoreType.DMA((2,)), pltpu.SemaphoreType.DMA],
        compiler_params=_cparams(("arbitrary", "arbitrary")),
        name="moe_combine",
    )(dest_tiles, ys, x1, route3, gt2, l2g, l2b)


def _tile(n, pref):
    t = min(n, pref)
    assert n % t == 0, (n, t)
    return t


def _rope_tables(s):
    half = HEAD_DIM // 2
    inv = ROPE_THETA ** (-jnp.arange(0, half, 2, dtype=F32) / half)
    pos = jnp.arange(s)
    row = (pos // GRID_W).astype(F32)[:, None] * inv[None, :]
    col = (pos % GRID_W).astype(F32)[:, None] * inv[None, :]
    cos = jnp.concatenate([jnp.cos(row), jnp.cos(row), jnp.cos(col), jnp.cos(col)], axis=-1)
    sin = jnp.concatenate([-jnp.sin(row), jnp.sin(row), -jnp.sin(col), jnp.sin(col)], axis=-1)
    return cos, sin


def _layer(x, c, w_ada, b_ada, w_in, conv_w, a_log, dt_bias, gdn_norm_g, q_norm_g, k_norm_g, attn_norm_g,
           w_out, ln1_g, ln1_b, w_group, b_group, w_router, b_router, w1, w3, w2, ln2_g, ln2_b):
    b, s, d = x.shape
    t = b * s
    assert s % CHUNK == 0 and s % GRID_W == 0

    c_pad = jnp.pad(c, ((0, (-b) % 8), (0, 0)))
    mod = _adaln_mod(c_pad, w_ada, b_ada.reshape(1, -1), _tile(6 * d, 1536))[:b]
    mod3 = mod.reshape(b, 6, d)

    gq, gk, gv, gz, gab, aq, ak, av = jnp.split(
        w_in, [GDN_WIDTH, 2 * GDN_WIDTH, 3 * GDN_WIDTH, 4 * GDN_WIDTH, 4 * GDN_WIDTH + N_GATES,
               4 * GDN_WIDTH + N_GATES + ATTN_WIDTH, 4 * GDN_WIDTH + N_GATES + ATTN_WIDTH + ATTN_KV_WIDTH], axis=1)
    w_packed = jnp.concatenate([gq, gk, gv, gz, ak, jnp.pad(gab, ((0, 0), (0, LANES - N_GATES)))], axis=1).astype(BF16)
    wqv_t = jnp.concatenate([aq, av], axis=1).T.astype(BF16)
    wab_t = gab.T.astype(BF16)
    cos, sin = _rope_tables(s)

    tm = _tile(s, 512)
    gqkv, z, ab, abt, a_qt, a_k, a_vt = _in_projection(
        x, mod3, w_packed, wqv_t, wab_t, cos, sin, cos.T, sin.T, q_norm_g.reshape(-1, 1), k_norm_g.reshape(1, -1), tm)

    pad_gates = lambda p: jnp.pad(p.reshape(1, -1), ((0, 0), (0, N_GATES - p.size)))
    alog_c = pad_gates(a_log)
    dtb_c = pad_gates(dt_bias)
    tc = _tile(s, 512)
    gq_n, gk_n, gv_n, gb, egc, egl, ea, gcr = _gdn_prep(
        gqkv, conv_w, ab, abt, alog_c, dtb_c, alog_c.reshape(-1, 1), dtb_c.reshape(-1, 1), tc)
    a_m, b_m, a_dg, q_eff, o_in = _gdn_local(gq_n, gk_n, gv_n, gb, egc, egl, ea, gcr, _tile(s, 256))
    o_f, o_b = _gdn_scan(a_m, b_m, a_dg, q_eff, o_in, _tile(s // CHUNK, 4))

    tq, tk = _tile(s, 1024), _tile(s, 2048)
    att_lagged, excess = _attention_lagged(a_qt, a_k, a_vt, tq, tk)
    att = lax.cond(jnp.max(excess) > ATTN_LAG_LIMIT,
                   lambda: _attention(a_qt, a_k, a_vt, tq, tk), lambda: att_lagged)

    w_rt = jnp.pad(jnp.concatenate([w_group, w_router], axis=1), ((0, 0), (0, LANES - N_GROUPS - N_EXPERTS)))
    b_rt = jnp.pad(jnp.concatenate([b_group, b_router]).reshape(1, -1), ((0, 0), (0, LANES - N_GROUPS - N_EXPERTS)))
    x1, h2, route = _mixer_out(o_f, o_b, z, att, x, mod3, w_out.astype(BF16), gdn_norm_g.reshape(1, -1),
                               attn_norm_g.reshape(-1, 1), ln1_g.reshape(1, -1), ln1_b.reshape(1, -1),
                               w_rt, b_rt, _tile(s, 512))

    blk = 256
    th = _tile(t, 1024)
    route2 = route.reshape(t, ROUTE_W)
    rank, counts = _expert_ranks(route2, th)
    counts_i = counts[0, :N_EXPERTS].astype(I32)
    padded = (counts_i + blk - 1) // blk * blk
    pad_end = jnp.cumsum(padded)
    pad_start = pad_end - padded
    n_blocks = -(-(2 * t) // blk) + N_EXPERTS
    blk_pos = jnp.arange(n_blocks, dtype=I32) * blk
    blk_expert = jnp.minimum(jnp.sum((pad_end[None, :] <= blk_pos[:, None]).astype(I32), axis=1), N_EXPERTS - 1)
    n_used = jnp.maximum(pad_end[-1:] // blk, 1).astype(I32)
    start_row = jnp.pad(pad_start.astype(F32).reshape(1, -1), ((0, 0), (0, LANES - N_EXPERTS)))
    dest = _expert_dest(route2, rank, start_row, _tile(t, 4096))

    td = _tile(s, 256)
    dest_tiles = dest[:, 0:2].reshape(t // td, 2 * td)
    assert d == ROW_TILE * LANES
    xs = _dispatch(dest_tiles, h2, jnp.zeros((n_blocks * blk * ROW_TILE, LANES), F32), td)
    ys = _experts(blk_expert, n_used, xs, w1, w3, w2, blk)
    gt2 = mod3[:, 5:6, :]
    return _combine(dest_tiles, ys, x1, route, gt2, ln2_g.reshape(1, -1), ln2_b.reshape(1, -1), td)


def kernel(x, c, w_ada, b_ada, w_in, conv_w, a_log, dt_bias, gdn_norm_g, q_norm_g, k_norm_g, attn_norm_g,
           w_out, ln1_g, ln1_b, w_group, b_group, w_router, b_router, w1, w3, w2, ln2_g, ln2_b):
    for layer in range(w_ada.shape[0]):
        x = _layer(x, c, w_ada[layer], b_ada[layer], w_in[layer], conv_w[layer], a_log[layer], dt_bias[layer],
                   gdn_norm_g[layer], q_norm_g[layer], k_norm_g[layer], attn_norm_g[layer], w_out[layer],
                   ln1_g[layer], ln1_b[layer], w_group[layer], b_group[layer], w_router[layer], b_router[layer],
                   w1[layer], w3[layer], w2[layer], ln2_g[layer], ln2_b[layer])
    return x
```

```python
import math

import jax
import jax.numpy as jnp
from jax import lax
from jax.experimental import pallas as pl
from jax.experimental.pallas import tpu as pltpu

F32 = jnp.float32
BF16 = jnp.bfloat16
I32 = jnp.int32

HEAD_DIM = 128
GDN_HEADS = 4
GDN_WIDTH = GDN_HEADS * HEAD_DIM
ATTN_Q_HEADS = 4
ATTN_KV_HEADS = 2
ATTN_GROUP = ATTN_Q_HEADS // ATTN_KV_HEADS
ATTN_WIDTH = ATTN_Q_HEADS * HEAD_DIM
ATTN_KV_WIDTH = ATTN_KV_HEADS * HEAD_DIM
CONV_K = 5
CHUNK = 64
GRID_W = 64
ROPE_THETA = 10000.0
N_GROUPS = 4
EXPERTS_PER_GROUP = 8
N_EXPERTS = N_GROUPS * EXPERTS_PER_GROUP
D_EXPERT = 256
DEPTH = 1
ALPHA = (2.0 * DEPTH) ** 0.25
LN_EPS = 1e-5
RMS_EPS = 1e-6

LANES = 128
VMEM_LIMIT_BYTES = 56 * 1024 * 1024
NEG_BIG = -1e30


def _cparams(semantics):
    return pltpu.CompilerParams(dimension_semantics=semantics, vmem_limit_bytes=VMEM_LIMIT_BYTES)


def _dot(a, b):
    return jnp.dot(a, b, preferred_element_type=F32)


def _dot_nt(a, b):
    return lax.dot_general(a, b, (((1,), (1,)), ((), ())), preferred_element_type=F32)


def _dot_tn(a, b):
    return lax.dot_general(a, b, (((0,), (0,)), ((), ())), preferred_element_type=F32)


def _split3(a):
    hi = a.astype(BF16)
    r = a - hi.astype(F32)
    mid = r.astype(BF16)
    lo = (r - mid.astype(F32)).astype(BF16)
    return hi, mid, lo


def _dot_f32_lhs_exact(a_bf16_exact, b):
    hi, mid, lo = _split3(b)
    return _dot(a_bf16_exact, hi) + _dot(a_bf16_exact, mid) + _dot(a_bf16_exact, lo)


def _dot_f32_rhs_exact(a, b_bf16_exact):
    hi, mid, lo = _split3(a)
    return _dot(hi, b_bf16_exact) + _dot(mid, b_bf16_exact) + _dot(lo, b_bf16_exact)


def _dot_f32(a, b):
    ah, am, _ = _split3(a)
    bh, bm, _ = _split3(b)
    return _dot(ah, bh) + (_dot(ah, bm) + _dot(am, bh))


ROW_TILE = 8


def _store_tile_rows(ref, lead, row0, value):
    n = value.shape[0]
    for c in range(ROW_TILE):
        ref[lead + (pl.ds(ROW_TILE * row0 + c, n, stride=ROW_TILE), slice(None))] = value[:, c * LANES:(c + 1) * LANES]


def _load_tile_rows(ref, lead, row0, n):
    return jnp.concatenate(
        [ref[lead + (pl.ds(ROW_TILE * row0 + c, n, stride=ROW_TILE), slice(None))] for c in range(ROW_TILE)], axis=1)


def _sigmoid(x):
    return 1.0 / (1.0 + jnp.exp(-x))


def _silu(x):
    return x * _sigmoid(x)


def _softplus(x):
    return jnp.maximum(x, 0.0) + jnp.log1p(jnp.exp(-jnp.abs(x)))


def _layer_norm(x):
    mu = jnp.mean(x, axis=-1, keepdims=True)
    xc = x - mu
    var = jnp.mean(xc * xc, axis=-1, keepdims=True)
    return xc * lax.rsqrt(var + LN_EPS)


def _mod_kernel(c_ref, w_ref, b_ref, o_ref):
    o_ref[...] = _dot_f32(_silu(c_ref[...]), w_ref[...]) + b_ref[...]


def _adaln_mod(c_pad, w_ada, b_ada, tn):
    rows, d = c_pad.shape
    n = w_ada.shape[1]
    return pl.pallas_call(
        _mod_kernel,
        grid=(n // tn,),
        in_specs=[pl.BlockSpec((rows, d), lambda j: (0, 0)),
                  pl.BlockSpec((d, tn), lambda j: (0, j)),
                  pl.BlockSpec((1, tn), lambda j: (0, j))],
        out_specs=pl.BlockSpec((rows, tn), lambda j: (0, j)),
        out_shape=jax.ShapeDtypeStruct((rows, n), F32),
        compiler_params=_cparams(("arbitrary",)),
        name="adaln_mod",
    )(c_pad, w_ada, b_ada)


_W_GQKV = 0
_W_Z = 3 * GDN_WIDTH
_W_AK = _W_Z + GDN_WIDTH
_W_AB = _W_AK + ATTN_KV_WIDTH
_W_COLS = _W_AB + LANES
N_GATES = 4 * GDN_HEADS
LOG2E = math.log2(math.e)


def _rope(xh, cos, sin_signed, lane):
    fwd = pltpu.roll(xh, 32, 1)
    bwd = pltpu.roll(xh, LANES - 32, 1)
    partner = jnp.where((lane % 64) < 32, bwd, fwd)
    return xh * cos + partner * sin_signed


def _rope_t(xt, cos_t, sin_signed_t):
    q = HEAD_DIM // 4
    partner = jnp.concatenate([xt[q:2 * q], xt[0:q], xt[3 * q:4 * q], xt[2 * q:3 * q]], axis=0)
    return xt * cos_t + partner * sin_signed_t


def _inproj_kernel(x_ref, mod_ref, w_ref, wqvt_ref, wabt_ref, cos_ref, sin_ref, cost_ref, sint_ref, qg_ref, kg_ref,
                   gqkv_ref, z_ref, ab_ref, abt_ref, aqt_ref, ak_ref, avt_ref):
    sh1 = mod_ref[0, 0:1, :]
    sc1 = mod_ref[0, 1:2, :]
    q_scale = (HEAD_DIM ** -0.5) * LOG2E
    h = _layer_norm(x_ref[0]) * (1.0 + sc1) + sh1
    hb = h.astype(BF16)
    gqkv_ref[0] = _dot(hb, w_ref[:, _W_GQKV:_W_Z]).astype(BF16)
    z_ref[0] = _dot(hb, w_ref[:, _W_Z:_W_AK]).astype(BF16)
    ab_ref[0] = _dot(hb, w_ref[:, _W_AB:_W_COLS])[:, 0:N_GATES]
    abt_ref[0] = _dot_nt(wabt_ref[...], hb)
    ak = _dot(hb, w_ref[:, _W_AK:_W_AB])
    cos = cos_ref[...]
    sin = sin_ref[...]
    lane = lax.broadcasted_iota(I32, cos.shape, 1)
    for j in range(ATTN_KV_HEADS):
        xh = ak[:, j * HEAD_DIM:(j + 1) * HEAD_DIM]
        xn = xh * lax.rsqrt(jnp.mean(xh * xh, axis=-1, keepdims=True) + RMS_EPS) * kg_ref[...]
        ak_ref[0, :, j * HEAD_DIM:(j + 1) * HEAD_DIM] = _rope(xn, cos, sin, lane).astype(BF16)
    qvt = _dot_nt(wqvt_ref[...], hb)
    cos_t = cost_ref[...]
    sin_t = sint_ref[...]
    for i in range(ATTN_Q_HEADS):
        xt = qvt[i * HEAD_DIM:(i + 1) * HEAD_DIM, :]
        xn = xt * lax.rsqrt(jnp.mean(xt * xt, axis=0, keepdims=True) + RMS_EPS) * qg_ref[...]
        aqt_ref[0, i * HEAD_DIM:(i + 1) * HEAD_DIM, :] = (_rope_t(xn, cos_t, sin_t) * q_scale).astype(BF16)
    avt_ref[0] = qvt[ATTN_WIDTH:, :].astype(BF16)


def _in_projection(x, mod3, w_packed, wqv_t, wab_t, cos, sin, cos_t, sin_t, qg_col, kg_row, tm):
    b, s, d = x.shape
    grid = (b, s // tm)
    row = lambda w: pl.BlockSpec((1, tm, w), lambda bi, i: (bi, i, 0))
    col = lambda h: pl.BlockSpec((1, h, tm), lambda bi, i: (bi, 0, i))
    const = lambda shape: pl.BlockSpec(shape, lambda bi, i: (0,) * len(shape))
    out_shapes = [
        jax.ShapeDtypeStruct((b, s, 3 * GDN_WIDTH), BF16),
        jax.ShapeDtypeStruct((b, s, GDN_WIDTH), BF16),
        jax.ShapeDtypeStruct((b, s, N_GATES), F32),
        jax.ShapeDtypeStruct((b, N_GATES, s), F32),
        jax.ShapeDtypeStruct((b, ATTN_WIDTH, s), BF16),
        jax.ShapeDtypeStruct((b, s, ATTN_KV_WIDTH), BF16),
        jax.ShapeDtypeStruct((b, ATTN_KV_WIDTH, s), BF16),
    ]
    out_specs = [row(3 * GDN_WIDTH), row(GDN_WIDTH), row(N_GATES), col(N_GATES),
                 col(ATTN_WIDTH), row(ATTN_KV_WIDTH), col(ATTN_KV_WIDTH)]
    return pl.pallas_call(
        _inproj_kernel,
        grid=grid,
        in_specs=[row(d),
                  pl.BlockSpec((1, 6, d), lambda bi, i: (bi, 0, 0)),
                  const((d, _W_COLS)),
                  const((ATTN_WIDTH + ATTN_KV_WIDTH, d)),
                  const((N_GATES, d)),
                  pl.BlockSpec((tm, HEAD_DIM), lambda bi, i: (i, 0)),
                  pl.BlockSpec((tm, HEAD_DIM), lambda bi, i: (i, 0)),
                  pl.BlockSpec((HEAD_DIM, tm), lambda bi, i: (0, i)),
                  pl.BlockSpec((HEAD_DIM, tm), lambda bi, i: (0, i)),
                  const((HEAD_DIM, 1)),
                  const((1, HEAD_DIM))],
        out_specs=out_specs,
        out_shape=out_shapes,
        compiler_params=_cparams(("arbitrary", "arbitrary")),
        name="in_projection",
    )(x, mod3, w_packed, wqv_t, wab_t, cos, sin, cos_t, sin_t, qg_col, kg_row)


_HALO = 16


def _chunk_masks(n):
    r = lax.broadcasted_iota(I32, (n, n), 0)
    c = lax.broadcasted_iota(I32, (n, n), 1)
    same = (r // CHUNK) == (c // CHUNK)
    lower = same & (r >= c)
    upper = same & (r <= c)
    return lower, upper


def _gdn_prep_kernel(prev_ref, main_ref, next_ref, convw_ref, ab_ref, abt_ref,
                     alog_c_ref, dtb_c_ref, alog_r_ref, dtb_r_ref,
                     q_ref, k_ref, v_ref, gb_ref, egc_ref, egl_ref, ea_ref, gcr_ref, buf_ref):
    i = pl.program_id(1)
    n_i = pl.num_programs(1)
    tc = main_ref.shape[1]
    prev_scale = jnp.where(i > 0, 1.0, 0.0).astype(F32)
    next_scale = jnp.where(i < n_i - 1, 1.0, 0.0).astype(F32)
    buf_ref[0:_HALO, :] = prev_ref[0].astype(F32) * prev_scale
    buf_ref[_HALO:_HALO + tc, :] = main_ref[0].astype(F32)
    buf_ref[_HALO + tc:, :] = next_ref[0].astype(F32) * next_scale
    pad = CONV_K // 2
    for part, out_ref in enumerate((q_ref, k_ref, v_ref)):
        for hh in range(GDN_HEADS):
            c0 = part * GDN_WIDTH + hh * HEAD_DIM
            acc = None
            for j in range(CONV_K):
                term = buf_ref[_HALO - pad + j:_HALO - pad + j + tc, c0:c0 + HEAD_DIM] * convw_ref[j:j + 1, c0:c0 + HEAD_DIM]
                acc = term if acc is None else acc + term
            y = _silu(acc)
            if part < 2:
                y = y * lax.rsqrt(jnp.sum(y * y, axis=-1, keepdims=True) + RMS_EPS)
            if part == 0:
                y = y * (HEAD_DIM ** -0.5)
            out_ref[0, :, hh * HEAD_DIM:(hh + 1) * HEAD_DIM] = y.astype(BF16)

    lower, upper = _chunk_masks(tc)
    lower_b = jnp.where(lower, 1.0, 0.0).astype(BF16)
    upper_b = jnp.where(upper, 1.0, 0.0).astype(BF16)
    n_dir = 2 * GDN_HEADS
    ab = ab_ref[0]
    lane = lax.broadcasted_iota(I32, ab.shape, 1)
    g = -jnp.exp(alog_c_ref[...]) * _softplus(ab + dtb_c_ref[...])
    pre = _dot_f32_lhs_exact(lower_b, g)
    suf = _dot_f32_lhs_exact(upper_b, g)
    gtot = pre + suf - g
    gc = jnp.where(lane < GDN_HEADS, pre, suf)
    gb_ref[0] = jnp.where(lane < n_dir, gc, _sigmoid(ab))
    egc_ref[0] = jnp.exp(gc)
    egl_ref[0] = jnp.exp(gtot - gc)
    ea_ref[0] = jnp.exp(gtot)
    abt = abt_ref[0]
    row = lax.broadcasted_iota(I32, abt.shape, 0)
    g_r = -jnp.exp(alog_r_ref[...]) * _softplus(abt + dtb_r_ref[...])
    pre_r = _dot_f32_rhs_exact(g_r, upper_b)
    suf_r = _dot_f32_rhs_exact(g_r, lower_b)
    gcr_ref[0] = jnp.where(row < GDN_HEADS, pre_r, suf_r)[0:n_dir]


def _gdn_prep(gqkv, conv_w, ab, abt, alog_c, dtb_c, alog_r, dtb_r, tc):
    b, s, w = gqkv.shape
    nh = tc // _HALO
    n_dir = 2 * GDN_HEADS
    grid = (b, s // tc)
    row = lambda width: pl.BlockSpec((1, tc, width), lambda bi, i: (bi, i, 0))
    const = lambda shape: pl.BlockSpec(shape, lambda bi, i: (0,) * len(shape))
    last_halo = s // _HALO - 1
    out_shapes = ([jax.ShapeDtypeStruct((b, s, GDN_WIDTH), BF16)] * 3
                  + [jax.ShapeDtypeStruct((b, s, N_GATES), F32)] * 4
                  + [jax.ShapeDtypeStruct((b, n_dir, s), F32)])
    out_specs = ([row(GDN_WIDTH)] * 3 + [row(N_GATES)] * 4
                 + [pl.BlockSpec((1, n_dir, tc), lambda bi, i: (bi, 0, i))])
    return pl.pallas_call(
        _gdn_prep_kernel,
        grid=grid,
        in_specs=[pl.BlockSpec((1, _HALO, w), lambda bi, i: (bi, jnp.maximum(i * nh - 1, 0), 0)),
                  row(w),
                  pl.BlockSpec((1, _HALO, w), lambda bi, i: (bi, jnp.minimum((i + 1) * nh, last_halo), 0)),
                  const((CONV_K, w)),
                  row(N_GATES),
                  pl.BlockSpec((1, N_GATES, tc), lambda bi, i: (bi, 0, i)),
                  const((1, N_GATES)), const((1, N_GATES)),
                  const((N_GATES, 1)), const((N_GATES, 1))],
        out_specs=out_specs,
        out_shape=out_shapes,
        scratch_shapes=[pltpu.VMEM((tc + 2 * _HALO, w), F32)],
        compiler_params=_cparams(("arbitrary", "arbitrary")),
        name="gdn_prep",
    )(gqkv, gqkv, gqkv, conv_w, ab, abt, alog_c, dtb_c, alog_r, dtb_r)


def _gdn_local_kernel(q_ref, k_ref, v_ref, gb_ref, egc_ref, egl_ref, ea_ref, gcr_ref,
                      a_ref, b_ref, dg_ref, qe_ref, oi_ref):
    tc = q_ref.shape[1]
    nch = tc // CHUNK
    lower, upper = _chunk_masks(tc)
    r = lax.broadcasted_iota(I32, (tc, tc), 0)
    c = lax.broadcasted_iota(I32, (tc, tc), 1)
    eye = r == c
    blk_xor = r ^ c
    n_dir = 2 * GDN_HEADS
    combos = [(d, hh) for d in range(2) for hh in range(GDN_HEADS)]
    head = lambda hh: slice(hh * HEAD_DIM, (hh + 1) * HEAD_DIM)
    kk = [_dot_nt(k_ref[0, :, head(hh)], k_ref[0, :, head(hh)]) for hh in range(GDN_HEADS)]
    qk = [_dot_nt(q_ref[0, :, head(hh)], k_ref[0, :, head(hh)]) for hh in range(GDN_HEADS)]
    decay, m, t = {}, {}, {}
    for cb in combos:
        d, hh = cb
        idx = d * GDN_HEADS + hh
        incl = lower if d == 0 else upper
        gc = gb_ref[0, :, idx:idx + 1]
        beta = gb_ref[0, :, n_dir + idx:n_dir + idx + 1]
        gcr = gcr_ref[0, idx:idx + 1, :]
        decay[cb] = jnp.where(incl, jnp.exp(jnp.minimum(gc - gcr, 0.0)), 0.0)
        m[cb] = jnp.where(eye, 0.0, beta * kk[hh] * decay[cb])
        t[cb] = jnp.where(eye, 1.0, 0.0) - jnp.where(blk_xor < 2, m[cb], 0.0)
    sz = 2
    while sz < CHUNK:
        join = (blk_xor >= sz) & (blk_xor < 2 * sz)
        tb = {cb: t[cb].astype(BF16) for cb in combos}
        tc_s = {cb: _dot(tb[cb], jnp.where(join, m[cb], 0.0).astype(BF16)).astype(BF16) for cb in combos}
        for cb in combos:
            t[cb] = t[cb] - _dot(tc_s[cb], tb[cb])
        sz *= 2
    solb = {}
    for cb in combos:
        d, hh = cb
        idx = d * GDN_HEADS + hh
        beta = gb_ref[0, :, n_dir + idx:n_dir + idx + 1]
        egc = egc_ref[0, :, idx:idx + 1]
        rhs = jnp.concatenate([v_ref[0, :, head(hh)].astype(F32) * beta,
                               k_ref[0, :, head(hh)].astype(F32) * (beta * egc)], axis=1).astype(BF16)
        solb[cb] = _dot(t[cb].astype(BF16), rhs).astype(BF16)
    for cb in combos:
        d, hh = cb
        idx = d * GDN_HEADS + hh
        egc = egc_ref[0, :, idx:idx + 1]
        qo = _dot((qk[hh] * decay[cb]).astype(BF16), solb[cb])
        oi_ref[0, d, :, head(hh)] = qo[:, :HEAD_DIM].astype(BF16)
        qe_ref[0, d, :, head(hh)] = (q_ref[0, :, head(hh)].astype(F32) * egc - qo[:, HEAD_DIM:]).astype(BF16)
    for cb in combos:
        d, hh = cb
        idx = d * GDN_HEADS + hh
        egl = egl_ref[0, :, idx:idx + 1]
        ea = ea_ref[0, :, idx:idx + 1]
        kg = (k_ref[0, :, head(hh)].astype(F32) * egl).astype(BF16)
        for ci in range(nch):
            rs = slice(ci * CHUNK, (ci + 1) * CHUNK)
            ab = _dot_tn(kg[rs], solb[cb][rs])
            a_ref[0, d, ci, :, head(hh)] = (-ab[:, HEAD_DIM:]).astype(BF16)
            b_ref[0, d, ci, :, head(hh)] = ab[:, :HEAD_DIM].astype(BF16)
            dg_ref[0, d, ci, :, head(hh)] = jnp.broadcast_to(ea[ci * CHUNK:ci * CHUNK + 1, :], (8, HEAD_DIM))


def _gdn_local(q, k, v, gb, egc, egl, ea, gcr, tc):
    b, s, w = q.shape
    n = s // CHUNK
    nch = tc // CHUNK
    n_dir = 2 * GDN_HEADS
    grid = (b, s // tc)
    row = lambda width: pl.BlockSpec((1, tc, width), lambda bi, i: (bi, i, 0))
    return pl.pallas_call(
        _gdn_local_kernel,
        grid=grid,
        in_specs=[row(w), row(w), row(w), row(N_GATES), row(N_GATES), row(N_GATES), row(N_GATES),
                  pl.BlockSpec((1, n_dir, tc), lambda bi, i: (bi, 0, i))],
        out_specs=[pl.BlockSpec((1, 2, nch, HEAD_DIM, w), lambda bi, i: (bi, 0, i, 0, 0)),
                   pl.BlockSpec((1, 2, nch, HEAD_DIM, w), lambda bi, i: (bi, 0, i, 0, 0)),
                   pl.BlockSpec((1, 2, nch, 8, w), lambda bi, i: (bi, 0, i, 0, 0)),
                   pl.BlockSpec((1, 2, tc, w), lambda bi, i: (bi, 0, i, 0)),
                   pl.BlockSpec((1, 2, tc, w), lambda bi, i: (bi, 0, i, 0))],
        out_shape=[jax.ShapeDtypeStruct((b, 2, n, HEAD_DIM, w), BF16),
                   jax.ShapeDtypeStruct((b, 2, n, HEAD_DIM, w), BF16),
                   jax.ShapeDtypeStruct((b, 2, n, 8, w), F32),
                   jax.ShapeDtypeStruct((b, 2, s, w), BF16),
                   jax.ShapeDtypeStruct((b, 2, s, w), BF16)],
        compiler_params=_cparams(("arbitrary", "arbitrary")),
        name="gdn_local",
    )(q, k, v, gb, egc, egl, ea, gcr)


def _gdn_scan_kernel(af_ref, bf_ref, df_ref, qf_ref, of_ref, ab_ref, bb_ref, db_ref, qb_ref, ob_ref,
                     outf_ref, outb_ref, s_ref):
    @pl.when(pl.program_id(1) == 0)
    def _():
        s_ref[...] = jnp.zeros_like(s_ref)

    cs = af_ref.shape[2]
    dirs = ((af_ref, bf_ref, df_ref, qf_ref, of_ref, outf_ref), (ab_ref, bb_ref, db_ref, qb_ref, ob_ref, outb_ref))
    for j in range(cs):
        for d, (a_ref, b_ref, dg_ref, q_ref, o_ref, out_ref) in enumerate(dirs):
            ci = j if d == 0 else cs - 1 - j
            rs = slice(ci * CHUNK, (ci + 1) * CHUNK)
            for hh in range(GDN_HEADS):
                hs = slice(hh * HEAD_DIM, (hh + 1) * HEAD_DIM)
                si = d * GDN_HEADS + hh
                st = s_ref[si]
                stb = st.astype(BF16)
                out_ref[0, rs, hs] = (_dot(q_ref[0, 0, rs, hs], stb) + o_ref[0, 0, rs, hs].astype(F32)).astype(BF16)
                s_ref[si] = (dg_ref[0, 0, ci, 0:1, hs] * st + _dot(a_ref[0, 0, ci, :, hs], stb)
                             + b_ref[0, 0, ci, :, hs].astype(F32))


def _gdn_scan(a, bm, dg, qe, oi, cs):
    b, _, n, _, w = a.shape
    s = n * CHUNK
    ns = n // cs
    tr = cs * CHUNK
    fwd5 = lambda bi, i: (bi, 0, i, 0, 0)
    bwd5 = lambda bi, i: (bi, 1, ns - 1 - i, 0, 0)
    fwd4 = lambda bi, i: (bi, 0, i, 0)
    bwd4 = lambda bi, i: (bi, 1, ns - 1 - i, 0)
    blk5 = (1, 1, cs, HEAD_DIM, w)
    blkd = (1, 1, cs, 8, w)
    blk4 = (1, 1, tr, w)
    return pl.pallas_call(
        _gdn_scan_kernel,
        grid=(b, ns),
        in_specs=[pl.BlockSpec(blk5, fwd5), pl.BlockSpec(blk5, fwd5), pl.BlockSpec(blkd, fwd5),
                  pl.BlockSpec(blk4, fwd4), pl.BlockSpec(blk4, fwd4),
                  pl.BlockSpec(blk5, bwd5), pl.BlockSpec(blk5, bwd5), pl.BlockSpec(blkd, bwd5),
                  pl.BlockSpec(blk4, bwd4), pl.BlockSpec(blk4, bwd4)],
        out_specs=[pl.BlockSpec((1, tr, w), lambda bi, i: (bi, i, 0)),
                   pl.BlockSpec((1, tr, w), lambda bi, i: (bi, ns - 1 - i, 0))],
        out_shape=[jax.ShapeDtypeStruct((b, s, w), BF16)] * 2,
        scratch_shapes=[pltpu.VMEM((2 * GDN_HEADS, HEAD_DIM, HEAD_DIM), F32)],
        compiler_params=_cparams(("arbitrary", "arbitrary")),
        name="gdn_scan",
    )(a, bm, dg, qe, oi, a, bm, dg, qe, oi)


ATTN_SUB = 512
ATTN_LAG_LIMIT = 64.0


def _attn_lagged_kernel(qt_ref, k_ref, vt_ref, ot_ref, ex_ref, q2t_ref, m_ref, l_ref, acc_ref, exc_ref):
    j = pl.program_id(3)
    tq = qt_ref.shape[2]
    tk = k_ref.shape[1]

    @pl.when(j == 0)
    def _():
        for gi in range(ATTN_GROUP):
            q2t_ref[:, gi * tq:(gi + 1) * tq] = qt_ref[0, gi * HEAD_DIM:(gi + 1) * HEAD_DIM, :]
        m_ref[...] = jnp.max(_dot(k_ref[0, 0:ATTN_SUB, :], q2t_ref[...]), axis=0, keepdims=True)
        l_ref[...] = jnp.zeros_like(l_ref)
        acc_ref[...] = jnp.zeros_like(acc_ref)
        exc_ref[...] = jnp.zeros_like(exc_ref)

    q2t = q2t_ref[...]
    m_est = m_ref[...]
    m_run = m_est
    l_run = l_ref[...]
    n_sub = tk // ATTN_SUB
    scores = lambda jj: _dot(k_ref[0, jj * ATTN_SUB:(jj + 1) * ATTN_SUB, :], q2t)
    st_next = scores(0)
    for jj in range(n_sub):
        ks = slice(jj * ATTN_SUB, (jj + 1) * ATTN_SUB)
        st = st_next
        if jj + 1 < n_sub:
            st_next = scores(jj + 1)
        p = jnp.exp2(st - m_est)
        m_run = jnp.maximum(m_run, jnp.max(st, axis=0, keepdims=True))
        l_run = l_run + jnp.sum(p, axis=0, keepdims=True)
        acc_ref[...] += _dot(vt_ref[0, :, ks], p.astype(BF16))
    rebase = jnp.exp2(m_est - m_run)
    acc_ref[...] = acc_ref[...] * rebase
    l_ref[...] = l_run * rebase
    m_ref[...] = m_run
    exc_ref[...] = jnp.maximum(exc_ref[...], m_run - m_est)

    @pl.when(j == pl.num_programs(3) - 1)
    def _():
        out = acc_ref[...] / l_ref[...]
        for gi in range(ATTN_GROUP):
            ot_ref[0, gi * HEAD_DIM:(gi + 1) * HEAD_DIM, :] = out[:, gi * tq:(gi + 1) * tq].astype(BF16)
        ex_ref[0] = exc_ref[...]


def _attention_lagged(q_t, k, v_t, tq, tk):
    b, _, s = q_t.shape
    gw = ATTN_GROUP * HEAD_DIM
    nq = s // tq
    assert tk % ATTN_SUB == 0
    return pl.pallas_call(
        _attn_lagged_kernel,
        grid=(b, ATTN_KV_HEADS, nq, s // tk),
        in_specs=[pl.BlockSpec((1, gw, tq), lambda bi, g, i, j: (bi, g, i)),
                  pl.BlockSpec((1, tk, HEAD_DIM), lambda bi, g, i, j: (bi, j, g)),
                  pl.BlockSpec((1, HEAD_DIM, tk), lambda bi, g, i, j: (bi, g, j))],
        out_specs=[pl.BlockSpec((1, gw, tq), lambda bi, g, i, j: (bi, g, i)),
                   pl.BlockSpec((1, 1, ATTN_GROUP * tq), lambda bi, g, i, j: ((bi * ATTN_KV_HEADS + g) * nq + i, 0, 0))],
        out_shape=[jax.ShapeDtypeStruct((b, ATTN_WIDTH, s), BF16),
                   jax.ShapeDtypeStruct((b * ATTN_KV_HEADS * nq, 1, ATTN_GROUP * tq), F32)],
        scratch_shapes=[pltpu.VMEM((HEAD_DIM, ATTN_GROUP * tq), BF16),
                        pltpu.VMEM((1, ATTN_GROUP * tq), F32),
                        pltpu.VMEM((1, ATTN_GROUP * tq), F32),
                        pltpu.VMEM((HEAD_DIM, ATTN_GROUP * tq), F32),
                        pltpu.VMEM((1, ATTN_GROUP * tq), F32)],
        compiler_params=_cparams(("arbitrary", "arbitrary", "arbitrary", "arbitrary")),
        name="gqa_attention_lagged",
    )(q_t, k, v_t)


def _attn_kernel(qt_ref, k_ref, vt_ref, ot_ref, q2t_ref, m_ref, l_ref, acc_ref):
    j = pl.program_id(3)
    tq = qt_ref.shape[2]
    tk = k_ref.shape[1]

    @pl.when(j == 0)
    def _():
        for gi in range(ATTN_GROUP):
            q2t_ref[:, gi * tq:(gi + 1) * tq] = qt_ref[0, gi * HEAD_DIM:(gi + 1) * HEAD_DIM, :]
        m_ref[...] = jnp.full_like(m_ref, NEG_BIG)
        l_ref[...] = jnp.zeros_like(l_ref)
        acc_ref[...] = jnp.zeros_like(acc_ref)

    q2t = q2t_ref[...]
    m_prev = m_ref[...]
    l_prev = l_ref[...]
    n_sub = tk // ATTN_SUB
    scores = lambda jj: _dot(k_ref[0, jj * ATTN_SUB:(jj + 1) * ATTN_SUB, :], q2t)
    st_next = scores(0)
    for jj in range(n_sub):
        ks = slice(jj * ATTN_SUB, (jj + 1) * ATTN_SUB)
        st = st_next
        if jj + 1 < n_sub:
            st_next = scores(jj + 1)
        m_new = jnp.maximum(m_prev, jnp.max(st, axis=0, keepdims=True))
        alpha = jnp.exp2(m_prev - m_new)
        p = jnp.exp2(st - m_new)
        l_prev = alpha * l_prev + jnp.sum(p, axis=0, keepdims=True)
        acc_ref[...] = alpha * acc_ref[...] + _dot(vt_ref[0, :, ks], p.astype(BF16))
        m_prev = m_new
    m_ref[...] = m_prev
    l_ref[...] = l_prev

    @pl.when(j == pl.num_programs(3) - 1)
    def _():
        out = acc_ref[...] / l_ref[...]
        for gi in range(ATTN_GROUP):
            ot_ref[0, gi * HEAD_DIM:(gi + 1) * HEAD_DIM, :] = out[:, gi * tq:(gi + 1) * tq].astype(BF16)


def _attention(q_t, k, v_t, tq, tk):
    b, _, s = q_t.shape
    gw = ATTN_GROUP * HEAD_DIM
    assert tk % ATTN_SUB == 0
    return pl.pallas_call(
        _attn_kernel,
        grid=(b, ATTN_KV_HEADS, s // tq, s // tk),
        in_specs=[pl.BlockSpec((1, gw, tq), lambda bi, g, i, j: (bi, g, i)),
                  pl.BlockSpec((1, tk, HEAD_DIM), lambda bi, g, i, j: (bi, j, g)),
                  pl.BlockSpec((1, HEAD_DIM, tk), lambda bi, g, i, j: (bi, g, j))],
        out_specs=pl.BlockSpec((1, gw, tq), lambda bi, g, i, j: (bi, g, i)),
        out_shape=jax.ShapeDtypeStruct((b, ATTN_WIDTH, s), BF16),
        scratch_shapes=[pltpu.VMEM((HEAD_DIM, ATTN_GROUP * tq), BF16),
                        pltpu.VMEM((1, ATTN_GROUP * tq), F32),
                        pltpu.VMEM((1, ATTN_GROUP * tq), F32),
                        pltpu.VMEM((HEAD_DIM, ATTN_GROUP * tq), F32)],
        compiler_params=_cparams(("arbitrary", "arbitrary", "arbitrary", "arbitrary")),
        name="gqa_attention",
    )(q_t, k, v_t)


ROUTE_W = 8


def _first_argmax(vals, lane_f, valid):
    vmax = jnp.max(jnp.where(valid, vals, NEG_BIG), axis=-1, keepdims=True)
    idx = jnp.min(jnp.where(valid & (vals == vmax), lane_f, float(LANES)), axis=-1, keepdims=True)
    return vmax, idx


def _mix_kernel(of_ref, ob_ref, z_ref, att_ref, x_ref, mod_ref, wout_ref, gng_ref, ang_ref,
                l1g_ref, l1b_ref, wrt_ref, brt_ref, x1_ref, h2_ref, route_ref):
    o = of_ref[0].astype(F32) + ob_ref[0].astype(F32)
    z = z_ref[0].astype(F32)
    parts = []
    for hh in range(GDN_HEADS):
        hs = slice(hh * HEAD_DIM, (hh + 1) * HEAD_DIM)
        oh = o[:, hs]
        on = oh * lax.rsqrt(jnp.mean(oh * oh, axis=-1, keepdims=True) + RMS_EPS) * gng_ref[...]
        parts.append((on * _silu(z[:, hs])).astype(BF16))
    att_t = att_ref[0].astype(F32)
    attn_t = (att_t * lax.rsqrt(jnp.mean(att_t * att_t, axis=0, keepdims=True) + RMS_EPS) * ang_ref[...]).astype(BF16)
    mixed = _dot(jnp.concatenate(parts, axis=1), wout_ref[:GDN_WIDTH, :]) + _dot_tn(attn_t, wout_ref[GDN_WIDTH:, :])
    gt1 = mod_ref[0, 2:3, :]
    sh2 = mod_ref[0, 3:4, :]
    sc2 = mod_ref[0, 4:5, :]
    x1 = _layer_norm(ALPHA * x_ref[0] + gt1 * mixed) * l1g_ref[...] + l1b_ref[...]
    x1_ref[0] = x1
    h2 = _layer_norm(x1) * (1.0 + sc2) + sh2
    _store_tile_rows(h2_ref, (), 0, h2)
    logits = _dot_f32(h2, wrt_ref[...]) + brt_ref[...]
    lane = lax.broadcasted_iota(I32, logits.shape, 1)
    lane_f = lane.astype(F32)
    is_grp = lane < N_GROUPS
    gmax, gidx = _first_argmax(logits, lane_f, is_grp)
    gsum = jnp.sum(jnp.where(is_grp, jnp.exp(jnp.minimum(logits - gmax, 0.0)), 0.0), axis=-1, keepdims=True)
    grp_p = 1.0 / gsum
    lo = float(N_GROUPS) + float(EXPERTS_PER_GROUP) * gidx
    in_grp = (lane_f >= lo) & (lane_f < lo + float(EXPERTS_PER_GROUP))
    v0, i0 = _first_argmax(logits, lane_f, in_grp)
    v1, i1 = _first_argmax(logits, lane_f, in_grp & (lane_f != i0))
    e1 = jnp.exp(v1 - v0)
    w0 = grp_p / (1.0 + e1)
    w1 = grp_p * e1 / (1.0 + e1)
    route = jnp.where(lane == 0, i0 - float(N_GROUPS),
                      jnp.where(lane == 1, i1 - float(N_GROUPS),
                                jnp.where(lane == 2, w0, jnp.where(lane == 3, w1, 0.0))))
    route_ref[0] = route[:, 0:ROUTE_W]


def _mixer_out(o_f, o_b, z, att, x, mod3, w_out, gng, ang, l1g, l1b, w_rt, b_rt, tm):
    b, s, d = x.shape
    row = lambda w: pl.BlockSpec((1, tm, w), lambda bi, i: (bi, i, 0))
    const = lambda shape: pl.BlockSpec(shape, lambda bi, i: (0,) * len(shape))
    return pl.pallas_call(
        _mix_kernel,
        grid=(b, s // tm),
        in_specs=[row(GDN_WIDTH), row(GDN_WIDTH), row(GDN_WIDTH),
                  pl.BlockSpec((1, ATTN_WIDTH, tm), lambda bi, i: (bi, 0, i)), row(d),
                  pl.BlockSpec((1, 6, d), lambda bi, i: (bi, 0, 0)),
                  const(w_out.shape), const((1, HEAD_DIM)), const((ATTN_WIDTH, 1)),
                  const((1, d)), const((1, d)), const((d, LANES)), const((1, LANES))],
        out_specs=[row(d), pl.BlockSpec((tm * ROW_TILE, LANES), lambda bi, i: (bi * (s // tm) + i, 0)), row(ROUTE_W)],
        out_shape=[jax.ShapeDtypeStruct((b, s, d), F32),
                   jax.ShapeDtypeStruct((b * s * ROW_TILE, LANES), F32),
                   jax.ShapeDtypeStruct((b, s, ROUTE_W), F32)],
        compiler_params=_cparams(("arbitrary", "arbitrary")),
        name="mixer_out",
    )(o_f, o_b, z, att, x, mod3, w_out, gng, ang, l1g, l1b, w_rt, b_rt)


def _rank_kernel(route_ref, rank_ref, cnt_ref, carry_ref):
    @pl.when(pl.program_id(0) == 0)
    def _():
        carry_ref[...] = jnp.zeros_like(carry_ref)

    th = route_ref.shape[0]
    route = route_ref[...]
    lane_f = lax.broadcasted_iota(I32, (th, LANES), 1).astype(F32)
    oh0 = lane_f == route[:, 0:1]
    oh1 = lane_f == route[:, 1:2]
    both = jnp.where(oh0 | oh1, 1.0, 0.0).astype(BF16)
    r = lax.broadcasted_iota(I32, (th, th), 0)
    c = lax.broadcasted_iota(I32, (th, th), 1)
    before = _dot(jnp.where(r > c, 1.0, 0.0).astype(BF16), both) + carry_ref[...]
    rank0 = jnp.sum(jnp.where(oh0, before, 0.0), axis=-1, keepdims=True)
    rank1 = jnp.sum(jnp.where(oh1, before, 0.0), axis=-1, keepdims=True)
    lane8 = lax.broadcasted_iota(I32, (th, ROUTE_W), 1)
    rank_ref[...] = jnp.where(lane8 == 0, rank0, jnp.where(lane8 == 1, rank1, 0.0))
    total = carry_ref[...] + jnp.sum(both.astype(F32), axis=0, keepdims=True)
    carry_ref[...] = total
    cnt_ref[...] = total


def _expert_ranks(route, th):
    t = route.shape[0]
    return pl.pallas_call(
        _rank_kernel,
        grid=(t // th,),
        in_specs=[pl.BlockSpec((th, ROUTE_W), lambda i: (i, 0))],
        out_specs=[pl.BlockSpec((th, ROUTE_W), lambda i: (i, 0)),
                   pl.BlockSpec((1, LANES), lambda i: (0, 0))],
        out_shape=[jax.ShapeDtypeStruct((t, ROUTE_W), F32),
                   jax.ShapeDtypeStruct((1, LANES), F32)],
        scratch_shapes=[pltpu.VMEM((1, LANES), F32)],
        compiler_params=_cparams(("arbitrary",)),
        name="expert_ranks",
    )(route)


def _dest_kernel(route_ref, rank_ref, start_ref, dest_ref):
    th = route_ref.shape[0]
    route = route_ref[...]
    rank = rank_ref[...]
    lane_f = lax.broadcasted_iota(I32, (th, LANES), 1).astype(F32)
    start = start_ref[...]
    d0 = rank[:, 0:1] + jnp.sum(jnp.where(lane_f == route[:, 0:1], start, 0.0), axis=-1, keepdims=True)
    d1 = rank[:, 1:2] + jnp.sum(jnp.where(lane_f == route[:, 1:2], start, 0.0), axis=-1, keepdims=True)
    lane8 = lax.broadcasted_iota(I32, (th, ROUTE_W), 1)
    dest_ref[...] = (jnp.where(lane8 == 0, d0, jnp.where(lane8 == 1, d1, 0.0)) * float(ROW_TILE)).astype(I32)


def _expert_dest(route, rank, start_row, th):
    t = route.shape[0]
    return pl.pallas_call(
        _dest_kernel,
        grid=(t // th,),
        in_specs=[pl.BlockSpec((th, ROUTE_W), lambda i: (i, 0)),
                  pl.BlockSpec((th, ROUTE_W), lambda i: (i, 0)),
                  pl.BlockSpec((1, LANES), lambda i: (0, 0))],
        out_specs=pl.BlockSpec((th, ROUTE_W), lambda i: (i, 0)),
        out_shape=jax.ShapeDtypeStruct((t, ROUTE_W), I32),
        compiler_params=_cparams(("arbitrary",)),
        name="expert_dest",
    )(route, rank, start_row)


def _prefetched_indices(dest_hbm, idx_smem, idx_sem, tile, n_tiles):
    def idx_copy(t, slot):
        return pltpu.make_async_copy(dest_hbm.at[t], idx_smem.at[slot], idx_sem.at[slot])

    slot = tile % 2

    @pl.when(tile == 0)
    def _():
        idx_copy(0, 0).start()

    idx_copy(tile, slot).wait()

    @pl.when(tile + 1 < n_tiles)
    def _():
        idx_copy(tile + 1, 1 - slot).start()

    return slot


_DISPATCH_SLOTS = 3


def _dispatch_kernel(dest_hbm, h_hbm, xs_in_hbm, xs_hbm, idx_smem, h_buf, idx_sem, tile_sem, row_sem):
    del xs_in_hbm
    td = h_buf.shape[1] // ROW_TILE
    tile = pl.program_id(0)
    n_tiles = pl.num_programs(0)
    idx_slot = _prefetched_indices(dest_hbm, idx_smem, idx_sem, tile, n_tiles)
    slot = tile % _DISPATCH_SLOTS
    sem_slot = tile % 2

    def tile_copy(t, sl):
        return pltpu.make_async_copy(h_hbm.at[pl.ds(t * (td * ROW_TILE), td * ROW_TILE)], h_buf.at[sl],
                                     tile_sem.at[sl])

    def row_copy(sl, sem_sl, r, dst_row):
        return pltpu.make_async_copy(h_buf.at[sl, pl.ds(ROW_TILE * r, ROW_TILE)],
                                     xs_hbm.at[pl.ds(pl.multiple_of(dst_row, ROW_TILE), ROW_TILE)], row_sem.at[sem_sl])

    @pl.when(tile == 0)
    def _():
        tile_copy(0, 0).start()

    @pl.when(tile + 1 < n_tiles)
    def _():
        tile_copy(tile + 1, (tile + 1) % _DISPATCH_SLOTS).start()

    tile_copy(tile, slot).wait()
    for n in range(2 * td):
        row_copy(slot, sem_slot, n // 2, idx_smem[idx_slot, n]).start(priority=n % 2)

    @pl.when(tile > 0)
    def _():
        for n in range(2 * td):
            row_copy(slot, 1 - sem_slot, n // 2, 0).wait()

    @pl.when(tile == n_tiles - 1)
    def _():
        for n in range(2 * td):
            row_copy(slot, sem_slot, n // 2, 0).wait()


def _dispatch(dest_tiles, h2, xs_init, td):
    t = h2.shape[0] // ROW_TILE
    return pl.pallas_call(
        _dispatch_kernel,
        grid=(t // td,),
        in_specs=[pl.BlockSpec(memory_space=pl.ANY),
                  pl.BlockSpec(memory_space=pl.ANY),
                  pl.BlockSpec(memory_space=pl.ANY)],
        out_specs=pl.BlockSpec(memory_space=pl.ANY),
        out_shape=jax.ShapeDtypeStruct(xs_init.shape, xs_init.dtype),
        scratch_shapes=[pltpu.SMEM((2, 2 * td), I32), pltpu.VMEM((_DISPATCH_SLOTS, td * ROW_TILE, LANES), F32),
                        pltpu.SemaphoreType.DMA((2,)), pltpu.SemaphoreType.DMA((_DISPATCH_SLOTS,)),
                        pltpu.SemaphoreType.DMA((2,))],
        input_output_aliases={2: 0},
        compiler_params=_cparams(("arbitrary",)),
        name="moe_dispatch",
    )(dest_tiles, h2, xs_init)


def _expert_kernel(be_ref, nused_ref, xs_ref, w1_ref, w3_ref, w2_ref, ys_ref, w13b_ref, w2b_ref):
    i = pl.program_id(0)
    changed = jnp.logical_or(i == 0, be_ref[i] != be_ref[jnp.maximum(i - 1, 0)])

    @pl.when(jnp.logical_and(changed, i < nused_ref[0]))
    def _():
        w13b_ref[:, :D_EXPERT] = w1_ref[0].astype(BF16)
        w13b_ref[:, D_EXPERT:] = w3_ref[0].astype(BF16)
        w2b_ref[...] = w2_ref[0].astype(BF16)

    @pl.when(i < nused_ref[0])
    def _():
        half = xs_ref.shape[0] // (2 * ROW_TILE)
        h13 = [_dot(_load_tile_rows(xs_ref, (), r0, half).astype(BF16), w13b_ref[...]) for r0 in (0, half)]
        for r0, h in zip((0, half), h13):
            hid = _silu(h[:, :D_EXPERT]) * h[:, D_EXPERT:]
            _store_tile_rows(ys_ref, (), r0, _dot(hid.astype(BF16), w2b_ref[...]))

    @pl.when(i >= nused_ref[0])
    def _():
        ys_ref[...] = jnp.zeros_like(ys_ref)


def _experts(blk_expert, n_used, xs, w1, w3, w2, blk):
    d = w1.shape[1]
    n_blocks = xs.shape[0] // (blk * ROW_TILE)
    row_map = lambda i, be, nu: (i, 0)
    grid_spec = pltpu.PrefetchScalarGridSpec(
        num_scalar_prefetch=2,
        grid=(n_blocks,),
        in_specs=[pl.BlockSpec((blk * ROW_TILE, LANES), row_map),
                  pl.BlockSpec((1, d, D_EXPERT), lambda i, be, nu: (be[i], 0, 0)),
                  pl.BlockSpec((1, d, D_EXPERT), lambda i, be, nu: (be[i], 0, 0)),
                  pl.BlockSpec((1, D_EXPERT, d), lambda i, be, nu: (be[i], 0, 0))],
        out_specs=pl.BlockSpec((blk * ROW_TILE, LANES), row_map),
        scratch_shapes=[pltpu.VMEM((d, 2 * D_EXPERT), BF16), pltpu.VMEM((D_EXPERT, d), BF16)],
    )
    return pl.pallas_call(
        _expert_kernel,
        grid_spec=grid_spec,
        out_shape=jax.ShapeDtypeStruct(xs.shape, F32),
        compiler_params=_cparams(("arbitrary",)),
        name="moe_experts",
    )(blk_expert, n_used, xs, w1, w3, w2)


def _combine_kernel(dest_hbm, ys_hbm, x1_ref, route_ref, gt2_ref, l2g_ref, l2b_ref, o_ref,
                    idx_smem, buf_ref, idx_sem, row_sem):
    b_i = pl.program_id(0)
    i = pl.program_id(1)
    td = x1_ref.shape[1]
    tile = b_i * pl.num_programs(1) + i
    n_tiles = pl.num_programs(0) * pl.num_programs(1)
    slot = tile % 2

    def idx_copy(t, sl):
        return pltpu.make_async_copy(dest_hbm.at[t], idx_smem.at[sl], idx_sem.at[sl])

    def row_copy(sl, n, src_row):
        return pltpu.make_async_copy(ys_hbm.at[pl.ds(pl.multiple_of(src_row, ROW_TILE), ROW_TILE)],
                                     buf_ref.at[sl, n % 2, pl.ds(ROW_TILE * (n // 2), ROW_TILE)], row_sem.at[sl])

    def start_rows(sl):
        for n in range(2 * td):
            row_copy(sl, n, idx_smem[sl, n]).start(priority=n % 2)

    @pl.when(tile == 0)
    def _():
        idx_copy(0, 0).start()
        idx_copy(0, 0).wait()
        start_rows(0)

        @pl.when(n_tiles > 1)
        def _():
            idx_copy(1, 1).start()

    @pl.when(tile + 1 < n_tiles)
    def _():
        idx_copy(tile + 1, 1 - slot).wait()
        start_rows(1 - slot)

        @pl.when(tile + 2 < n_tiles)
        def _():
            idx_copy(tile + 2, slot).start()

    for n in range(2 * td):
        row_copy(slot, n, 0).wait()
    route = route_ref[0]
    ffn = (_load_tile_rows(buf_ref, (slot, 0), 0, td) * route[:, 2:3]
           + _load_tile_rows(buf_ref, (slot, 1), 0, td) * route[:, 3:4])
    o_ref[0] = _layer_norm(ALPHA * x1_ref[0] + gt2_ref[0] * ffn) * l2g_ref[...] + l2b_ref[...]


def _combine(dest_tiles, ys, x1, route3, gt2, l2g, l2b, td):
    b, s, d = x1.shape
    row = lambda w: pl.BlockSpec((1, td, w), lambda bi, i: (bi, i, 0))
    const = lambda shape: pl.BlockSpec(shape, lambda bi, i: (0,) * len(shape))
    return pl.pallas_call(
        _combine_kernel,
        grid=(b, s // td),
        in_specs=[pl.BlockSpec(memory_space=pl.ANY),
                  pl.BlockSpec(memory_space=pl.ANY),
                  row(d), row(ROUTE_W),
                  pl.BlockSpec((1, 1, d), lambda bi, i: (bi, 0, 0)),
                  const((1, d)), const((1, d))],
        out_specs=row(d),
        out_shape=jax.ShapeDtypeStruct((b, s, d), F32),
        scratch_shapes=[pltpu.SMEM((2, 2 * td), I32), pltpu.VMEM((2, 2, td * ROW_TILE, LANES), F32),
                        pltpu.SemaphoreType.DMA((2,)), pltpu.SemaphoreType.DMA((2,))],
        compiler_params=_cparams(("arbitrary", "arbitrary")),
        name="moe_combine",
    )(dest_tiles, ys, x1, route3, gt2, l2g, l2b)


def _tile(n, pref):
    t = min(n, pref)
    assert n % t == 0, (n, t)
    return t


def _rope_tables(s):
    half = HEAD_DIM // 2
    inv = ROPE_THETA ** (-jnp.arange(0, half, 2, dtype=F32) / half)
    pos = jnp.arange(s)
    row = (pos // GRID_W).astype(F32)[:, None] * inv[None, :]
    col = (pos % GRID_W).astype(F32)[:, None] * inv[None, :]
    cos = jnp.concatenate([jnp.cos(row), jnp.cos(row), jnp.cos(col), jnp.cos(col)], axis=-1)
    sin = jnp.concatenate([-jnp.sin(row), jnp.sin(row), -jnp.sin(col), jnp.sin(col)], axis=-1)
    return cos, sin


def _layer(x, c, w_ada, b_ada, w_in, conv_w, a_log, dt_bias, gdn_norm_g, q_norm_g, k_norm_g, attn_norm_g,
           w_out, ln1_g, ln1_b, w_group, b_group, w_router, b_router, w1, w3, w2, ln2_g, ln2_b):
    b, s, d = x.shape
    t = b * s
    assert s % CHUNK == 0 and s % GRID_W == 0

    c_pad = jnp.pad(c, ((0, (-b) % 8), (0, 0)))
    mod = _adaln_mod(c_pad, w_ada, b_ada.reshape(1, -1), _tile(6 * d, 1536))[:b]
    mod3 = mod.reshape(b, 6, d)

    gq, gk, gv, gz, gab, aq, ak, av = jnp.split(
        w_in, [GDN_WIDTH, 2 * GDN_WIDTH, 3 * GDN_WIDTH, 4 * GDN_WIDTH, 4 * GDN_WIDTH + N_GATES,
               4 * GDN_WIDTH + N_GATES + ATTN_WIDTH, 4 * GDN_WIDTH + N_GATES + ATTN_WIDTH + ATTN_KV_WIDTH], axis=1)
    w_packed = jnp.concatenate([gq, gk, gv, gz, ak, jnp.pad(gab, ((0, 0), (0, LANES - N_GATES)))], axis=1).astype(BF16)
    wqv_t = jnp.concatenate([aq, av], axis=1).T.astype(BF16)
    wab_t = gab.T.astype(BF16)
    cos, sin = _rope_tables(s)

    tm = _tile(s, 512)
    gqkv, z, ab, abt, a_qt, a_k, a_vt = _in_projection(
        x, mod3, w_packed, wqv_t, wab_t, cos, sin, cos.T, sin.T, q_norm_g.reshape(-1, 1), k_norm_g.reshape(1, -1), tm)

    pad_gates = lambda p: jnp.pad(p.reshape(1, -1), ((0, 0), (0, N_GATES - p.size)))
    alog_c = pad_gates(a_log)
    dtb_c = pad_gates(dt_bias)
    tc = _tile(s, 512)
    gq_n, gk_n, gv_n, gb, egc, egl, ea, gcr = _gdn_prep(
        gqkv, conv_w, ab, abt, alog_c, dtb_c, alog_c.reshape(-1, 1), dtb_c.reshape(-1, 1), tc)
    a_m, b_m, a_dg, q_eff, o_in = _gdn_local(gq_n, gk_n, gv_n, gb, egc, egl, ea, gcr, _tile(s, 256))
    o_f, o_b = _gdn_scan(a_m, b_m, a_dg, q_eff, o_in, _tile(s // CHUNK, 4))

    tq, tk = _tile(s, 1024), _tile(s, 4096)
    att_lagged, excess = _attention_lagged(a_qt, a_k, a_vt, tq, tk)
    att = lax.cond(jnp.max(excess) > ATTN_LAG_LIMIT,
                   lambda: _attention(a_qt, a_k, a_vt, tq, tk), lambda: att_lagged)

    w_rt = jnp.pad(jnp.concatenate([w_group, w_router], axis=1), ((0, 0), (0, LANES - N_GROUPS - N_EXPERTS)))
    b_rt = jnp.pad(jnp.concatenate([b_group, b_router]).reshape(1, -1), ((0, 0), (0, LANES - N_GROUPS - N_EXPERTS)))
    x1, h2, route = _mixer_out(o_f, o_b, z, att, x, mod3, w_out.astype(BF16), gdn_norm_g.reshape(1, -1),
                               attn_norm_g.reshape(-1, 1), ln1_g.reshape(1, -1), ln1_b.reshape(1, -1),
                               w_rt, b_rt, _tile(s, 512))

    blk = 256
    th = _tile(t, 1024)
    route2 = route.reshape(t, ROUTE_W)
    rank, counts = _expert_ranks(route2, th)
    counts_i = counts[0, :N_EXPERTS].astype(I32)
    padded = (counts_i + blk - 1) // blk * blk
    pad_end = jnp.cumsum(padded)
    pad_start = pad_end - padded
    n_blocks = -(-(2 * t) // blk) + N_EXPERTS
    blk_pos = jnp.arange(n_blocks, dtype=I32) * blk
    blk_expert = jnp.minimum(jnp.sum((pad_end[None, :] <= blk_pos[:, None]).astype(I32), axis=1), N_EXPERTS - 1)
    n_used = jnp.maximum(pad_end[-1:] // blk, 1).astype(I32)
    start_row = jnp.pad(pad_start.astype(F32).reshape(1, -1), ((0, 0), (0, LANES - N_EXPERTS)))
    dest = _expert_dest(route2, rank, start_row, _tile(t, 4096))

    td = _tile(s, 256)
    dest_tiles = dest[:, 0:2].reshape(t // td, 2 * td)
    assert d == ROW_TILE * LANES
    xs = _dispatch(dest_tiles, h2, jnp.zeros((n_blocks * blk * ROW_TILE, LANES), F32), td)
    ys = _experts(blk_expert, n_used, xs, w1, w3, w2, blk)
    gt2 = mod3[:, 5:6, :]
    return _combine(dest_tiles, ys, x1, route, gt2, ln2_g.reshape(1, -1), ln2_b.reshape(1, -1), td)


def kernel(x, c, w_ada, b_ada, w_in, conv_w, a_log, dt_bias, gdn_norm_g, q_norm_g, k_norm_g, attn_norm_g,
           w_out, ln1_g, ln1_b, w_group, b_group, w_router, b_router, w1, w3, w2, ln2_g, ln2_b):
    for layer in range(w_ada.shape[0]):
        x = _layer(x, c, w_ada[layer], b_ada[layer], w_in[layer], conv_w[layer], a_log[layer], dt_bias[layer],
                   gdn_norm_g[layer], q_norm_g[layer], k_norm_g[layer], attn_norm_g[layer], w_out[layer],
                   ln1_g[layer], ln1_b[layer], w_group[layer], b_group[layer], w_router[layer], b_router[layer],
                   w1[layer], w3[layer], w2[layer], ln2_g[layer], ln2_b[layer])
    return x
```

```python
import math

import jax
import jax.numpy as jnp
from jax import lax
from jax.experimental import pallas as pl
from jax.experimental.pallas import tpu as pltpu

F32 = jnp.float32
BF16 = jnp.bfloat16
I32 = jnp.int32

HEAD_DIM = 128
GDN_HEADS = 4
GDN_WIDTH = GDN_HEADS * HEAD_DIM
ATTN_Q_HEADS = 4
ATTN_KV_HEADS = 2
ATTN_GROUP = ATTN_Q_HEADS // ATTN_KV_HEADS
ATTN_WIDTH = ATTN_Q_HEADS * HEAD_DIM
ATTN_KV_WIDTH = ATTN_KV_HEADS * HEAD_DIM
CONV_K = 5
CHUNK = 64
GRID_W = 64
ROPE_THETA = 10000.0
N_GROUPS = 4
EXPERTS_PER_GROUP = 8
N_EXPERTS = N_GROUPS * EXPERTS_PER_GROUP
D_EXPERT = 256
DEPTH = 1
ALPHA = (2.0 * DEPTH) ** 0.25
LN_EPS = 1e-5
RMS_EPS = 1e-6

LANES = 128
VMEM_LIMIT_BYTES = 56 * 1024 * 1024
NEG_BIG = -1e30


def _cparams(semantics):
    return pltpu.CompilerParams(dimension_semantics=semantics, vmem_limit_bytes=VMEM_LIMIT_BYTES)


def _dot(a, b):
    return jnp.dot(a, b, preferred_element_type=F32)


def _dot_nt(a, b):
    return lax.dot_general(a, b, (((1,), (1,)), ((), ())), preferred_element_type=F32)


def _dot_tn(a, b):
    return lax.dot_general(a, b, (((0,), (0,)), ((), ())), preferred_element_type=F32)


def _split3(a):
    hi = a.astype(BF16)
    r = a - hi.astype(F32)
    mid = r.astype(BF16)
    lo = (r - mid.astype(F32)).astype(BF16)
    return hi, mid, lo


def _dot_f32_lhs_exact(a_bf16_exact, b):
    hi, mid, lo = _split3(b)
    return _dot(a_bf16_exact, hi) + _dot(a_bf16_exact, mid) + _dot(a_bf16_exact, lo)


def _dot_f32_rhs_exact(a, b_bf16_exact):
    hi, mid, lo = _split3(a)
    return _dot(hi, b_bf16_exact) + _dot(mid, b_bf16_exact) + _dot(lo, b_bf16_exact)


def _dot_f32(a, b):
    ah, am, _ = _split3(a)
    bh, bm, _ = _split3(b)
    return _dot(ah, bh) + (_dot(ah, bm) + _dot(am, bh))


ROW_TILE = 8


def _store_tile_rows(ref, lead, row0, value):
    n = value.shape[0]
    for c in range(ROW_TILE):
        ref[lead + (pl.ds(ROW_TILE * row0 + c, n, stride=ROW_TILE), slice(None))] = value[:, c * LANES:(c + 1) * LANES]


def _load_tile_rows(ref, lead, row0, n):
    return jnp.concatenate(
        [ref[lead + (pl.ds(ROW_TILE * row0 + c, n, stride=ROW_TILE), slice(None))] for c in range(ROW_TILE)], axis=1)


def _sigmoid(x):
    return 1.0 / (1.0 + jnp.exp(-x))


def _silu(x):
    return x * _sigmoid(x)


def _softplus(x):
    return jnp.maximum(x, 0.0) + jnp.log1p(jnp.exp(-jnp.abs(x)))


def _layer_norm(x):
    mu = jnp.mean(x, axis=-1, keepdims=True)
    xc = x - mu
    var = jnp.mean(xc * xc, axis=-1, keepdims=True)
    return xc * lax.rsqrt(var + LN_EPS)


def _mod_kernel(c_ref, w_ref, b_ref, o_ref):
    o_ref[...] = _dot_f32(_silu(c_ref[...]), w_ref[...]) + b_ref[...]


def _adaln_mod(c_pad, w_ada, b_ada, tn):
    rows, d = c_pad.shape
    n = w_ada.shape[1]
    return pl.pallas_call(
        _mod_kernel,
        grid=(n // tn,),
        in_specs=[pl.BlockSpec((rows, d), lambda j: (0, 0)),
                  pl.BlockSpec((d, tn), lambda j: (0, j)),
                  pl.BlockSpec((1, tn), lambda j: (0, j))],
        out_specs=pl.BlockSpec((rows, tn), lambda j: (0, j)),
        out_shape=jax.ShapeDtypeStruct((rows, n), F32),
        compiler_params=_cparams(("arbitrary",)),
        name="adaln_mod",
    )(c_pad, w_ada, b_ada)


_W_GQKV = 0
_W_Z = 3 * GDN_WIDTH
_W_AK = _W_Z + GDN_WIDTH
_W_AB = _W_AK + ATTN_KV_WIDTH
_W_COLS = _W_AB + LANES
N_GATES = 4 * GDN_HEADS
LOG2E = math.log2(math.e)


def _rope(xh, cos, sin_signed, lane):
    fwd = pltpu.roll(xh, 32, 1)
    bwd = pltpu.roll(xh, LANES - 32, 1)
    partner = jnp.where((lane % 64) < 32, bwd, fwd)
    return xh * cos + partner * sin_signed


def _rope_t(xt, cos_t, sin_signed_t):
    q = HEAD_DIM // 4
    partner = jnp.concatenate([xt[q:2 * q], xt[0:q], xt[3 * q:4 * q], xt[2 * q:3 * q]], axis=0)
    return xt * cos_t + partner * sin_signed_t


def _inproj_kernel(x_ref, mod_ref, w_ref, wqvt_ref, wabt_ref, cos_ref, sin_ref, cost_ref, sint_ref, qg_ref, kg_ref,
                   gqkv_ref, z_ref, ab_ref, abt_ref, aqt_ref, ak_ref, avt_ref):
    sh1 = mod_ref[0, 0:1, :]
    sc1 = mod_ref[0, 1:2, :]
    q_scale = (HEAD_DIM ** -0.5) * LOG2E
    h = _layer_norm(x_ref[0]) * (1.0 + sc1) + sh1
    hb = h.astype(BF16)
    gqkv_ref[0] = _dot(hb, w_ref[:, _W_GQKV:_W_Z]).astype(BF16)
    z_ref[0] = _dot(hb, w_ref[:, _W_Z:_W_AK]).astype(BF16)
    ab_ref[0] = _dot(hb, w_ref[:, _W_AB:_W_COLS])[:, 0:N_GATES]
    abt_ref[0] = _dot_nt(wabt_ref[...], hb)
    ak = _dot(hb, w_ref[:, _W_AK:_W_AB])
    cos = cos_ref[...]
    sin = sin_ref[...]
    lane = lax.broadcasted_iota(I32, cos.shape, 1)
    for j in range(ATTN_KV_HEADS):
        xh = ak[:, j * HEAD_DIM:(j + 1) * HEAD_DIM]
        xn = xh * lax.rsqrt(jnp.mean(xh * xh, axis=-1, keepdims=True) + RMS_EPS) * kg_ref[...]
        ak_ref[0, :, j * HEAD_DIM:(j + 1) * HEAD_DIM] = _rope(xn, cos, sin, lane).astype(BF16)
    qvt = _dot_nt(wqvt_ref[...], hb)
    cos_t = cost_ref[...]
    sin_t = sint_ref[...]
    for i in range(ATTN_Q_HEADS):
        xt = qvt[i * HEAD_DIM:(i + 1) * HEAD_DIM, :]
        xn = xt * lax.rsqrt(jnp.mean(xt * xt, axis=0, keepdims=True) + RMS_EPS) * qg_ref[...]
        aqt_ref[0, i * HEAD_DIM:(i + 1) * HEAD_DIM, :] = (_rope_t(xn, cos_t, sin_t) * q_scale).astype(BF16)
    avt_ref[0] = qvt[ATTN_WIDTH:, :].astype(BF16)


def _in_projection(x, mod3, w_packed, wqv_t, wab_t, cos, sin, cos_t, sin_t, qg_col, kg_row, tm):
    b, s, d = x.shape
    grid = (b, s // tm)
    row = lambda w: pl.BlockSpec((1, tm, w), lambda bi, i: (bi, i, 0))
    col = lambda h: pl.BlockSpec((1, h, tm), lambda bi, i: (bi, 0, i))
    const = lambda shape: pl.BlockSpec(shape, lambda bi, i: (0,) * len(shape))
    out_shapes = [
        jax.ShapeDtypeStruct((b, s, 3 * GDN_WIDTH), BF16),
        jax.ShapeDtypeStruct((b, s, GDN_WIDTH), BF16),
        jax.ShapeDtypeStruct((b, s, N_GATES), F32),
        jax.ShapeDtypeStruct((b, N_GATES, s), F32),
        jax.ShapeDtypeStruct((b, ATTN_WIDTH, s), BF16),
        jax.ShapeDtypeStruct((b, s, ATTN_KV_WIDTH), BF16),
        jax.ShapeDtypeStruct((b, ATTN_KV_WIDTH, s), BF16),
    ]
    out_specs = [row(3 * GDN_WIDTH), row(GDN_WIDTH), row(N_GATES), col(N_GATES),
                 col(ATTN_WIDTH), row(ATTN_KV_WIDTH), col(ATTN_KV_WIDTH)]
    return pl.pallas_call(
        _inproj_kernel,
        grid=grid,
        in_specs=[row(d),
                  pl.BlockSpec((1, 6, d), lambda bi, i: (bi, 0, 0)),
                  const((d, _W_COLS)),
                  const((ATTN_WIDTH + ATTN_KV_WIDTH, d)),
                  const((N_GATES, d)),
                  pl.BlockSpec((tm, HEAD_DIM), lambda bi, i: (i, 0)),
                  pl.BlockSpec((tm, HEAD_DIM), lambda bi, i: (i, 0)),
                  pl.BlockSpec((HEAD_DIM, tm), lambda bi, i: (0, i)),
                  pl.BlockSpec((HEAD_DIM, tm), lambda bi, i: (0, i)),
                  const((HEAD_DIM, 1)),
                  const((1, HEAD_DIM))],
        out_specs=out_specs,
        out_shape=out_shapes,
        compiler_params=_cparams(("arbitrary", "arbitrary")),
        name="in_projection",
    )(x, mod3, w_packed, wqv_t, wab_t, cos, sin, cos_t, sin_t, qg_col, kg_row)


_HALO = 16


def _chunk_masks(n):
    r = lax.broadcasted_iota(I32, (n, n), 0)
    c = lax.broadcasted_iota(I32, (n, n), 1)
    same = (r // CHUNK) == (c // CHUNK)
    lower = same & (r >= c)
    upper = same & (r <= c)
    return lower, upper


def _gdn_prep_kernel(prev_ref, main_ref, next_ref, convw_ref, ab_ref, abt_ref,
                     alog_c_ref, dtb_c_ref, alog_r_ref, dtb_r_ref,
                     q_ref, k_ref, v_ref, gb_ref, egc_ref, egl_ref, ea_ref, gcr_ref, buf_ref):
    i = pl.program_id(1)
    n_i = pl.num_programs(1)
    tc = main_ref.shape[1]
    prev_scale = jnp.where(i > 0, 1.0, 0.0).astype(F32)
    next_scale = jnp.where(i < n_i - 1, 1.0, 0.0).astype(F32)
    buf_ref[0:_HALO, :] = prev_ref[0].astype(F32) * prev_scale
    buf_ref[_HALO:_HALO + tc, :] = main_ref[0].astype(F32)
    buf_ref[_HALO + tc:, :] = next_ref[0].astype(F32) * next_scale
    pad = CONV_K // 2
    for part, out_ref in enumerate((q_ref, k_ref, v_ref)):
        for hh in range(GDN_HEADS):
            c0 = part * GDN_WIDTH + hh * HEAD_DIM
            acc = None
            for j in range(CONV_K):
                term = buf_ref[_HALO - pad + j:_HALO - pad + j + tc, c0:c0 + HEAD_DIM] * convw_ref[j:j + 1, c0:c0 + HEAD_DIM]
                acc = term if acc is None else acc + term
            y = _silu(acc)
            if part < 2:
                y = y * lax.rsqrt(jnp.sum(y * y, axis=-1, keepdims=True) + RMS_EPS)
            if part == 0:
                y = y * (HEAD_DIM ** -0.5)
            out_ref[0, :, hh * HEAD_DIM:(hh + 1) * HEAD_DIM] = y.astype(BF16)

    lower, upper = _chunk_masks(tc)
    lower_b = jnp.where(lower, 1.0, 0.0).astype(BF16)
    upper_b = jnp.where(upper, 1.0, 0.0).astype(BF16)
    n_dir = 2 * GDN_HEADS
    ab = ab_ref[0]
    lane = lax.broadcasted_iota(I32, ab.shape, 1)
    g = -jnp.exp(alog_c_ref[...]) * _softplus(ab + dtb_c_ref[...])
    pre = _dot_f32_lhs_exact(lower_b, g)
    suf = _dot_f32_lhs_exact(upper_b, g)
    gtot = pre + suf - g
    gc = jnp.where(lane < GDN_HEADS, pre, suf)
    gb_ref[0] = jnp.where(lane < n_dir, gc, _sigmoid(ab))
    egc_ref[0] = jnp.exp(gc)
    egl_ref[0] = jnp.exp(gtot - gc)
    ea_ref[0] = jnp.exp(gtot)
    abt = abt_ref[0]
    row = lax.broadcasted_iota(I32, abt.shape, 0)
    g_r = -jnp.exp(alog_r_ref[...]) * _softplus(abt + dtb_r_ref[...])
    pre_r = _dot_f32_rhs_exact(g_r, upper_b)
    suf_r = _dot_f32_rhs_exact(g_r, lower_b)
    gcr_ref[0] = jnp.where(row < GDN_HEADS, pre_r, suf_r)[0:n_dir]


def _gdn_prep(gqkv, conv_w, ab, abt, alog_c, dtb_c, alog_r, dtb_r, tc):
    b, s, w = gqkv.shape
    nh = tc // _HALO
    n_dir = 2 * GDN_HEADS
    grid = (b, s // tc)
    row = lambda width: pl.BlockSpec((1, tc, width), lambda bi, i: (bi, i, 0))
    const = lambda shape: pl.BlockSpec(shape, lambda bi, i: (0,) * len(shape))
    last_halo = s // _HALO - 1
    out_shapes = ([jax.ShapeDtypeStruct((b, s, GDN_WIDTH), BF16)] * 3
                  + [jax.ShapeDtypeStruct((b, s, N_GATES), F32)] * 4
                  + [jax.ShapeDtypeStruct((b, n_dir, s), F32)])
    out_specs = ([row(GDN_WIDTH)] * 3 + [row(N_GATES)] * 4
                 + [pl.BlockSpec((1, n_dir, tc), lambda bi, i: (bi, 0, i))])
    return pl.pallas_call(
        _gdn_prep_kernel,
        grid=grid,
        in_specs=[pl.BlockSpec((1, _HALO, w), lambda bi, i: (bi, jnp.maximum(i * nh - 1, 0), 0)),
                  row(w),
                  pl.BlockSpec((1, _HALO, w), lambda bi, i: (bi, jnp.minimum((i + 1) * nh, last_halo), 0)),
                  const((CONV_K, w)),
                  row(N_GATES),
                  pl.BlockSpec((1, N_GATES, tc), lambda bi, i: (bi, 0, i)),
                  const((1, N_GATES)), const((1, N_GATES)),
                  const((N_GATES, 1)), const((N_GATES, 1))],
        out_specs=out_specs,
        out_shape=out_shapes,
        scratch_shapes=[pltpu.VMEM((tc + 2 * _HALO, w), F32)],
        compiler_params=_cparams(("arbitrary", "arbitrary")),
        name="gdn_prep",
    )(gqkv, gqkv, gqkv, conv_w, ab, abt, alog_c, dtb_c, alog_r, dtb_r)


def _gdn_local_kernel(q_ref, k_ref, v_ref, gb_ref, egc_ref, egl_ref, ea_ref, gcr_ref,
                      a_ref, b_ref, dg_ref, qe_ref, oi_ref):
    tc = q_ref.shape[1]
    nch = tc // CHUNK
    lower, upper = _chunk_masks(tc)
    r = lax.broadcasted_iota(I32, (tc, tc), 0)
    c = lax.broadcasted_iota(I32, (tc, tc), 1)
    eye = r == c
    blk_xor = r ^ c
    n_dir = 2 * GDN_HEADS
    combos = [(d, hh) for d in range(2) for hh in range(GDN_HEADS)]
    head = lambda hh: slice(hh * HEAD_DIM, (hh + 1) * HEAD_DIM)
    kk = [_dot_nt(k_ref[0, :, head(hh)], k_ref[0, :, head(hh)]) for hh in range(GDN_HEADS)]
    qk = [_dot_nt(q_ref[0, :, head(hh)], k_ref[0, :, head(hh)]) for hh in range(GDN_HEADS)]
    side_r = lax.broadcasted_iota(I32, (CHUNK, tc), 0)
    side_c = lax.broadcasted_iota(I32, (CHUNK, tc), 1)
    side_eye = (side_c % CHUNK) == side_r
    same_chunk = blk_xor < CHUNK

    def side_by_side(bd):
        out = None
        for ch in range(nch):
            part = jnp.where((side_c // CHUNK) == ch, bd[ch * CHUNK:(ch + 1) * CHUNK, :], 0.0)
            out = part if out is None else out + part
        return out

    def block_diag(ss):
        return jnp.where(same_chunk, jnp.concatenate([ss] * nch, axis=0), 0.0)

    decay, m, t = {}, {}, {}
    for cb in combos:
        d, hh = cb
        idx = d * GDN_HEADS + hh
        incl = lower if d == 0 else upper
        gc = gb_ref[0, :, idx:idx + 1]
        beta = gb_ref[0, :, n_dir + idx:n_dir + idx + 1]
        gcr = gcr_ref[0, idx:idx + 1, :]
        decay[cb] = jnp.where(incl, jnp.exp(jnp.minimum(gc - gcr, 0.0)), 0.0)
        m[cb] = jnp.where(eye, 0.0, beta * kk[hh] * decay[cb])
        t[cb] = jnp.where(side_eye, 1.0, 0.0) - side_by_side(jnp.where(blk_xor < 2, m[cb], 0.0))
    sz = 2
    while sz < CHUNK:
        join = (blk_xor >= sz) & (blk_xor < 2 * sz)
        tc_s = {cb: _dot(t[cb].astype(BF16), jnp.where(join, m[cb], 0.0).astype(BF16)).astype(BF16) for cb in combos}
        for cb in combos:
            t[cb] = t[cb] - _dot(tc_s[cb], block_diag(t[cb]).astype(BF16))
        sz *= 2
    t = {cb: block_diag(t[cb]) for cb in combos}
    solb = {}
    for cb in combos:
        d, hh = cb
        idx = d * GDN_HEADS + hh
        beta = gb_ref[0, :, n_dir + idx:n_dir + idx + 1]
        egc = egc_ref[0, :, idx:idx + 1]
        rhs = jnp.concatenate([v_ref[0, :, head(hh)].astype(F32) * beta,
                               k_ref[0, :, head(hh)].astype(F32) * (beta * egc)], axis=1).astype(BF16)
        solb[cb] = _dot(t[cb].astype(BF16), rhs).astype(BF16)
    for cb in combos:
        d, hh = cb
        idx = d * GDN_HEADS + hh
        egc = egc_ref[0, :, idx:idx + 1]
        qo = _dot((qk[hh] * decay[cb]).astype(BF16), solb[cb])
        oi_ref[0, d, :, head(hh)] = qo[:, :HEAD_DIM].astype(BF16)
        qe_ref[0, d, :, head(hh)] = (q_ref[0, :, head(hh)].astype(F32) * egc - qo[:, HEAD_DIM:]).astype(BF16)
    for cb in combos:
        d, hh = cb
        idx = d * GDN_HEADS + hh
        egl = egl_ref[0, :, idx:idx + 1]
        ea = ea_ref[0, :, idx:idx + 1]
        kg = (k_ref[0, :, head(hh)].astype(F32) * egl).astype(BF16)
        for ci in range(nch):
            rs = slice(ci * CHUNK, (ci + 1) * CHUNK)
            ab = _dot_tn(kg[rs], solb[cb][rs])
            a_ref[0, d, ci, :, head(hh)] = (-ab[:, HEAD_DIM:]).astype(BF16)
            b_ref[0, d, ci, :, head(hh)] = ab[:, :HEAD_DIM].astype(BF16)
            dg_ref[0, d, ci, :, head(hh)] = jnp.broadcast_to(ea[ci * CHUNK:ci * CHUNK + 1, :], (8, HEAD_DIM))


def _gdn_local(q, k, v, gb, egc, egl, ea, gcr, tc):
    b, s, w = q.shape
    n = s // CHUNK
    nch = tc // CHUNK
    n_dir = 2 * GDN_HEADS
    grid = (b, s // tc)
    row = lambda width: pl.BlockSpec((1, tc, width), lambda bi, i: (bi, i, 0))
    return pl.pallas_call(
        _gdn_local_kernel,
        grid=grid,
        in_specs=[row(w), row(w), row(w), row(N_GATES), row(N_GATES), row(N_GATES), row(N_GATES),
                  pl.BlockSpec((1, n_dir, tc), lambda bi, i: (bi, 0, i))],
        out_specs=[pl.BlockSpec((1, 2, nch, HEAD_DIM, w), lambda bi, i: (bi, 0, i, 0, 0)),
                   pl.BlockSpec((1, 2, nch, HEAD_DIM, w), lambda bi, i: (bi, 0, i, 0, 0)),
                   pl.BlockSpec((1, 2, nch, 8, w), lambda bi, i: (bi, 0, i, 0, 0)),
                   pl.BlockSpec((1, 2, tc, w), lambda bi, i: (bi, 0, i, 0)),
                   pl.BlockSpec((1, 2, tc, w), lambda bi, i: (bi, 0, i, 0))],
        out_shape=[jax.ShapeDtypeStruct((b, 2, n, HEAD_DIM, w), BF16),
                   jax.ShapeDtypeStruct((b, 2, n, HEAD_DIM, w), BF16),
                   jax.ShapeDtypeStruct((b, 2, n, 8, w), F32),
                   jax.ShapeDtypeStruct((b, 2, s, w), BF16),
                   jax.ShapeDtypeStruct((b, 2, s, w), BF16)],
        compiler_params=_cparams(("arbitrary", "arbitrary")),
        name="gdn_local",
    )(q, k, v, gb, egc, egl, ea, gcr)


def _gdn_scan_kernel(af_ref, bf_ref, df_ref, qf_ref, of_ref, ab_ref, bb_ref, db_ref, qb_ref, ob_ref,
                     outf_ref, outb_ref, s_ref):
    @pl.when(pl.program_id(1) == 0)
    def _():
        s_ref[...] = jnp.zeros_like(s_ref)

    cs = af_ref.shape[2]
    dirs = ((af_ref, bf_ref, df_ref, qf_ref, of_ref, outf_ref), (ab_ref, bb_ref, db_ref, qb_ref, ob_ref, outb_ref))
    for j in range(cs):
        for d, (a_ref, b_ref, dg_ref, q_ref, o_ref, out_ref) in enumerate(dirs):
            ci = j if d == 0 else cs - 1 - j
            rs = slice(ci * CHUNK, (ci + 1) * CHUNK)
            for hh in range(GDN_HEADS):
                hs = slice(hh * HEAD_DIM, (hh + 1) * HEAD_DIM)
                si = d * GDN_HEADS + hh
                st = s_ref[si]
                stb = st.astype(BF16)
                out_ref[0, rs, hs] = (_dot(q_ref[0, 0, rs, hs], stb) + o_ref[0, 0, rs, hs].astype(F32)).astype(BF16)
                s_ref[si] = (dg_ref[0, 0, ci, 0:1, hs] * st + _dot(a_ref[0, 0, ci, :, hs], stb)
                             + b_ref[0, 0, ci, :, hs].astype(F32))


def _gdn_scan(a, bm, dg, qe, oi, cs):
    b, _, n, _, w = a.shape
    s = n * CHUNK
    ns = n // cs
    tr = cs * CHUNK
    fwd5 = lambda bi, i: (bi, 0, i, 0, 0)
    bwd5 = lambda bi, i: (bi, 1, ns - 1 - i, 0, 0)
    fwd4 = lambda bi, i: (bi, 0, i, 0)
    bwd4 = lambda bi, i: (bi, 1, ns - 1 - i, 0)
    blk5 = (1, 1, cs, HEAD_DIM, w)
    blkd = (1, 1, cs, 8, w)
    blk4 = (1, 1, tr, w)
    return pl.pallas_call(
        _gdn_scan_kernel,
        grid=(b, ns),
        in_specs=[pl.BlockSpec(blk5, fwd5), pl.BlockSpec(blk5, fwd5), pl.BlockSpec(blkd, fwd5),
                  pl.BlockSpec(blk4, fwd4), pl.BlockSpec(blk4, fwd4),
                  pl.BlockSpec(blk5, bwd5), pl.BlockSpec(blk5, bwd5), pl.BlockSpec(blkd, bwd5),
                  pl.BlockSpec(blk4, bwd4), pl.BlockSpec(blk4, bwd4)],
        out_specs=[pl.BlockSpec((1, tr, w), lambda bi, i: (bi, i, 0)),
                   pl.BlockSpec((1, tr, w), lambda bi, i: (bi, ns - 1 - i, 0))],
        out_shape=[jax.ShapeDtypeStruct((b, s, w), BF16)] * 2,
        scratch_shapes=[pltpu.VMEM((2 * GDN_HEADS, HEAD_DIM, HEAD_DIM), F32)],
        compiler_params=_cparams(("arbitrary", "arbitrary")),
        name="gdn_scan",
    )(a, bm, dg, qe, oi, a, bm, dg, qe, oi)


ATTN_SUB = 512
ATTN_LAG_LIMIT = 64.0


def _attn_lagged_kernel(qt_ref, k_ref, vt_ref, ot_ref, ex_ref, q2t_ref, m_ref, l_ref, acc_ref, exc_ref):
    j = pl.program_id(3)
    tq = qt_ref.shape[2]
    tk = k_ref.shape[1]

    @pl.when(j == 0)
    def _():
        for gi in range(ATTN_GROUP):
            q2t_ref[:, gi * tq:(gi + 1) * tq] = qt_ref[0, gi * HEAD_DIM:(gi + 1) * HEAD_DIM, :]
        m_ref[...] = jnp.max(_dot(k_ref[0, 0:ATTN_SUB, :], q2t_ref[...]), axis=0, keepdims=True)
        l_ref[...] = jnp.zeros_like(l_ref)
        acc_ref[...] = jnp.zeros_like(acc_ref)
        exc_ref[...] = jnp.zeros_like(exc_ref)

    q2t = q2t_ref[...]
    m_est = m_ref[...]
    m_run = m_est
    l_run = l_ref[...]
    n_sub = tk // ATTN_SUB
    scores = lambda jj: _dot(k_ref[0, jj * ATTN_SUB:(jj + 1) * ATTN_SUB, :], q2t)
    st_next = scores(0)
    for jj in range(n_sub):
        ks = slice(jj * ATTN_SUB, (jj + 1) * ATTN_SUB)
        st = st_next
        if jj + 1 < n_sub:
            st_next = scores(jj + 1)
        p = jnp.exp2(st - m_est)
        m_run = jnp.maximum(m_run, jnp.max(st, axis=0, keepdims=True))
        l_run = l_run + jnp.sum(p, axis=0, keepdims=True)
        acc_ref[...] += _dot(vt_ref[0, :, ks], p.astype(BF16))
    rebase = jnp.exp2(m_est - m_run)
    acc_ref[...] = acc_ref[...] * rebase
    l_ref[...] = l_run * rebase
    m_ref[...] = m_run
    exc_ref[...] = jnp.maximum(exc_ref[...], m_run - m_est)

    @pl.when(j == pl.num_programs(3) - 1)
    def _():
        out = acc_ref[...] / l_ref[...]
        for gi in range(ATTN_GROUP):
            ot_ref[0, gi * HEAD_DIM:(gi + 1) * HEAD_DIM, :] = out[:, gi * tq:(gi + 1) * tq].astype(BF16)
        ex_ref[0] = exc_ref[...]


def _attention_lagged(q_t, k, v_t, tq, tk):
    b, _, s = q_t.shape
    gw = ATTN_GROUP * HEAD_DIM
    nq = s // tq
    assert tk % ATTN_SUB == 0
    return pl.pallas_call(
        _attn_lagged_kernel,
        grid=(b, ATTN_KV_HEADS, nq, s // tk),
        in_specs=[pl.BlockSpec((1, gw, tq), lambda bi, g, i, j: (bi, g, i)),
                  pl.BlockSpec((1, tk, HEAD_DIM), lambda bi, g, i, j: (bi, j, g)),
                  pl.BlockSpec((1, HEAD_DIM, tk), lambda bi, g, i, j: (bi, g, j))],
        out_specs=[pl.BlockSpec((1, gw, tq), lambda bi, g, i, j: (bi, g, i)),
                   pl.BlockSpec((1, 1, ATTN_GROUP * tq), lambda bi, g, i, j: ((bi * ATTN_KV_HEADS + g) * nq + i, 0, 0))],
        out_shape=[jax.ShapeDtypeStruct((b, ATTN_WIDTH, s), BF16),
                   jax.ShapeDtypeStruct((b * ATTN_KV_HEADS * nq, 1, ATTN_GROUP * tq), F32)],
        scratch_shapes=[pltpu.VMEM((HEAD_DIM, ATTN_GROUP * tq), BF16),
                        pltpu.VMEM((1, ATTN_GROUP * tq), F32),
                        pltpu.VMEM((1, ATTN_GROUP * tq), F32),
                        pltpu.VMEM((HEAD_DIM, ATTN_GROUP * tq), F32),
                        pltpu.VMEM((1, ATTN_GROUP * tq), F32)],
        compiler_params=_cparams(("arbitrary", "arbitrary", "arbitrary", "arbitrary")),
        name="gqa_attention_lagged",
    )(q_t, k, v_t)


def _attn_kernel(qt_ref, k_ref, vt_ref, ot_ref, q2t_ref, m_ref, l_ref, acc_ref):
    j = pl.program_id(3)
    tq = qt_ref.shape[2]
    tk = k_ref.shape[1]

    @pl.when(j == 0)
    def _():
        for gi in range(ATTN_GROUP):
            q2t_ref[:, gi * tq:(gi + 1) * tq] = qt_ref[0, gi * HEAD_DIM:(gi + 1) * HEAD_DIM, :]
        m_ref[...] = jnp.full_like(m_ref, NEG_BIG)
        l_ref[...] = jnp.zeros_like(l_ref)
        acc_ref[...] = jnp.zeros_like(acc_ref)

    q2t = q2t_ref[...]
    m_prev = m_ref[...]
    l_prev = l_ref[...]
    n_sub = tk // ATTN_SUB
    scores = lambda jj: _dot(k_ref[0, jj * ATTN_SUB:(jj + 1) * ATTN_SUB, :], q2t)
    st_next = scores(0)
    for jj in range(n_sub):
        ks = slice(jj * ATTN_SUB, (jj + 1) * ATTN_SUB)
        st = st_next
        if jj + 1 < n_sub:
            st_next = scores(jj + 1)
        m_new = jnp.maximum(m_prev, jnp.max(st, axis=0, keepdims=True))
        alpha = jnp.exp2(m_prev - m_new)
        p = jnp.exp2(st - m_new)
        l_prev = alpha * l_prev + jnp.sum(p, axis=0, keepdims=True)
        acc_ref[...] = alpha * acc_ref[...] + _dot(vt_ref[0, :, ks], p.astype(BF16))
        m_prev = m_new
    m_ref[...] = m_prev
    l_ref[...] = l_prev

    @pl.when(j == pl.num_programs(3) - 1)
    def _():
        out = acc_ref[...] / l_ref[...]
        for gi in range(ATTN_GROUP):
            ot_ref[0, gi * HEAD_DIM:(gi + 1) * HEAD_DIM, :] = out[:, gi * tq:(gi + 1) * tq].astype(BF16)


def _attention(q_t, k, v_t, tq, tk):
    b, _, s = q_t.shape
    gw = ATTN_GROUP * HEAD_DIM
    assert tk % ATTN_SUB == 0
    return pl.pallas_call(
        _attn_kernel,
        grid=(b, ATTN_KV_HEADS, s // tq, s // tk),
        in_specs=[pl.BlockSpec((1, gw, tq), lambda bi, g, i, j: (bi, g, i)),
                  pl.BlockSpec((1, tk, HEAD_DIM), lambda bi, g, i, j: (bi, j, g)),
                  pl.BlockSpec((1, HEAD_DIM, tk), lambda bi, g, i, j: (bi, g, j))],
        out_specs=pl.BlockSpec((1, gw, tq), lambda bi, g, i, j: (bi, g, i)),
        out_shape=jax.ShapeDtypeStruct((b, ATTN_WIDTH, s), BF16),
        scratch_shapes=[pltpu.VMEM((HEAD_DIM, ATTN_GROUP * tq), BF16),
                        pltpu.VMEM((1, ATTN_GROUP * tq), F32),
                        pltpu.VMEM((1, ATTN_GROUP * tq), F32),
                        pltpu.VMEM((HEAD_DIM, ATTN_GROUP * tq), F32)],
        compiler_params=_cparams(("arbitrary", "arbitrary", "arbitrary", "arbitrary")),
        name="gqa_attention",
    )(q_t, k, v_t)


ROUTE_W = 8


def _first_argmax(vals, lane_f, valid):
    vmax = jnp.max(jnp.where(valid, vals, NEG_BIG), axis=-1, keepdims=True)
    idx = jnp.min(jnp.where(valid & (vals == vmax), lane_f, float(LANES)), axis=-1, keepdims=True)
    return vmax, idx


def _mix_kernel(of_ref, ob_ref, z_ref, att_ref, x_ref, mod_ref, wout_ref, gng_ref, ang_ref,
                l1g_ref, l1b_ref, wrt_ref, brt_ref, x1_ref, h2_ref, route_ref):
    o = of_ref[0].astype(F32) + ob_ref[0].astype(F32)
    z = z_ref[0].astype(F32)
    parts = []
    for hh in range(GDN_HEADS):
        hs = slice(hh * HEAD_DIM, (hh + 1) * HEAD_DIM)
        oh = o[:, hs]
        on = oh * lax.rsqrt(jnp.mean(oh * oh, axis=-1, keepdims=True) + RMS_EPS) * gng_ref[...]
        parts.append((on * _silu(z[:, hs])).astype(BF16))
    att_t = att_ref[0].astype(F32)
    attn_t = (att_t * lax.rsqrt(jnp.mean(att_t * att_t, axis=0, keepdims=True) + RMS_EPS) * ang_ref[...]).astype(BF16)
    mixed = _dot(jnp.concatenate(parts, axis=1), wout_ref[:GDN_WIDTH, :]) + _dot_tn(attn_t, wout_ref[GDN_WIDTH:, :])
    gt1 = mod_ref[0, 2:3, :]
    sh2 = mod_ref[0, 3:4, :]
    sc2 = mod_ref[0, 4:5, :]
    x1 = _layer_norm(ALPHA * x_ref[0] + gt1 * mixed) * l1g_ref[...] + l1b_ref[...]
    x1_ref[0] = x1
    h2 = _layer_norm(x1) * (1.0 + sc2) + sh2
    _store_tile_rows(h2_ref, (), 0, h2)
    logits = _dot_f32(h2, wrt_ref[...]) + brt_ref[...]
    lane = lax.broadcasted_iota(I32, logits.shape, 1)
    lane_f = lane.astype(F32)
    is_grp = lane < N_GROUPS
    gmax, gidx = _first_argmax(logits, lane_f, is_grp)
    gsum = jnp.sum(jnp.where(is_grp, jnp.exp(jnp.minimum(logits - gmax, 0.0)), 0.0), axis=-1, keepdims=True)
    grp_p = 1.0 / gsum
    lo = float(N_GROUPS) + float(EXPERTS_PER_GROUP) * gidx
    in_grp = (lane_f >= lo) & (lane_f < lo + float(EXPERTS_PER_GROUP))
    v0, i0 = _first_argmax(logits, lane_f, in_grp)
    v1, i1 = _first_argmax(logits, lane_f, in_grp & (lane_f != i0))
    e1 = jnp.exp(v1 - v0)
    w0 = grp_p / (1.0 + e1)
    w1 = grp_p * e1 / (1.0 + e1)
    route = jnp.where(lane == 0, i0 - float(N_GROUPS),
                      jnp.where(lane == 1, i1 - float(N_GROUPS),
                                jnp.where(lane == 2, w0, jnp.where(lane == 3, w1, 0.0))))
    route_ref[0] = route[:, 0:ROUTE_W]


def _mixer_out(o_f, o_b, z, att, x, mod3, w_out, gng, ang, l1g, l1b, w_rt, b_rt, tm):
    b, s, d = x.shape
    row = lambda w: pl.BlockSpec((1, tm, w), lambda bi, i: (bi, i, 0))
    const = lambda shape: pl.BlockSpec(shape, lambda bi, i: (0,) * len(shape))
    return pl.pallas_call(
        _mix_kernel,
        grid=(b, s // tm),
        in_specs=[row(GDN_WIDTH), row(GDN_WIDTH), row(GDN_WIDTH),
                  pl.BlockSpec((1, ATTN_WIDTH, tm), lambda bi, i: (bi, 0, i)), row(d),
                  pl.BlockSpec((1, 6, d), lambda bi, i: (bi, 0, 0)),
                  const(w_out.shape), const((1, HEAD_DIM)), const((ATTN_WIDTH, 1)),
                  const((1, d)), const((1, d)), const((d, LANES)), const((1, LANES))],
        out_specs=[row(d), pl.BlockSpec((tm * ROW_TILE, LANES), lambda bi, i: (bi * (s // tm) + i, 0)), row(ROUTE_W)],
        out_shape=[jax.ShapeDtypeStruct((b, s, d), F32),
                   jax.ShapeDtypeStruct((b * s * ROW_TILE, LANES), F32),
                   jax.ShapeDtypeStruct((b, s, ROUTE_W), F32)],
        compiler_params=_cparams(("arbitrary", "arbitrary")),
        name="mixer_out",
    )(o_f, o_b, z, att, x, mod3, w_out, gng, ang, l1g, l1b, w_rt, b_rt)


def _rank_kernel(route_ref, rank_ref, cnt_ref, carry_ref):
    @pl.when(pl.program_id(0) == 0)
    def _():
        carry_ref[...] = jnp.zeros_like(carry_ref)

    th = route_ref.shape[0]
    route = route_ref[...]
    lane_f = lax.broadcasted_iota(I32, (th, LANES), 1).astype(F32)
    oh0 = lane_f == route[:, 0:1]
    oh1 = lane_f == route[:, 1:2]
    both = jnp.where(oh0 | oh1, 1.0, 0.0).astype(BF16)
    r = lax.broadcasted_iota(I32, (th, th), 0)
    c = lax.broadcasted_iota(I32, (th, th), 1)
    before = _dot(jnp.where(r > c, 1.0, 0.0).astype(BF16), both) + carry_ref[...]
    rank0 = jnp.sum(jnp.where(oh0, before, 0.0), axis=-1, keepdims=True)
    rank1 = jnp.sum(jnp.where(oh1, before, 0.0), axis=-1, keepdims=True)
    lane8 = lax.broadcasted_iota(I32, (th, ROUTE_W), 1)
    rank_ref[...] = jnp.where(lane8 == 0, rank0, jnp.where(lane8 == 1, rank1, 0.0))
    total = carry_ref[...] + jnp.sum(both.astype(F32), axis=0, keepdims=True)
    carry_ref[...] = total
    cnt_ref[...] = total


def _expert_ranks(route, th):
    t = route.shape[0]
    return pl.pallas_call(
        _rank_kernel,
        grid=(t // th,),
        in_specs=[pl.BlockSpec((th, ROUTE_W), lambda i: (i, 0))],
        out_specs=[pl.BlockSpec((th, ROUTE_W), lambda i: (i, 0)),
                   pl.BlockSpec((1, LANES), lambda i: (0, 0))],
        out_shape=[jax.ShapeDtypeStruct((t, ROUTE_W), F32),
                   jax.ShapeDtypeStruct((1, LANES), F32)],
        scratch_shapes=[pltpu.VMEM((1, LANES), F32)],
        compiler_params=_cparams(("arbitrary",)),
        name="expert_ranks",
    )(route)


def _dest_kernel(route_ref, rank_ref, start_ref, dest_ref):
    th = route_ref.shape[0]
    route = route_ref[...]
    rank = rank_ref[...]
    lane_f = lax.broadcasted_iota(I32, (th, LANES), 1).astype(F32)
    start = start_ref[...]
    d0 = rank[:, 0:1] + jnp.sum(jnp.where(lane_f == route[:, 0:1], start, 0.0), axis=-1, keepdims=True)
    d1 = rank[:, 1:2] + jnp.sum(jnp.where(lane_f == route[:, 1:2], start, 0.0), axis=-1, keepdims=True)
    lane8 = lax.broadcasted_iota(I32, (th, ROUTE_W), 1)
    dest_ref[...] = (jnp.where(lane8 == 0, d0, jnp.where(lane8 == 1, d1, 0.0)) * float(ROW_TILE)).astype(I32)


def _expert_dest(route, rank, start_row, th):
    t = route.shape[0]
    return pl.pallas_call(
        _dest_kernel,
        grid=(t // th,),
        in_specs=[pl.BlockSpec((th, ROUTE_W), lambda i: (i, 0)),
                  pl.BlockSpec((th, ROUTE_W), lambda i: (i, 0)),
                  pl.BlockSpec((1, LANES), lambda i: (0, 0))],
        out_specs=pl.BlockSpec((th, ROUTE_W), lambda i: (i, 0)),
        out_shape=jax.ShapeDtypeStruct((t, ROUTE_W), I32),
        compiler_params=_cparams(("arbitrary",)),
        name="expert_dest",
    )(route, rank, start_row)


def _prefetched_indices(dest_hbm, idx_smem, idx_sem, tile, n_tiles):
    def idx_copy(t, slot):
        return pltpu.make_async_copy(dest_hbm.at[t], idx_smem.at[slot], idx_sem.at[slot])

    slot = tile % 2

    @pl.when(tile == 0)
    def _():
        idx_copy(0, 0).start()

    idx_copy(tile, slot).wait()

    @pl.when(tile + 1 < n_tiles)
    def _():
        idx_copy(tile + 1, 1 - slot).start()

    return slot


_DISPATCH_SLOTS = 3


def _dispatch_kernel(pad_end_ref, nused_ref, dest_hbm, h_hbm, xs_hbm, idx_smem, h_buf, zero_buf,
                     idx_sem, tile_sem, row_sem, zero_sem):
    td = h_buf.shape[1] // ROW_TILE
    blk_rows = zero_buf.shape[0]
    tile = pl.program_id(0)
    n_tiles = pl.num_programs(0)

    @pl.when(tile == 0)
    def _():
        def zero_block(first_row):
            return pltpu.make_async_copy(
                zero_buf, xs_hbm.at[pl.ds(pl.multiple_of(first_row, ROW_TILE), blk_rows)], zero_sem)

        zero_buf[...] = jnp.zeros_like(zero_buf)
        for e in range(N_EXPERTS):
            zero_block(jnp.maximum(pad_end_ref[e] * ROW_TILE - blk_rows, 0)).start()
        for e in range(N_EXPERTS):
            zero_block(0).wait()

        def zero_unused(bi, carry):
            zero_block(bi * blk_rows).start()
            zero_block(0).wait()
            return carry

        lax.fori_loop(nused_ref[0], xs_hbm.shape[0] // blk_rows, zero_unused, 0)

    idx_slot = _prefetched_indices(dest_hbm, idx_smem, idx_sem, tile, n_tiles)
    slot = tile % _DISPATCH_SLOTS
    sem_slot = tile % 2

    def tile_copy(t, sl):
        return pltpu.make_async_copy(h_hbm.at[pl.ds(t * (td * ROW_TILE), td * ROW_TILE)], h_buf.at[sl],
                                     tile_sem.at[sl])

    def row_copy(sl, sem_sl, r, dst_row):
        return pltpu.make_async_copy(h_buf.at[sl, pl.ds(ROW_TILE * r, ROW_TILE)],
                                     xs_hbm.at[pl.ds(pl.multiple_of(dst_row, ROW_TILE), ROW_TILE)], row_sem.at[sem_sl])

    @pl.when(tile == 0)
    def _():
        tile_copy(0, 0).start()

    @pl.when(tile + 1 < n_tiles)
    def _():
        tile_copy(tile + 1, (tile + 1) % _DISPATCH_SLOTS).start()

    tile_copy(tile, slot).wait()
    for n in range(2 * td):
        row_copy(slot, sem_slot, n // 2, idx_smem[idx_slot, n]).start(priority=n % 2)

    @pl.when(tile > 0)
    def _():
        for n in range(2 * td):
            row_copy(slot, 1 - sem_slot, n // 2, 0).wait()

    @pl.when(tile == n_tiles - 1)
    def _():
        for n in range(2 * td):
            row_copy(slot, sem_slot, n // 2, 0).wait()


def _dispatch(pad_end, n_used, dest_tiles, h2, n_blocks, blk, td):
    t = h2.shape[0] // ROW_TILE
    grid_spec = pltpu.PrefetchScalarGridSpec(
        num_scalar_prefetch=2,
        grid=(t // td,),
        in_specs=[pl.BlockSpec(memory_space=pl.ANY), pl.BlockSpec(memory_space=pl.ANY)],
        out_specs=pl.BlockSpec(memory_space=pl.ANY),
        scratch_shapes=[pltpu.SMEM((2, 2 * td), I32), pltpu.VMEM((_DISPATCH_SLOTS, td * ROW_TILE, LANES), F32),
                        pltpu.VMEM((blk * ROW_TILE, LANES), F32),
                        pltpu.SemaphoreType.DMA((2,)), pltpu.SemaphoreType.DMA((_DISPATCH_SLOTS,)),
                        pltpu.SemaphoreType.DMA((2,)), pltpu.SemaphoreType.DMA],
    )
    return pl.pallas_call(
        _dispatch_kernel,
        grid_spec=grid_spec,
        out_shape=jax.ShapeDtypeStruct((n_blocks * blk * ROW_TILE, LANES), F32),
        compiler_params=_cparams(("arbitrary",)),
        name="moe_dispatch",
    )(pad_end, n_used, dest_tiles, h2)


def _expert_kernel(be_ref, nused_ref, xs_ref, w1_ref, w3_ref, w2_ref, ys_ref, w13b_ref, w2b_ref):
    i = pl.program_id(0)
    changed = jnp.logical_or(i == 0, be_ref[i] != be_ref[jnp.maximum(i - 1, 0)])

    @pl.when(jnp.logical_and(changed, i < nused_ref[0]))
    def _():
        w13b_ref[:, :D_EXPERT] = w1_ref[0].astype(BF16)
        w13b_ref[:, D_EXPERT:] = w3_ref[0].astype(BF16)
        w2b_ref[...] = w2_ref[0].astype(BF16)

    @pl.when(i < nused_ref[0])
    def _():
        half = xs_ref.shape[0] // (2 * ROW_TILE)
        h13 = [_dot(_load_tile_rows(xs_ref, (), r0, half).astype(BF16), w13b_ref[...]) for r0 in (0, half)]
        for r0, h in zip((0, half), h13):
            hid = _silu(h[:, :D_EXPERT]) * h[:, D_EXPERT:]
            _store_tile_rows(ys_ref, (), r0, _dot(hid.astype(BF16), w2b_ref[...]))

    @pl.when(i >= nused_ref[0])
    def _():
        ys_ref[...] = jnp.zeros_like(ys_ref)


def _experts(blk_expert, n_used, xs, w1, w3, w2, blk):
    d = w1.shape[1]
    n_blocks = xs.shape[0] // (blk * ROW_TILE)
    row_map = lambda i, be, nu: (i, 0)
    grid_spec = pltpu.PrefetchScalarGridSpec(
        num_scalar_prefetch=2,
        grid=(n_blocks,),
        in_specs=[pl.BlockSpec((blk * ROW_TILE, LANES), row_map),
                  pl.BlockSpec((1, d, D_EXPERT), lambda i, be, nu: (be[i], 0, 0)),
                  pl.BlockSpec((1, d, D_EXPERT), lambda i, be, nu: (be[i], 0, 0)),
                  pl.BlockSpec((1, D_EXPERT, d), lambda i, be, nu: (be[i], 0, 0))],
        out_specs=pl.BlockSpec((blk * ROW_TILE, LANES), row_map),
        scratch_shapes=[pltpu.VMEM((d, 2 * D_EXPERT), BF16), pltpu.VMEM((D_EXPERT, d), BF16)],
    )
    return pl.pallas_call(
        _expert_kernel,
        grid_spec=grid_spec,
        out_shape=jax.ShapeDtypeStruct(xs.shape, F32),
        compiler_params=_cparams(("arbitrary",)),
        name="moe_experts",
    )(blk_expert, n_used, xs, w1, w3, w2)


def _combine_kernel(dest_hbm, ys_hbm, x1_ref, route_ref, gt2_ref, l2g_ref, l2b_ref, o_ref,
                    idx_smem, buf_ref, idx_sem, row_sem):
    b_i = pl.program_id(0)
    i = pl.program_id(1)
    td = x1_ref.shape[1]
    tile = b_i * pl.num_programs(1) + i
    n_tiles = pl.num_programs(0) * pl.num_programs(1)
    slot = tile % 2

    def idx_copy(t, sl):
        return pltpu.make_async_copy(dest_hbm.at[t], idx_smem.at[sl], idx_sem.at[sl])

    def row_copy(sl, n, src_row):
        return pltpu.make_async_copy(ys_hbm.at[pl.ds(pl.multiple_of(src_row, ROW_TILE), ROW_TILE)],
                                     buf_ref.at[sl, n % 2, pl.ds(ROW_TILE * (n // 2), ROW_TILE)], row_sem.at[sl])

    def start_rows(sl):
        for n in range(2 * td):
            row_copy(sl, n, idx_smem[sl, n]).start(priority=n % 2)

    @pl.when(tile == 0)
    def _():
        idx_copy(0, 0).start()
        idx_copy(0, 0).wait()
        start_rows(0)

        @pl.when(n_tiles > 1)
        def _():
            idx_copy(1, 1).start()

    @pl.when(tile + 1 < n_tiles)
    def _():
        idx_copy(tile + 1, 1 - slot).wait()
        start_rows(1 - slot)

        @pl.when(tile + 2 < n_tiles)
        def _():
            idx_copy(tile + 2, slot).start()

    for n in range(2 * td):
        row_copy(slot, n, 0).wait()
    route = route_ref[0]
    ffn = (_load_tile_rows(buf_ref, (slot, 0), 0, td) * route[:, 2:3]
           + _load_tile_rows(buf_ref, (slot, 1), 0, td) * route[:, 3:4])
    o_ref[0] = _layer_norm(ALPHA * x1_ref[0] + gt2_ref[0] * ffn) * l2g_ref[...] + l2b_ref[...]


def _combine(dest_tiles, ys, x1, route3, gt2, l2g, l2b, td):
    b, s, d = x1.shape
    row = lambda w: pl.BlockSpec((1, td, w), lambda bi, i: (bi, i, 0))
    const = lambda shape: pl.BlockSpec(shape, lambda bi, i: (0,) * len(shape))
    return pl.pallas_call(
        _combine_kernel,
        grid=(b, s // td),
        in_specs=[pl.BlockSpec(memory_space=pl.ANY),
                  pl.BlockSpec(memory_space=pl.ANY),
                  row(d), row(ROUTE_W),
                  pl.BlockSpec((1, 1, d), lambda bi, i: (bi, 0, 0)),
                  const((1, d)), const((1, d))],
        out_specs=row(d),
        out_shape=jax.ShapeDtypeStruct((b, s, d), F32),
        scratch_shapes=[pltpu.SMEM((2, 2 * td), I32), pltpu.VMEM((2, 2, td * ROW_TILE, LANES), F32),
                        pltpu.SemaphoreType.DMA((2,)), pltpu.SemaphoreType.DMA((2,))],
        compiler_params=_cparams(("arbitrary", "arbitrary")),
        name="moe_combine",
    )(dest_tiles, ys, x1, route3, gt2, l2g, l2b)


def _tile(n, pref):
    t = min(n, pref)
    assert n % t == 0, (n, t)
    return t


def _rope_tables(s):
    half = HEAD_DIM // 2
    inv = ROPE_THETA ** (-jnp.arange(0, half, 2, dtype=F32) / half)
    pos = jnp.arange(s)
    row = (pos // GRID_W).astype(F32)[:, None] * inv[None, :]
    col = (pos % GRID_W).astype(F32)[:, None] * inv[None, :]
    cos = jnp.concatenate([jnp.cos(row), jnp.cos(row), jnp.cos(col), jnp.cos(col)], axis=-1)
    sin = jnp.concatenate([-jnp.sin(row), jnp.sin(row), -jnp.sin(col), jnp.sin(col)], axis=-1)
    return cos, sin


def _layer(x, c, w_ada, b_ada, w_in, conv_w, a_log, dt_bias, gdn_norm_g, q_norm_g, k_norm_g, attn_norm_g,
           w_out, ln1_g, ln1_b, w_group, b_group, w_router, b_router, w1, w3, w2, ln2_g, ln2_b):
    b, s, d = x.shape
    t = b * s
    assert s % CHUNK == 0 and s % GRID_W == 0

    c_pad = jnp.pad(c, ((0, (-b) % 8), (0, 0)))
    mod = _adaln_mod(c_pad, w_ada, b_ada.reshape(1, -1), _tile(6 * d, 1536))[:b]
    mod3 = mod.reshape(b, 6, d)

    gq, gk, gv, gz, gab, aq, ak, av = jnp.split(
        w_in, [GDN_WIDTH, 2 * GDN_WIDTH, 3 * GDN_WIDTH, 4 * GDN_WIDTH, 4 * GDN_WIDTH + N_GATES,
               4 * GDN_WIDTH + N_GATES + ATTN_WIDTH, 4 * GDN_WIDTH + N_GATES + ATTN_WIDTH + ATTN_KV_WIDTH], axis=1)
    w_packed = jnp.concatenate([gq, gk, gv, gz, ak, jnp.pad(gab, ((0, 0), (0, LANES - N_GATES)))], axis=1).astype(BF16)
    wqv_t = jnp.concatenate([aq, av], axis=1).T.astype(BF16)
    wab_t = gab.T.astype(BF16)
    cos, sin = _rope_tables(s)

    tm = _tile(s, 512)
    gqkv, z, ab, abt, a_qt, a_k, a_vt = _in_projection(
        x, mod3, w_packed, wqv_t, wab_t, cos, sin, cos.T, sin.T, q_norm_g.reshape(-1, 1), k_norm_g.reshape(1, -1), tm)

    pad_gates = lambda p: jnp.pad(p.reshape(1, -1), ((0, 0), (0, N_GATES - p.size)))
    alog_c = pad_gates(a_log)
    dtb_c = pad_gates(dt_bias)
    tc = _tile(s, 512)
    gq_n, gk_n, gv_n, gb, egc, egl, ea, gcr = _gdn_prep(
        gqkv, conv_w, ab, abt, alog_c, dtb_c, alog_c.reshape(-1, 1), dtb_c.reshape(-1, 1), tc)
    a_m, b_m, a_dg, q_eff, o_in = _gdn_local(gq_n, gk_n, gv_n, gb, egc, egl, ea, gcr, _tile(s, 256))
    o_f, o_b = _gdn_scan(a_m, b_m, a_dg, q_eff, o_in, _tile(s // CHUNK, 4))

    tq, tk = _tile(s, 1024), _tile(s, 4096)
    att_lagged, excess = _attention_lagged(a_qt, a_k, a_vt, tq, tk)
    att = lax.cond(jnp.max(excess) > ATTN_LAG_LIMIT,
                   lambda: _attention(a_qt, a_k, a_vt, tq, tk), lambda: att_lagged)

    w_rt = jnp.pad(jnp.concatenate([w_group, w_router], axis=1), ((0, 0), (0, LANES - N_GROUPS - N_EXPERTS)))
    b_rt = jnp.pad(jnp.concatenate([b_group, b_router]).reshape(1, -1), ((0, 0), (0, LANES - N_GROUPS - N_EXPERTS)))
    x1, h2, route = _mixer_out(o_f, o_b, z, att, x, mod3, w_out.astype(BF16), gdn_norm_g.reshape(1, -1),
                               attn_norm_g.reshape(-1, 1), ln1_g.reshape(1, -1), ln1_b.reshape(1, -1),
                               w_rt, b_rt, _tile(s, 512))

    blk = 256
    th = _tile(t, 1024)
    route2 = route.reshape(t, ROUTE_W)
    rank, counts = _expert_ranks(route2, th)
    counts_i = counts[0, :N_EXPERTS].astype(I32)
    padded = (counts_i + blk - 1) // blk * blk
    pad_end = jnp.cumsum(padded)
    pad_start = pad_end - padded
    n_blocks = -(-(2 * t) // blk) + N_EXPERTS
    blk_pos = jnp.arange(n_blocks, dtype=I32) * blk
    blk_expert = jnp.minimum(jnp.sum((pad_end[None, :] <= blk_pos[:, None]).astype(I32), axis=1), N_EXPERTS - 1)
    n_used = jnp.maximum(pad_end[-1:] // blk, 1).astype(I32)
    start_row = jnp.pad(pad_start.astype(F32).reshape(1, -1), ((0, 0), (0, LANES - N_EXPERTS)))
    dest = _expert_dest(route2, rank, start_row, _tile(t, 4096))

    td = _tile(s, 256)
    dest_tiles = dest[:, 0:2].reshape(t // td, 2 * td)
    assert d == ROW_TILE * LANES
    xs = _dispatch(pad_end.astype(I32), n_used, dest_tiles, h2, n_blocks, blk, td)
    ys = _experts(blk_expert, n_used, xs, w1, w3, w2, blk)
    gt2 = mod3[:, 5:6, :]
    return _combine(dest_tiles, ys, x1, route, gt2, ln2_g.reshape(1, -1), ln2_b.reshape(1, -1), td)


def kernel(x, c, w_ada, b_ada, w_in, conv_w, a_log, dt_bias, gdn_norm_g, q_norm_g, k_norm_g, attn_norm_g,
           w_out, ln1_g, ln1_b, w_group, b_group, w_router, b_router, w1, w3, w2, ln2_g, ln2_b):
    for layer in range(w_ada.shape[0]):
        x = _layer(x, c, w_ada[layer], b_ada[layer], w_in[layer], conv_w[layer], a_log[layer], dt_bias[layer],
                   gdn_norm_g[layer], q_norm_g[layer], k_norm_g[layer], attn_norm_g[layer], w_out[layer],
                   ln1_g[layer], ln1_b[layer], w_group[layer], b_group[layer], w_router[layer], b_router[layer],
                   w1[layer], w3[layer], w2[layer], ln2_g[layer], ln2_b[layer])
    return x
```

```python
import math

import jax
import jax.numpy as jnp
from jax import lax
from jax.experimental import pallas as pl
from jax.experimental.pallas import tpu as pltpu

F32 = jnp.float32
BF16 = jnp.bfloat16
I32 = jnp.int32

HEAD_DIM = 128
GDN_HEADS = 4
GDN_WIDTH = GDN_HEADS * HEAD_DIM
ATTN_Q_HEADS = 4
ATTN_KV_HEADS = 2
ATTN_GROUP = ATTN_Q_HEADS // ATTN_KV_HEADS
ATTN_WIDTH = ATTN_Q_HEADS * HEAD_DIM
ATTN_KV_WIDTH = ATTN_KV_HEADS * HEAD_DIM
CONV_K = 5
CHUNK = 64
GRID_W = 64
ROPE_THETA = 10000.0
N_GROUPS = 4
EXPERTS_PER_GROUP = 8
N_EXPERTS = N_GROUPS * EXPERTS_PER_GROUP
D_EXPERT = 256
DEPTH = 1
ALPHA = (2.0 * DEPTH) ** 0.25
LN_EPS = 1e-5
RMS_EPS = 1e-6

LANES = 128
VMEM_LIMIT_BYTES = 56 * 1024 * 1024
NEG_BIG = -1e30


def _cparams(semantics):
    return pltpu.CompilerParams(dimension_semantics=semantics, vmem_limit_bytes=VMEM_LIMIT_BYTES)


def _dot(a, b):
    return jnp.dot(a, b, preferred_element_type=F32)


def _dot_nt(a, b):
    return lax.dot_general(a, b, (((1,), (1,)), ((), ())), preferred_element_type=F32)


def _dot_tn(a, b):
    return lax.dot_general(a, b, (((0,), (0,)), ((), ())), preferred_element_type=F32)


def _split3(a):
    hi = a.astype(BF16)
    r = a - hi.astype(F32)
    mid = r.astype(BF16)
    lo = (r - mid.astype(F32)).astype(BF16)
    return hi, mid, lo


def _dot_f32_lhs_exact(a_bf16_exact, b):
    hi, mid, lo = _split3(b)
    return _dot(a_bf16_exact, hi) + _dot(a_bf16_exact, mid) + _dot(a_bf16_exact, lo)


def _dot_f32_rhs_exact(a, b_bf16_exact):
    hi, mid, lo = _split3(a)
    return _dot(hi, b_bf16_exact) + _dot(mid, b_bf16_exact) + _dot(lo, b_bf16_exact)


def _dot_f32(a, b):
    ah, am, _ = _split3(a)
    bh, bm, _ = _split3(b)
    return _dot(ah, bh) + (_dot(ah, bm) + _dot(am, bh))


ROW_TILE = 8


def _store_tile_rows(ref, lead, row0, value):
    n = value.shape[0]
    for c in range(ROW_TILE):
        ref[lead + (pl.ds(ROW_TILE * row0 + c, n, stride=ROW_TILE), slice(None))] = value[:, c * LANES:(c + 1) * LANES]


def _load_tile_rows(ref, lead, row0, n):
    return jnp.concatenate(
        [ref[lead + (pl.ds(ROW_TILE * row0 + c, n, stride=ROW_TILE), slice(None))] for c in range(ROW_TILE)], axis=1)


def _sigmoid(x):
    return 1.0 / (1.0 + jnp.exp(-x))


def _silu(x):
    return x * _sigmoid(x)


def _softplus(x):
    return jnp.maximum(x, 0.0) + jnp.log1p(jnp.exp(-jnp.abs(x)))


def _layer_norm(x):
    mu = jnp.mean(x, axis=-1, keepdims=True)
    xc = x - mu
    var = jnp.mean(xc * xc, axis=-1, keepdims=True)
    return xc * lax.rsqrt(var + LN_EPS)


def _mod_kernel(c_ref, w_ref, b_ref, o_ref):
    o_ref[...] = _dot_f32(_silu(c_ref[...]), w_ref[...]) + b_ref[...]


def _adaln_mod(c_pad, w_ada, b_ada, tn):
    rows, d = c_pad.shape
    n = w_ada.shape[1]
    return pl.pallas_call(
        _mod_kernel,
        grid=(n // tn,),
        in_specs=[pl.BlockSpec((rows, d), lambda j: (0, 0)),
                  pl.BlockSpec((d, tn), lambda j: (0, j)),
                  pl.BlockSpec((1, tn), lambda j: (0, j))],
        out_specs=pl.BlockSpec((rows, tn), lambda j: (0, j)),
        out_shape=jax.ShapeDtypeStruct((rows, n), F32),
        compiler_params=_cparams(("arbitrary",)),
        name="adaln_mod",
    )(c_pad, w_ada, b_ada)


_W_GQKV = 0
_W_Z = 3 * GDN_WIDTH
_W_AK = _W_Z + GDN_WIDTH
_W_AB = _W_AK + ATTN_KV_WIDTH
_W_COLS = _W_AB + LANES
N_GATES = 4 * GDN_HEADS
LOG2E = math.log2(math.e)


def _rope(xh, cos, sin_signed, lane):
    fwd = pltpu.roll(xh, 32, 1)
    bwd = pltpu.roll(xh, LANES - 32, 1)
    partner = jnp.where((lane % 64) < 32, bwd, fwd)
    return xh * cos + partner * sin_signed


def _rope_t(xt, cos_t, sin_signed_t):
    q = HEAD_DIM // 4
    partner = jnp.concatenate([xt[q:2 * q], xt[0:q], xt[3 * q:4 * q], xt[2 * q:3 * q]], axis=0)
    return xt * cos_t + partner * sin_signed_t


def _inproj_kernel(x_ref, mod_ref, w_ref, wqvt_ref, wabt_ref, cos_ref, sin_ref, cost_ref, sint_ref, qg_ref, kg_ref,
                   gqkv_ref, z_ref, ab_ref, abt_ref, aqt_ref, ak_ref, avt_ref):
    sh1 = mod_ref[0, 0:1, :]
    sc1 = mod_ref[0, 1:2, :]
    q_scale = (HEAD_DIM ** -0.5) * LOG2E
    h = _layer_norm(x_ref[0]) * (1.0 + sc1) + sh1
    hb = h.astype(BF16)
    gqkv_ref[0] = _dot(hb, w_ref[:, _W_GQKV:_W_Z]).astype(BF16)
    z_ref[0] = _dot(hb, w_ref[:, _W_Z:_W_AK]).astype(BF16)
    ab_ref[0] = _dot(hb, w_ref[:, _W_AB:_W_COLS])[:, 0:N_GATES]
    abt_ref[0] = _dot_nt(wabt_ref[...], hb)
    ak = _dot(hb, w_ref[:, _W_AK:_W_AB])
    cos = cos_ref[...]
    sin = sin_ref[...]
    lane = lax.broadcasted_iota(I32, cos.shape, 1)
    for j in range(ATTN_KV_HEADS):
        xh = ak[:, j * HEAD_DIM:(j + 1) * HEAD_DIM]
        xn = xh * lax.rsqrt(jnp.mean(xh * xh, axis=-1, keepdims=True) + RMS_EPS) * kg_ref[...]
        ak_ref[0, :, j * HEAD_DIM:(j + 1) * HEAD_DIM] = _rope(xn, cos, sin, lane).astype(BF16)
    qvt = _dot_nt(wqvt_ref[...], hb)
    cos_t = cost_ref[...]
    sin_t = sint_ref[...]
    for i in range(ATTN_Q_HEADS):
        xt = qvt[i * HEAD_DIM:(i + 1) * HEAD_DIM, :]
        xn = xt * lax.rsqrt(jnp.mean(xt * xt, axis=0, keepdims=True) + RMS_EPS) * qg_ref[...]
        aqt_ref[0, i * HEAD_DIM:(i + 1) * HEAD_DIM, :] = (_rope_t(xn, cos_t, sin_t) * q_scale).astype(BF16)
    avt_ref[0] = qvt[ATTN_WIDTH:, :].astype(BF16)


def _in_projection(x, mod3, w_packed, wqv_t, wab_t, cos, sin, cos_t, sin_t, qg_col, kg_row, tm):
    b, s, d = x.shape
    grid = (b, s // tm)
    row = lambda w: pl.BlockSpec((1, tm, w), lambda bi, i: (bi, i, 0))
    col = lambda h: pl.BlockSpec((1, h, tm), lambda bi, i: (bi, 0, i))
    const = lambda shape: pl.BlockSpec(shape, lambda bi, i: (0,) * len(shape))
    out_shapes = [
        jax.ShapeDtypeStruct((b, s, 3 * GDN_WIDTH), BF16),
        jax.ShapeDtypeStruct((b, s, GDN_WIDTH), BF16),
        jax.ShapeDtypeStruct((b, s, N_GATES), F32),
        jax.ShapeDtypeStruct((b, N_GATES, s), F32),
        jax.ShapeDtypeStruct((b, ATTN_WIDTH, s), BF16),
        jax.ShapeDtypeStruct((b, s, ATTN_KV_WIDTH), BF16),
        jax.ShapeDtypeStruct((b, ATTN_KV_WIDTH, s), BF16),
    ]
    out_specs = [row(3 * GDN_WIDTH), row(GDN_WIDTH), row(N_GATES), col(N_GATES),
                 col(ATTN_WIDTH), row(ATTN_KV_WIDTH), col(ATTN_KV_WIDTH)]
    return pl.pallas_call(
        _inproj_kernel,
        grid=grid,
        in_specs=[row(d),
                  pl.BlockSpec((1, 6, d), lambda bi, i: (bi, 0, 0)),
                  const((d, _W_COLS)),
                  const((ATTN_WIDTH + ATTN_KV_WIDTH, d)),
                  const((N_GATES, d)),
                  pl.BlockSpec((tm, HEAD_DIM), lambda bi, i: (i, 0)),
                  pl.BlockSpec((tm, HEAD_DIM), lambda bi, i: (i, 0)),
                  pl.BlockSpec((HEAD_DIM, tm), lambda bi, i: (0, i)),
                  pl.BlockSpec((HEAD_DIM, tm), lambda bi, i: (0, i)),
                  const((HEAD_DIM, 1)),
                  const((1, HEAD_DIM))],
        out_specs=out_specs,
        out_shape=out_shapes,
        compiler_params=_cparams(("arbitrary", "arbitrary")),
        name="in_projection",
    )(x, mod3, w_packed, wqv_t, wab_t, cos, sin, cos_t, sin_t, qg_col, kg_row)


_HALO = 16


def _chunk_masks(n):
    r = lax.broadcasted_iota(I32, (n, n), 0)
    c = lax.broadcasted_iota(I32, (n, n), 1)
    same = (r // CHUNK) == (c // CHUNK)
    lower = same & (r >= c)
    upper = same & (r <= c)
    return lower, upper


def _gdn_prep_kernel(prev_ref, main_ref, next_ref, convw_ref, ab_ref, abt_ref,
                     alog_c_ref, dtb_c_ref, alog_r_ref, dtb_r_ref,
                     q_ref, k_ref, v_ref, gb_ref, egc_ref, egl_ref, ea_ref, gcr_ref, buf_ref):
    i = pl.program_id(1)
    n_i = pl.num_programs(1)
    tc = main_ref.shape[1]
    prev_scale = jnp.where(i > 0, 1.0, 0.0).astype(F32)
    next_scale = jnp.where(i < n_i - 1, 1.0, 0.0).astype(F32)
    buf_ref[0:_HALO, :] = prev_ref[0].astype(F32) * prev_scale
    buf_ref[_HALO:_HALO + tc, :] = main_ref[0].astype(F32)
    buf_ref[_HALO + tc:, :] = next_ref[0].astype(F32) * next_scale
    pad = CONV_K // 2
    for part, out_ref in enumerate((q_ref, k_ref, v_ref)):
        for hh in range(GDN_HEADS):
            c0 = part * GDN_WIDTH + hh * HEAD_DIM
            acc = None
            for j in range(CONV_K):
                term = buf_ref[_HALO - pad + j:_HALO - pad + j + tc, c0:c0 + HEAD_DIM] * convw_ref[j:j + 1, c0:c0 + HEAD_DIM]
                acc = term if acc is None else acc + term
            y = _silu(acc)
            if part < 2:
                y = y * lax.rsqrt(jnp.sum(y * y, axis=-1, keepdims=True) + RMS_EPS)
            if part == 0:
                y = y * (HEAD_DIM ** -0.5)
            out_ref[0, :, hh * HEAD_DIM:(hh + 1) * HEAD_DIM] = y.astype(BF16)

    lower, upper = _chunk_masks(tc)
    lower_b = jnp.where(lower, 1.0, 0.0).astype(BF16)
    upper_b = jnp.where(upper, 1.0, 0.0).astype(BF16)
    n_dir = 2 * GDN_HEADS
    ab = ab_ref[0]
    lane = lax.broadcasted_iota(I32, ab.shape, 1)
    g = -jnp.exp(alog_c_ref[...]) * _softplus(ab + dtb_c_ref[...])
    pre = _dot_f32_lhs_exact(lower_b, g)
    suf = _dot_f32_lhs_exact(upper_b, g)
    gtot = pre + suf - g
    gc = jnp.where(lane < GDN_HEADS, pre, suf)
    gb_ref[0] = jnp.where(lane < n_dir, gc, _sigmoid(ab))
    egc_ref[0] = jnp.exp(gc)
    egl_ref[0] = jnp.exp(gtot - gc)
    ea_ref[0] = jnp.exp(gtot)
    abt = abt_ref[0]
    row = lax.broadcasted_iota(I32, abt.shape, 0)
    g_r = -jnp.exp(alog_r_ref[...]) * _softplus(abt + dtb_r_ref[...])
    pre_r = _dot_f32_rhs_exact(g_r, upper_b)
    suf_r = _dot_f32_rhs_exact(g_r, lower_b)
    gcr_ref[0] = jnp.where(row < GDN_HEADS, pre_r, suf_r)[0:n_dir]


def _gdn_prep(gqkv, conv_w, ab, abt, alog_c, dtb_c, alog_r, dtb_r, tc):
    b, s, w = gqkv.shape
    nh = tc // _HALO
    n_dir = 2 * GDN_HEADS
    grid = (b, s // tc)
    row = lambda width: pl.BlockSpec((1, tc, width), lambda bi, i: (bi, i, 0))
    const = lambda shape: pl.BlockSpec(shape, lambda bi, i: (0,) * len(shape))
    last_halo = s // _HALO - 1
    out_shapes = ([jax.ShapeDtypeStruct((b, s, GDN_WIDTH), BF16)] * 3
                  + [jax.ShapeDtypeStruct((b, s, N_GATES), F32)] * 4
                  + [jax.ShapeDtypeStruct((b, n_dir, s), F32)])
    out_specs = ([row(GDN_WIDTH)] * 3 + [row(N_GATES)] * 4
                 + [pl.BlockSpec((1, n_dir, tc), lambda bi, i: (bi, 0, i))])
    return pl.pallas_call(
        _gdn_prep_kernel,
        grid=grid,
        in_specs=[pl.BlockSpec((1, _HALO, w), lambda bi, i: (bi, jnp.maximum(i * nh - 1, 0), 0)),
                  row(w),
                  pl.BlockSpec((1, _HALO, w), lambda bi, i: (bi, jnp.minimum((i + 1) * nh, last_halo), 0)),
                  const((CONV_K, w)),
                  row(N_GATES),
                  pl.BlockSpec((1, N_GATES, tc), lambda bi, i: (bi, 0, i)),
                  const((1, N_GATES)), const((1, N_GATES)),
                  const((N_GATES, 1)), const((N_GATES, 1))],
        out_specs=out_specs,
        out_shape=out_shapes,
        scratch_shapes=[pltpu.VMEM((tc + 2 * _HALO, w), F32)],
        compiler_params=_cparams(("arbitrary", "arbitrary")),
        name="gdn_prep",
    )(gqkv, gqkv, gqkv, conv_w, ab, abt, alog_c, dtb_c, alog_r, dtb_r)


def _gdn_local_kernel(q_ref, k_ref, v_ref, gb_ref, egc_ref, egl_ref, ea_ref, gcr_ref,
                      a_ref, b_ref, dg_ref, qe_ref, oi_ref):
    tc = q_ref.shape[1]
    nch = tc // CHUNK
    lower, upper = _chunk_masks(tc)
    r = lax.broadcasted_iota(I32, (tc, tc), 0)
    c = lax.broadcasted_iota(I32, (tc, tc), 1)
    eye = r == c
    blk_xor = r ^ c
    n_dir = 2 * GDN_HEADS
    combos = [(d, hh) for d in range(2) for hh in range(GDN_HEADS)]
    head = lambda hh: slice(hh * HEAD_DIM, (hh + 1) * HEAD_DIM)
    kk = [_dot_nt(k_ref[0, :, head(hh)], k_ref[0, :, head(hh)]) for hh in range(GDN_HEADS)]
    qk = [_dot_nt(q_ref[0, :, head(hh)], k_ref[0, :, head(hh)]) for hh in range(GDN_HEADS)]
    side_r = lax.broadcasted_iota(I32, (CHUNK, tc), 0)
    side_c = lax.broadcasted_iota(I32, (CHUNK, tc), 1)
    side_eye = (side_c % CHUNK) == side_r
    same_chunk = blk_xor < CHUNK

    def side_by_side(bd):
        out = None
        for ch in range(nch):
            part = jnp.where((side_c // CHUNK) == ch, bd[ch * CHUNK:(ch + 1) * CHUNK, :], 0.0)
            out = part if out is None else out + part
        return out

    def block_diag(ss):
        return jnp.where(same_chunk, jnp.concatenate([ss] * nch, axis=0), 0.0)

    decay, m, t = {}, {}, {}
    for cb in combos:
        d, hh = cb
        idx = d * GDN_HEADS + hh
        incl = lower if d == 0 else upper
        gc = gb_ref[0, :, idx:idx + 1]
        beta = gb_ref[0, :, n_dir + idx:n_dir + idx + 1]
        gcr = gcr_ref[0, idx:idx + 1, :]
        decay[cb] = jnp.where(incl, jnp.exp(jnp.minimum(gc - gcr, 0.0)), 0.0)
        m[cb] = jnp.where(eye, 0.0, beta * kk[hh] * decay[cb])
        t[cb] = jnp.where(side_eye, 1.0, 0.0) - side_by_side(jnp.where(blk_xor < 2, m[cb], 0.0))
    sz = 2
    while sz < CHUNK:
        join = (blk_xor >= sz) & (blk_xor < 2 * sz)
        tc_s = {cb: _dot(t[cb].astype(BF16), jnp.where(join, m[cb], 0.0).astype(BF16)).astype(BF16) for cb in combos}
        for cb in combos:
            t[cb] = t[cb] - _dot(tc_s[cb], block_diag(t[cb]).astype(BF16))
        sz *= 2
    t = {cb: block_diag(t[cb]) for cb in combos}
    solb = {}
    for cb in combos:
        d, hh = cb
        idx = d * GDN_HEADS + hh
        beta = gb_ref[0, :, n_dir + idx:n_dir + idx + 1]
        egc = egc_ref[0, :, idx:idx + 1]
        rhs = jnp.concatenate([v_ref[0, :, head(hh)].astype(F32) * beta,
                               k_ref[0, :, head(hh)].astype(F32) * (beta * egc)], axis=1).astype(BF16)
        solb[cb] = _dot(t[cb].astype(BF16), rhs).astype(BF16)
    for cb in combos:
        d, hh = cb
        idx = d * GDN_HEADS + hh
        egc = egc_ref[0, :, idx:idx + 1]
        qo = _dot((qk[hh] * decay[cb]).astype(BF16), solb[cb])
        oi_ref[0, d, :, head(hh)] = qo[:, :HEAD_DIM].astype(BF16)
        qe_ref[0, d, :, head(hh)] = (q_ref[0, :, head(hh)].astype(F32) * egc - qo[:, HEAD_DIM:]).astype(BF16)
    for cb in combos:
        d, hh = cb
        idx = d * GDN_HEADS + hh
        egl = egl_ref[0, :, idx:idx + 1]
        ea = ea_ref[0, :, idx:idx + 1]
        kg = (k_ref[0, :, head(hh)].astype(F32) * egl).astype(BF16)
        for ci in range(nch):
            rs = slice(ci * CHUNK, (ci + 1) * CHUNK)
            ab = _dot_tn(kg[rs], solb[cb][rs])
            a_ref[0, d, ci, :, head(hh)] = (-ab[:, HEAD_DIM:]).astype(BF16)
            b_ref[0, d, ci, :, head(hh)] = ab[:, :HEAD_DIM].astype(BF16)
            dg_ref[0, d, ci, :, head(hh)] = jnp.broadcast_to(ea[ci * CHUNK:ci * CHUNK + 1, :], (8, HEAD_DIM))


def _gdn_local(q, k, v, gb, egc, egl, ea, gcr, tc):
    b, s, w = q.shape
    n = s // CHUNK
    nch = tc // CHUNK
    n_dir = 2 * GDN_HEADS
    grid = (b, s // tc)
    row = lambda width: pl.BlockSpec((1, tc, width), lambda bi, i: (bi, i, 0))
    return pl.pallas_call(
        _gdn_local_kernel,
        grid=grid,
        in_specs=[row(w), row(w), row(w), row(N_GATES), row(N_GATES), row(N_GATES), row(N_GATES),
                  pl.BlockSpec((1, n_dir, tc), lambda bi, i: (bi, 0, i))],
        out_specs=[pl.BlockSpec((1, 2, nch, HEAD_DIM, w), lambda bi, i: (bi, 0, i, 0, 0)),
                   pl.BlockSpec((1, 2, nch, HEAD_DIM, w), lambda bi, i: (bi, 0, i, 0, 0)),
                   pl.BlockSpec((1, 2, nch, 8, w), lambda bi, i: (bi, 0, i, 0, 0)),
                   pl.BlockSpec((1, 2, tc, w), lambda bi, i: (bi, 0, i, 0)),
                   pl.BlockSpec((1, 2, tc, w), lambda bi, i: (bi, 0, i, 0))],
        out_shape=[jax.ShapeDtypeStruct((b, 2, n, HEAD_DIM, w), BF16),
                   jax.ShapeDtypeStruct((b, 2, n, HEAD_DIM, w), BF16),
                   jax.ShapeDtypeStruct((b, 2, n, 8, w), F32),
                   jax.ShapeDtypeStruct((b, 2, s, w), BF16),
                   jax.ShapeDtypeStruct((b, 2, s, w), BF16)],
        compiler_params=_cparams(("arbitrary", "arbitrary")),
        name="gdn_local",
    )(q, k, v, gb, egc, egl, ea, gcr)


def _gdn_scan_kernel(af_ref, bf_ref, df_ref, qf_ref, of_ref, ab_ref, bb_ref, db_ref, qb_ref, ob_ref,
                     outf_ref, outb_ref, s_ref):
    @pl.when(pl.program_id(1) == 0)
    def _():
        s_ref[...] = jnp.zeros_like(s_ref)

    cs = af_ref.shape[2]
    dirs = ((af_ref, bf_ref, df_ref, qf_ref, of_ref, outf_ref), (ab_ref, bb_ref, db_ref, qb_ref, ob_ref, outb_ref))
    for j in range(cs):
        for d, (a_ref, b_ref, dg_ref, q_ref, o_ref, out_ref) in enumerate(dirs):
            ci = j if d == 0 else cs - 1 - j
            rs = slice(ci * CHUNK, (ci + 1) * CHUNK)
            for hh in range(GDN_HEADS):
                hs = slice(hh * HEAD_DIM, (hh + 1) * HEAD_DIM)
                si = d * GDN_HEADS + hh
                st = s_ref[si]
                stb = st.astype(BF16)
                out_ref[0, rs, hs] = (_dot(q_ref[0, 0, rs, hs], stb) + o_ref[0, 0, rs, hs].astype(F32)).astype(BF16)
                s_ref[si] = (dg_ref[0, 0, ci, 0:1, hs] * st + _dot(a_ref[0, 0, ci, :, hs], stb)
                             + b_ref[0, 0, ci, :, hs].astype(F32))


def _gdn_scan(a, bm, dg, qe, oi, cs):
    b, _, n, _, w = a.shape
    s = n * CHUNK
    ns = n // cs
    tr = cs * CHUNK
    fwd5 = lambda bi, i: (bi, 0, i, 0, 0)
    bwd5 = lambda bi, i: (bi, 1, ns - 1 - i, 0, 0)
    fwd4 = lambda bi, i: (bi, 0, i, 0)
    bwd4 = lambda bi, i: (bi, 1, ns - 1 - i, 0)
    blk5 = (1, 1, cs, HEAD_DIM, w)
    blkd = (1, 1, cs, 8, w)
    blk4 = (1, 1, tr, w)
    return pl.pallas_call(
        _gdn_scan_kernel,
        grid=(b, ns),
        in_specs=[pl.BlockSpec(blk5, fwd5), pl.BlockSpec(blk5, fwd5), pl.BlockSpec(blkd, fwd5),
                  pl.BlockSpec(blk4, fwd4), pl.BlockSpec(blk4, fwd4),
                  pl.BlockSpec(blk5, bwd5), pl.BlockSpec(blk5, bwd5), pl.BlockSpec(blkd, bwd5),
                  pl.BlockSpec(blk4, bwd4), pl.BlockSpec(blk4, bwd4)],
        out_specs=[pl.BlockSpec((1, tr, w), lambda bi, i: (bi, i, 0)),
                   pl.BlockSpec((1, tr, w), lambda bi, i: (bi, ns - 1 - i, 0))],
        out_shape=[jax.ShapeDtypeStruct((b, s, w), BF16)] * 2,
        scratch_shapes=[pltpu.VMEM((2 * GDN_HEADS, HEAD_DIM, HEAD_DIM), F32)],
        compiler_params=_cparams(("arbitrary", "arbitrary")),
        name="gdn_scan",
    )(a, bm, dg, qe, oi, a, bm, dg, qe, oi)


ATTN_SUB = 512
ATTN_LAG_LIMIT = 64.0


def _attn_lagged_kernel(qt_ref, k_ref, vt_ref, ot_ref, ex_ref, q2t_ref, m_ref, l_ref, acc_ref, exc_ref):
    j = pl.program_id(3)
    tq = qt_ref.shape[2]
    tk = k_ref.shape[1]

    @pl.when(j == 0)
    def _():
        for gi in range(ATTN_GROUP):
            q2t_ref[:, gi * tq:(gi + 1) * tq] = qt_ref[0, gi * HEAD_DIM:(gi + 1) * HEAD_DIM, :]
        m_ref[...] = jnp.max(_dot(k_ref[0, 0:ATTN_SUB, :], q2t_ref[...]), axis=0, keepdims=True)
        l_ref[...] = jnp.zeros_like(l_ref)
        acc_ref[...] = jnp.zeros_like(acc_ref)
        exc_ref[...] = jnp.zeros_like(exc_ref)

    q2t = q2t_ref[...]
    m_est = m_ref[...]
    m_run = m_est
    l_run = l_ref[...]
    n_sub = tk // ATTN_SUB
    scores = lambda jj: _dot(k_ref[0, jj * ATTN_SUB:(jj + 1) * ATTN_SUB, :], q2t)
    st_next = scores(0)
    for jj in range(n_sub):
        ks = slice(jj * ATTN_SUB, (jj + 1) * ATTN_SUB)
        st = st_next
        if jj + 1 < n_sub:
            st_next = scores(jj + 1)
        p = jnp.exp2(st - m_est)
        m_run = jnp.maximum(m_run, jnp.max(st, axis=0, keepdims=True))
        l_run = l_run + jnp.sum(p, axis=0, keepdims=True)
        acc_ref[...] += _dot(vt_ref[0, :, ks], p.astype(BF16))
    rebase = jnp.exp2(m_est - m_run)
    acc_ref[...] = acc_ref[...] * rebase
    l_ref[...] = l_run * rebase
    m_ref[...] = m_run
    exc_ref[...] = jnp.maximum(exc_ref[...], m_run - m_est)

    @pl.when(j == pl.num_programs(3) - 1)
    def _():
        out = acc_ref[...] / l_ref[...]
        for gi in range(ATTN_GROUP):
            ot_ref[0, gi * HEAD_DIM:(gi + 1) * HEAD_DIM, :] = out[:, gi * tq:(gi + 1) * tq].astype(BF16)
        ex_ref[0] = exc_ref[...]


def _attention_lagged(q_t, k, v_t, tq, tk):
    b, _, s = q_t.shape
    gw = ATTN_GROUP * HEAD_DIM
    nq = s // tq
    assert tk % ATTN_SUB == 0
    return pl.pallas_call(
        _attn_lagged_kernel,
        grid=(b, ATTN_KV_HEADS, nq, s // tk),
        in_specs=[pl.BlockSpec((1, gw, tq), lambda bi, g, i, j: (bi, g, i)),
                  pl.BlockSpec((1, tk, HEAD_DIM), lambda bi, g, i, j: (bi, j, g)),
                  pl.BlockSpec((1, HEAD_DIM, tk), lambda bi, g, i, j: (bi, g, j))],
        out_specs=[pl.BlockSpec((1, gw, tq), lambda bi, g, i, j: (bi, g, i)),
                   pl.BlockSpec((1, 1, ATTN_GROUP * tq), lambda bi, g, i, j: ((bi * ATTN_KV_HEADS + g) * nq + i, 0, 0))],
        out_shape=[jax.ShapeDtypeStruct((b, ATTN_WIDTH, s), BF16),
                   jax.ShapeDtypeStruct((b * ATTN_KV_HEADS * nq, 1, ATTN_GROUP * tq), F32)],
        scratch_shapes=[pltpu.VMEM((HEAD_DIM, ATTN_GROUP * tq), BF16),
                        pltpu.VMEM((1, ATTN_GROUP * tq), F32),
                        pltpu.VMEM((1, ATTN_GROUP * tq), F32),
                        pltpu.VMEM((HEAD_DIM, ATTN_GROUP * tq), F32),
                        pltpu.VMEM((1, ATTN_GROUP * tq), F32)],
        compiler_params=_cparams(("arbitrary", "arbitrary", "arbitrary", "arbitrary")),
        name="gqa_attention_lagged",
    )(q_t, k, v_t)


def _attn_kernel(qt_ref, k_ref, vt_ref, ot_ref, q2t_ref, m_ref, l_ref, acc_ref):
    j = pl.program_id(3)
    tq = qt_ref.shape[2]
    tk = k_ref.shape[1]

    @pl.when(j == 0)
    def _():
        for gi in range(ATTN_GROUP):
            q2t_ref[:, gi * tq:(gi + 1) * tq] = qt_ref[0, gi * HEAD_DIM:(gi + 1) * HEAD_DIM, :]
        m_ref[...] = jnp.full_like(m_ref, NEG_BIG)
        l_ref[...] = jnp.zeros_like(l_ref)
        acc_ref[...] = jnp.zeros_like(acc_ref)

    q2t = q2t_ref[...]
    m_prev = m_ref[...]
    l_prev = l_ref[...]
    n_sub = tk // ATTN_SUB
    scores = lambda jj: _dot(k_ref[0, jj * ATTN_SUB:(jj + 1) * ATTN_SUB, :], q2t)
    st_next = scores(0)
    for jj in range(n_sub):
        ks = slice(jj * ATTN_SUB, (jj + 1) * ATTN_SUB)
        st = st_next
        if jj + 1 < n_sub:
            st_next = scores(jj + 1)
        m_new = jnp.maximum(m_prev, jnp.max(st, axis=0, keepdims=True))
        alpha = jnp.exp2(m_prev - m_new)
        p = jnp.exp2(st - m_new)
        l_prev = alpha * l_prev + jnp.sum(p, axis=0, keepdims=True)
        acc_ref[...] = alpha * acc_ref[...] + _dot(vt_ref[0, :, ks], p.astype(BF16))
        m_prev = m_new
    m_ref[...] = m_prev
    l_ref[...] = l_prev

    @pl.when(j == pl.num_programs(3) - 1)
    def _():
        out = acc_ref[...] / l_ref[...]
        for gi in range(ATTN_GROUP):
            ot_ref[0, gi * HEAD_DIM:(gi + 1) * HEAD_DIM, :] = out[:, gi * tq:(gi + 1) * tq].astype(BF16)


def _attention(q_t, k, v_t, tq, tk):
    b, _, s = q_t.shape
    gw = ATTN_GROUP * HEAD_DIM
    assert tk % ATTN_SUB == 0
    return pl.pallas_call(
        _attn_kernel,
        grid=(b, ATTN_KV_HEADS, s // tq, s // tk),
        in_specs=[pl.BlockSpec((1, gw, tq), lambda bi, g, i, j: (bi, g, i)),
                  pl.BlockSpec((1, tk, HEAD_DIM), lambda bi, g, i, j: (bi, j, g)),
                  pl.BlockSpec((1, HEAD_DIM, tk), lambda bi, g, i, j: (bi, g, j))],
        out_specs=pl.BlockSpec((1, gw, tq), lambda bi, g, i, j: (bi, g, i)),
        out_shape=jax.ShapeDtypeStruct((b, ATTN_WIDTH, s), BF16),
        scratch_shapes=[pltpu.VMEM((HEAD_DIM, ATTN_GROUP * tq), BF16),
                        pltpu.VMEM((1, ATTN_GROUP * tq), F32),
                        pltpu.VMEM((1, ATTN_GROUP * tq), F32),
                        pltpu.VMEM((HEAD_DIM, ATTN_GROUP * tq), F32)],
        compiler_params=_cparams(("arbitrary", "arbitrary", "arbitrary", "arbitrary")),
        name="gqa_attention",
    )(q_t, k, v_t)


ROUTE_W = 8


def _first_argmax(vals, lane_f, valid):
    vmax = jnp.max(jnp.where(valid, vals, NEG_BIG), axis=-1, keepdims=True)
    idx = jnp.min(jnp.where(valid & (vals == vmax), lane_f, float(LANES)), axis=-1, keepdims=True)
    return vmax, idx


def _mix_kernel(of_ref, ob_ref, z_ref, att_ref, x_ref, mod_ref, wout_ref, gng_ref, ang_ref,
                l1g_ref, l1b_ref, wrt_ref, brt_ref, x1_ref, h2_ref, route_ref):
    o = of_ref[0].astype(F32) + ob_ref[0].astype(F32)
    z = z_ref[0].astype(F32)
    parts = []
    for hh in range(GDN_HEADS):
        hs = slice(hh * HEAD_DIM, (hh + 1) * HEAD_DIM)
        oh = o[:, hs]
        on = oh * lax.rsqrt(jnp.mean(oh * oh, axis=-1, keepdims=True) + RMS_EPS) * gng_ref[...]
        parts.append((on * _silu(z[:, hs])).astype(BF16))
    att_t = att_ref[0].astype(F32)
    attn_t = (att_t * lax.rsqrt(jnp.mean(att_t * att_t, axis=0, keepdims=True) + RMS_EPS) * ang_ref[...]).astype(BF16)
    mixed = _dot(jnp.concatenate(parts, axis=1), wout_ref[:GDN_WIDTH, :]) + _dot_tn(attn_t, wout_ref[GDN_WIDTH:, :])
    gt1 = mod_ref[0, 2:3, :]
    sh2 = mod_ref[0, 3:4, :]
    sc2 = mod_ref[0, 4:5, :]
    x1 = _layer_norm(ALPHA * x_ref[0] + gt1 * mixed) * l1g_ref[...] + l1b_ref[...]
    x1_ref[0] = x1
    h2 = _layer_norm(x1) * (1.0 + sc2) + sh2
    _store_tile_rows(h2_ref, (), 0, h2)
    logits = _dot_f32(h2, wrt_ref[...]) + brt_ref[...]
    lane = lax.broadcasted_iota(I32, logits.shape, 1)
    lane_f = lane.astype(F32)
    is_grp = lane < N_GROUPS
    gmax, gidx = _first_argmax(logits, lane_f, is_grp)
    gsum = jnp.sum(jnp.where(is_grp, jnp.exp(jnp.minimum(logits - gmax, 0.0)), 0.0), axis=-1, keepdims=True)
    grp_p = 1.0 / gsum
    lo = float(N_GROUPS) + float(EXPERTS_PER_GROUP) * gidx
    in_grp = (lane_f >= lo) & (lane_f < lo + float(EXPERTS_PER_GROUP))
    v0, i0 = _first_argmax(logits, lane_f, in_grp)
    v1, i1 = _first_argmax(logits, lane_f, in_grp & (lane_f != i0))
    e1 = jnp.exp(v1 - v0)
    w0 = grp_p / (1.0 + e1)
    w1 = grp_p * e1 / (1.0 + e1)
    route = jnp.where(lane == 0, i0 - float(N_GROUPS),
                      jnp.where(lane == 1, i1 - float(N_GROUPS),
                                jnp.where(lane == 2, w0, jnp.where(lane == 3, w1, 0.0))))
    route_ref[0] = route[:, 0:ROUTE_W]


def _mixer_out(o_f, o_b, z, att, x, mod3, w_out, gng, ang, l1g, l1b, w_rt, b_rt, tm):
    b, s, d = x.shape
    row = lambda w: pl.BlockSpec((1, tm, w), lambda bi, i: (bi, i, 0))
    const = lambda shape: pl.BlockSpec(shape, lambda bi, i: (0,) * len(shape))
    return pl.pallas_call(
        _mix_kernel,
        grid=(b, s // tm),
        in_specs=[row(GDN_WIDTH), row(GDN_WIDTH), row(GDN_WIDTH),
                  pl.BlockSpec((1, ATTN_WIDTH, tm), lambda bi, i: (bi, 0, i)), row(d),
                  pl.BlockSpec((1, 6, d), lambda bi, i: (bi, 0, 0)),
                  const(w_out.shape), const((1, HEAD_DIM)), const((ATTN_WIDTH, 1)),
                  const((1, d)), const((1, d)), const((d, LANES)), const((1, LANES))],
        out_specs=[row(d), pl.BlockSpec((tm * ROW_TILE, LANES), lambda bi, i: (bi * (s // tm) + i, 0)), row(ROUTE_W)],
        out_shape=[jax.ShapeDtypeStruct((b, s, d), F32),
                   jax.ShapeDtypeStruct((b * s * ROW_TILE, LANES), F32),
                   jax.ShapeDtypeStruct((b, s, ROUTE_W), F32)],
        compiler_params=_cparams(("arbitrary", "arbitrary")),
        name="mixer_out",
    )(o_f, o_b, z, att, x, mod3, w_out, gng, ang, l1g, l1b, w_rt, b_rt)


def _rank_kernel(route_ref, rank_ref, cnt_ref, carry_ref):
    @pl.when(pl.program_id(0) == 0)
    def _():
        carry_ref[...] = jnp.zeros_like(carry_ref)

    th = route_ref.shape[0]
    route = route_ref[...]
    lane_f = lax.broadcasted_iota(I32, (th, LANES), 1).astype(F32)
    oh0 = lane_f == route[:, 0:1]
    oh1 = lane_f == route[:, 1:2]
    both = jnp.where(oh0 | oh1, 1.0, 0.0).astype(BF16)
    r = lax.broadcasted_iota(I32, (th, th), 0)
    c = lax.broadcasted_iota(I32, (th, th), 1)
    before = _dot(jnp.where(r > c, 1.0, 0.0).astype(BF16), both) + carry_ref[...]
    rank0 = jnp.sum(jnp.where(oh0, before, 0.0), axis=-1, keepdims=True)
    rank1 = jnp.sum(jnp.where(oh1, before, 0.0), axis=-1, keepdims=True)
    lane8 = lax.broadcasted_iota(I32, (th, ROUTE_W), 1)
    rank_ref[...] = jnp.where(lane8 == 0, rank0, jnp.where(lane8 == 1, rank1, 0.0))
    total = carry_ref[...] + jnp.sum(both.astype(F32), axis=0, keepdims=True)
    carry_ref[...] = total
    cnt_ref[...] = total


def _expert_ranks(route, th):
    t = route.shape[0]
    return pl.pallas_call(
        _rank_kernel,
        grid=(t // th,),
        in_specs=[pl.BlockSpec((th, ROUTE_W), lambda i: (i, 0))],
        out_specs=[pl.BlockSpec((th, ROUTE_W), lambda i: (i, 0)),
                   pl.BlockSpec((1, LANES), lambda i: (0, 0))],
        out_shape=[jax.ShapeDtypeStruct((t, ROUTE_W), F32),
                   jax.ShapeDtypeStruct((1, LANES), F32)],
        scratch_shapes=[pltpu.VMEM((1, LANES), F32)],
        compiler_params=_cparams(("arbitrary",)),
        name="expert_ranks",
    )(route)


def _dest_kernel(route_ref, rank_ref, start_ref, dest_ref):
    th = route_ref.shape[0]
    route = route_ref[...]
    rank = rank_ref[...]
    lane_f = lax.broadcasted_iota(I32, (th, LANES), 1).astype(F32)
    start = start_ref[...]
    d0 = rank[:, 0:1] + jnp.sum(jnp.where(lane_f == route[:, 0:1], start, 0.0), axis=-1, keepdims=True)
    d1 = rank[:, 1:2] + jnp.sum(jnp.where(lane_f == route[:, 1:2], start, 0.0), axis=-1, keepdims=True)
    lane8 = lax.broadcasted_iota(I32, (th, ROUTE_W), 1)
    dest_ref[...] = (jnp.where(lane8 == 0, d0, jnp.where(lane8 == 1, d1, 0.0)) * float(ROW_TILE)).astype(I32)


def _expert_dest(route, rank, start_row, th):
    t = route.shape[0]
    return pl.pallas_call(
        _dest_kernel,
        grid=(t // th,),
        in_specs=[pl.BlockSpec((th, ROUTE_W), lambda i: (i, 0)),
                  pl.BlockSpec((th, ROUTE_W), lambda i: (i, 0)),
                  pl.BlockSpec((1, LANES), lambda i: (0, 0))],
        out_specs=pl.BlockSpec((th, ROUTE_W), lambda i: (i, 0)),
        out_shape=jax.ShapeDtypeStruct((t, ROUTE_W), I32),
        compiler_params=_cparams(("arbitrary",)),
        name="expert_dest",
    )(route, rank, start_row)


def _prefetched_indices(dest_hbm, idx_smem, idx_sem, tile, n_tiles):
    def idx_copy(t, slot):
        return pltpu.make_async_copy(dest_hbm.at[t], idx_smem.at[slot], idx_sem.at[slot])

    slot = tile % 2

    @pl.when(tile == 0)
    def _():
        idx_copy(0, 0).start()

    idx_copy(tile, slot).wait()

    @pl.when(tile + 1 < n_tiles)
    def _():
        idx_copy(tile + 1, 1 - slot).start()

    return slot


_DISPATCH_SLOTS = 3


def _dispatch_kernel(pad_end_ref, nused_ref, dest_hbm, h_hbm, xs_hbm, idx_smem, h_buf, zero_buf,
                     idx_sem, tile_sem, row_sem, zero_sem):
    td = h_buf.shape[1] // ROW_TILE
    blk_rows = zero_buf.shape[0]
    tile = pl.program_id(0)
    n_tiles = pl.num_programs(0)

    @pl.when(tile == 0)
    def _():
        def zero_block(first_row):
            return pltpu.make_async_copy(
                zero_buf, xs_hbm.at[pl.ds(pl.multiple_of(first_row, ROW_TILE), blk_rows)], zero_sem)

        zero_buf[...] = jnp.zeros_like(zero_buf)
        for e in range(N_EXPERTS):
            zero_block(jnp.maximum(pad_end_ref[e] * ROW_TILE - blk_rows, 0)).start()
        for e in range(N_EXPERTS):
            zero_block(0).wait()

        def zero_unused(bi, carry):
            zero_block(bi * blk_rows).start()
            zero_block(0).wait()
            return carry

        lax.fori_loop(nused_ref[0], xs_hbm.shape[0] // blk_rows, zero_unused, 0)

    idx_slot = _prefetched_indices(dest_hbm, idx_smem, idx_sem, tile, n_tiles)
    slot = tile % _DISPATCH_SLOTS
    sem_slot = tile % 2

    def tile_copy(t, sl):
        return pltpu.make_async_copy(h_hbm.at[pl.ds(t * (td * ROW_TILE), td * ROW_TILE)], h_buf.at[sl],
                                     tile_sem.at[sl])

    def row_copy(sl, sem_sl, r, dst_row):
        return pltpu.make_async_copy(h_buf.at[sl, pl.ds(ROW_TILE * r, ROW_TILE)],
                                     xs_hbm.at[pl.ds(pl.multiple_of(dst_row, ROW_TILE), ROW_TILE)], row_sem.at[sem_sl])

    @pl.when(tile == 0)
    def _():
        tile_copy(0, 0).start()

    @pl.when(tile + 1 < n_tiles)
    def _():
        tile_copy(tile + 1, (tile + 1) % _DISPATCH_SLOTS).start()

    tile_copy(tile, slot).wait()
    for n in range(2 * td):
        row_copy(slot, sem_slot, n // 2, idx_smem[idx_slot, n]).start(priority=n % 2)

    @pl.when(tile > 0)
    def _():
        for n in range(2 * td):
            row_copy(slot, 1 - sem_slot, n // 2, 0).wait()

    @pl.when(tile == n_tiles - 1)
    def _():
        for n in range(2 * td):
            row_copy(slot, sem_slot, n // 2, 0).wait()


def _dispatch(pad_end, n_used, dest_tiles, h2, n_blocks, blk, td):
    t = h2.shape[0] // ROW_TILE
    grid_spec = pltpu.PrefetchScalarGridSpec(
        num_scalar_prefetch=2,
        grid=(t // td,),
        in_specs=[pl.BlockSpec(memory_space=pl.ANY), pl.BlockSpec(memory_space=pl.ANY)],
        out_specs=pl.BlockSpec(memory_space=pl.ANY),
        scratch_shapes=[pltpu.SMEM((2, 2 * td), I32), pltpu.VMEM((_DISPATCH_SLOTS, td * ROW_TILE, LANES), F32),
                        pltpu.VMEM((blk * ROW_TILE, LANES), F32),
                        pltpu.SemaphoreType.DMA((2,)), pltpu.SemaphoreType.DMA((_DISPATCH_SLOTS,)),
                        pltpu.SemaphoreType.DMA((2,)), pltpu.SemaphoreType.DMA],
    )
    return pl.pallas_call(
        _dispatch_kernel,
        grid_spec=grid_spec,
        out_shape=jax.ShapeDtypeStruct((n_blocks * blk * ROW_TILE, LANES), F32),
        compiler_params=_cparams(("arbitrary",)),
        name="moe_dispatch",
    )(pad_end, n_used, dest_tiles, h2)


def _expert_kernel(be_ref, nused_ref, xs_ref, w1_ref, w3_ref, w2_ref, ys_ref, w13b_ref, w2b_ref):
    i = pl.program_id(0)
    changed = jnp.logical_or(i == 0, be_ref[i] != be_ref[jnp.maximum(i - 1, 0)])

    @pl.when(jnp.logical_and(changed, i < nused_ref[0]))
    def _():
        w13b_ref[:, :D_EXPERT] = w1_ref[0].astype(BF16)
        w13b_ref[:, D_EXPERT:] = w3_ref[0].astype(BF16)
        w2b_ref[...] = w2_ref[0].astype(BF16)

    @pl.when(i < nused_ref[0])
    def _():
        rows = xs_ref.shape[0] // ROW_TILE
        h13 = _dot(_load_tile_rows(xs_ref, (), 0, rows).astype(BF16), w13b_ref[...])
        hid = _silu(h13[:, :D_EXPERT]) * h13[:, D_EXPERT:]
        _store_tile_rows(ys_ref, (), 0, _dot(hid.astype(BF16), w2b_ref[...]))

    @pl.when(i >= nused_ref[0])
    def _():
        ys_ref[...] = jnp.zeros_like(ys_ref)


def _experts(blk_expert, n_used, xs, w1, w3, w2, blk):
    d = w1.shape[1]
    n_blocks = xs.shape[0] // (blk * ROW_TILE)
    row_map = lambda i, be, nu: (i, 0)
    grid_spec = pltpu.PrefetchScalarGridSpec(
        num_scalar_prefetch=2,
        grid=(n_blocks,),
        in_specs=[pl.BlockSpec((blk * ROW_TILE, LANES), row_map),
                  pl.BlockSpec((1, d, D_EXPERT), lambda i, be, nu: (be[i], 0, 0)),
                  pl.BlockSpec((1, d, D_EXPERT), lambda i, be, nu: (be[i], 0, 0)),
                  pl.BlockSpec((1, D_EXPERT, d), lambda i, be, nu: (be[i], 0, 0))],
        out_specs=pl.BlockSpec((blk * ROW_TILE, LANES), row_map),
        scratch_shapes=[pltpu.VMEM((d, 2 * D_EXPERT), BF16), pltpu.VMEM((D_EXPERT, d), BF16)],
    )
    return pl.pallas_call(
        _expert_kernel,
        grid_spec=grid_spec,
        out_shape=jax.ShapeDtypeStruct(xs.shape, F32),
        compiler_params=_cparams(("arbitrary",)),
        name="moe_experts",
    )(blk_expert, n_used, xs, w1, w3, w2)


def _combine_kernel(dest_hbm, ys_hbm, x1_ref, route_ref, gt2_ref, l2g_ref, l2b_ref, o_ref,
                    idx_smem, buf_ref, idx_sem, row_sem):
    b_i = pl.program_id(0)
    i = pl.program_id(1)
    td = x1_ref.shape[1]
    tile = b_i * pl.num_programs(1) + i
    n_tiles = pl.num_programs(0) * pl.num_programs(1)
    slot = tile % 2

    def idx_copy(t, sl):
        return pltpu.make_async_copy(dest_hbm.at[t], idx_smem.at[sl], idx_sem.at[sl])

    def row_copy(sl, n, src_row):
        return pltpu.make_async_copy(ys_hbm.at[pl.ds(pl.multiple_of(src_row, ROW_TILE), ROW_TILE)],
                                     buf_ref.at[sl, n % 2, pl.ds(ROW_TILE * (n // 2), ROW_TILE)], row_sem.at[sl])

    def start_rows(sl):
        for n in range(2 * td):
            row_copy(sl, n, idx_smem[sl, n]).start(priority=n % 2)

    @pl.when(tile == 0)
    def _():
        idx_copy(0, 0).start()
        idx_copy(0, 0).wait()
        start_rows(0)

        @pl.when(n_tiles > 1)
        def _():
            idx_copy(1, 1).start()

    @pl.when(tile + 1 < n_tiles)
    def _():
        idx_copy(tile + 1, 1 - slot).wait()
        start_rows(1 - slot)

        @pl.when(tile + 2 < n_tiles)
        def _():
            idx_copy(tile + 2, slot).start()

    for n in range(2 * td):
        row_copy(slot, n, 0).wait()
    route = route_ref[0]
    ffn = (_load_tile_rows(buf_ref, (slot, 0), 0, td) * route[:, 2:3]
           + _load_tile_rows(buf_ref, (slot, 1), 0, td) * route[:, 3:4])
    o_ref[0] = _layer_norm(ALPHA * x1_ref[0] + gt2_ref[0] * ffn) * l2g_ref[...] + l2b_ref[...]


def _combine(dest_tiles, ys, x1, route3, gt2, l2g, l2b, td):
    b, s, d = x1.shape
    row = lambda w: pl.BlockSpec((1, td, w), lambda bi, i: (bi, i, 0))
    const = lambda shape: pl.BlockSpec(shape, lambda bi, i: (0,) * len(shape))
    return pl.pallas_call(
        _combine_kernel,
        grid=(b, s // td),
        in_specs=[pl.BlockSpec(memory_space=pl.ANY),
                  pl.BlockSpec(memory_space=pl.ANY),
                  row(d), row(ROUTE_W),
                  pl.BlockSpec((1, 1, d), lambda bi, i: (bi, 0, 0)),
                  const((1, d)), const((1, d))],
        out_specs=row(d),
        out_shape=jax.ShapeDtypeStruct((b, s, d), F32),
        scratch_shapes=[pltpu.SMEM((2, 2 * td), I32), pltpu.VMEM((2, 2, td * ROW_TILE, LANES), F32),
                        pltpu.SemaphoreType.DMA((2,)), pltpu.SemaphoreType.DMA((2,))],
        compiler_params=_cparams(("arbitrary", "arbitrary")),
        name="moe_combine",
    )(dest_tiles, ys, x1, route3, gt2, l2g, l2b)


def _tile(n, pref):
    t = min(n, pref)
    assert n % t == 0, (n, t)
    return t


def _rope_tables(s):
    half = HEAD_DIM // 2
    inv = ROPE_THETA ** (-jnp.arange(0, half, 2, dtype=F32) / half)
    pos = jnp.arange(s)
    row = (pos // GRID_W).astype(F32)[:, None] * inv[None, :]
    col = (pos % GRID_W).astype(F32)[:, None] * inv[None, :]
    cos = jnp.concatenate([jnp.cos(row), jnp.cos(row), jnp.cos(col), jnp.cos(col)], axis=-1)
    sin = jnp.concatenate([-jnp.sin(row), jnp.sin(row), -jnp.sin(col), jnp.sin(col)], axis=-1)
    return cos, sin


def _layer(x, c, w_ada, b_ada, w_in, conv_w, a_log, dt_bias, gdn_norm_g, q_norm_g, k_norm_g, attn_norm_g,
           w_out, ln1_g, ln1_b, w_group, b_group, w_router, b_router, w1, w3, w2, ln2_g, ln2_b):
    b, s, d = x.shape
    t = b * s
    assert s % CHUNK == 0 and s % GRID_W == 0

    c_pad = jnp.pad(c, ((0, (-b) % 8), (0, 0)))
    mod = _adaln_mod(c_pad, w_ada, b_ada.reshape(1, -1), _tile(6 * d, 1536))[:b]
    mod3 = mod.reshape(b, 6, d)

    gq, gk, gv, gz, gab, aq, ak, av = jnp.split(
        w_in, [GDN_WIDTH, 2 * GDN_WIDTH, 3 * GDN_WIDTH, 4 * GDN_WIDTH, 4 * GDN_WIDTH + N_GATES,
               4 * GDN_WIDTH + N_GATES + ATTN_WIDTH, 4 * GDN_WIDTH + N_GATES + ATTN_WIDTH + ATTN_KV_WIDTH], axis=1)
    w_packed = jnp.concatenate([gq, gk, gv, gz, ak, jnp.pad(gab, ((0, 0), (0, LANES - N_GATES)))], axis=1).astype(BF16)
    wqv_t = jnp.concatenate([aq, av], axis=1).T.astype(BF16)
    wab_t = gab.T.astype(BF16)
    cos, sin = _rope_tables(s)

    tm = _tile(s, 512)
    gqkv, z, ab, abt, a_qt, a_k, a_vt = _in_projection(
        x, mod3, w_packed, wqv_t, wab_t, cos, sin, cos.T, sin.T, q_norm_g.reshape(-1, 1), k_norm_g.reshape(1, -1), tm)

    pad_gates = lambda p: jnp.pad(p.reshape(1, -1), ((0, 0), (0, N_GATES - p.size)))
    alog_c = pad_gates(a_log)
    dtb_c = pad_gates(dt_bias)
    tc = _tile(s, 512)
    gq_n, gk_n, gv_n, gb, egc, egl, ea, gcr = _gdn_prep(
        gqkv, conv_w, ab, abt, alog_c, dtb_c, alog_c.reshape(-1, 1), dtb_c.reshape(-1, 1), tc)
    a_m, b_m, a_dg, q_eff, o_in = _gdn_local(gq_n, gk_n, gv_n, gb, egc, egl, ea, gcr, _tile(s, 256))
    o_f, o_b = _gdn_scan(a_m, b_m, a_dg, q_eff, o_in, _tile(s // CHUNK, 4))

    tq, tk = _tile(s, 1024), _tile(s, 4096)
    att_lagged, excess = _attention_lagged(a_qt, a_k, a_vt, tq, tk)
    att = lax.cond(jnp.max(excess) > ATTN_LAG_LIMIT,
                   lambda: _attention(a_qt, a_k, a_vt, tq, tk), lambda: att_lagged)

    w_rt = jnp.pad(jnp.concatenate([w_group, w_router], axis=1), ((0, 0), (0, LANES - N_GROUPS - N_EXPERTS)))
    b_rt = jnp.pad(jnp.concatenate([b_group, b_router]).reshape(1, -1), ((0, 0), (0, LANES - N_GROUPS - N_EXPERTS)))
    x1, h2, route = _mixer_out(o_f, o_b, z, att, x, mod3, w_out.astype(BF16), gdn_norm_g.reshape(1, -1),
                               attn_norm_g.reshape(-1, 1), ln1_g.reshape(1, -1), ln1_b.reshape(1, -1),
                               w_rt, b_rt, _tile(s, 512))

    blk = 512
    th = _tile(t, 1024)
    route2 = route.reshape(t, ROUTE_W)
    rank, counts = _expert_ranks(route2, th)
    counts_i = counts[0, :N_EXPERTS].astype(I32)
    padded = (counts_i + blk - 1) // blk * blk
    pad_end = jnp.cumsum(padded)
    pad_start = pad_end - padded
    n_blocks = -(-(2 * t) // blk) + N_EXPERTS
    blk_pos = jnp.arange(n_blocks, dtype=I32) * blk
    blk_expert = jnp.minimum(jnp.sum((pad_end[None, :] <= blk_pos[:, None]).astype(I32), axis=1), N_EXPERTS - 1)
    n_used = jnp.maximum(pad_end[-1:] // blk, 1).astype(I32)
    start_row = jnp.pad(pad_start.astype(F32).reshape(1, -1), ((0, 0), (0, LANES - N_EXPERTS)))
    dest = _expert_dest(route2, rank, start_row, _tile(t, 4096))

    td = _tile(s, 256)
    dest_tiles = dest[:, 0:2].reshape(t // td, 2 * td)
    assert d == ROW_TILE * LANES
    xs = _dispatch(pad_end.astype(I32), n_used, dest_tiles, h2, n_blocks, blk, td)
    ys = _experts(blk_expert, n_used, xs, w1, w3, w2, blk)
    gt2 = mod3[:, 5:6, :]
    return _combine(dest_tiles, ys, x1, route, gt2, ln2_g.reshape(1, -1), ln2_b.reshape(1, -1), td)


def kernel(x, c, w_ada, b_ada, w_in, conv_w, a_log, dt_bias, gdn_norm_g, q_norm_g, k_norm_g, attn_norm_g,
           w_out, ln1_g, ln1_b, w_group, b_group, w_router, b_router, w1, w3, w2, ln2_g, ln2_b):
    for layer in range(w_ada.shape[0]):
        x = _layer(x, c, w_ada[layer], b_ada[layer], w_in[layer], conv_w[layer], a_log[layer], dt_bias[layer],
                   gdn_norm_g[layer], q_norm_g[layer], k_norm_g[layer], attn_norm_g[layer], w_out[layer],
                   ln1_g[layer], ln1_b[layer], w_group[layer], b_group[layer], w_router[layer], b_router[layer],
                   w1[layer], w3[layer], w2[layer], ln2_g[layer], ln2_b[layer])
    return x
```

```python
import math

import jax
import jax.numpy as jnp
from jax import lax
from jax.experimental import pallas as pl
from jax.experimental.pallas import tpu as pltpu

F32 = jnp.float32
BF16 = jnp.bfloat16
I32 = jnp.int32

HEAD_DIM = 128
GDN_HEADS = 4
GDN_WIDTH = GDN_HEADS * HEAD_DIM
ATTN_Q_HEADS = 4
ATTN_KV_HEADS = 2
ATTN_GROUP = ATTN_Q_HEADS // ATTN_KV_HEADS
ATTN_WIDTH = ATTN_Q_HEADS * HEAD_DIM
ATTN_KV_WIDTH = ATTN_KV_HEADS * HEAD_DIM
CONV_K = 5
CHUNK = 64
GRID_W = 64
ROPE_THETA = 10000.0
N_GROUPS = 4
EXPERTS_PER_GROUP = 8
N_EXPERTS = N_GROUPS * EXPERTS_PER_GROUP
D_EXPERT = 256
DEPTH = 1
ALPHA = (2.0 * DEPTH) ** 0.25
LN_EPS = 1e-5
RMS_EPS = 1e-6

LANES = 128
VMEM_LIMIT_BYTES = 56 * 1024 * 1024
NEG_BIG = -1e30


def _cparams(semantics):
    return pltpu.CompilerParams(dimension_semantics=semantics, vmem_limit_bytes=VMEM_LIMIT_BYTES)


def _dot(a, b):
    return jnp.dot(a, b, preferred_element_type=F32)


def _dot_nt(a, b):
    return lax.dot_general(a, b, (((1,), (1,)), ((), ())), preferred_element_type=F32)


def _dot_tn(a, b):
    return lax.dot_general(a, b, (((0,), (0,)), ((), ())), preferred_element_type=F32)


def _split3(a):
    hi = a.astype(BF16)
    r = a - hi.astype(F32)
    mid = r.astype(BF16)
    lo = (r - mid.astype(F32)).astype(BF16)
    return hi, mid, lo


def _dot_f32_lhs_exact(a_bf16_exact, b):
    hi, mid, lo = _split3(b)
    return _dot(a_bf16_exact, hi) + _dot(a_bf16_exact, mid) + _dot(a_bf16_exact, lo)


def _dot_f32_rhs_exact(a, b_bf16_exact):
    hi, mid, lo = _split3(a)
    return _dot(hi, b_bf16_exact) + _dot(mid, b_bf16_exact) + _dot(lo, b_bf16_exact)


def _dot_f32(a, b):
    ah, am, _ = _split3(a)
    bh, bm, _ = _split3(b)
    return _dot(ah, bh) + (_dot(ah, bm) + _dot(am, bh))


ROW_TILE = 8


def _store_tile_rows(ref, lead, row0, value):
    n = value.shape[0]
    for c in range(ROW_TILE):
        ref[lead + (pl.ds(ROW_TILE * row0 + c, n, stride=ROW_TILE), slice(None))] = value[:, c * LANES:(c + 1) * LANES]


def _load_tile_rows(ref, lead, row0, n):
    return jnp.concatenate(
        [ref[lead + (pl.ds(ROW_TILE * row0 + c, n, stride=ROW_TILE), slice(None))] for c in range(ROW_TILE)], axis=1)


def _sigmoid(x):
    return 1.0 / (1.0 + jnp.exp(-x))


def _silu(x):
    return x * _sigmoid(x)


def _softplus(x):
    return jnp.maximum(x, 0.0) + jnp.log1p(jnp.exp(-jnp.abs(x)))


def _layer_norm(x):
    mu = jnp.mean(x, axis=-1, keepdims=True)
    xc = x - mu
    var = jnp.mean(xc * xc, axis=-1, keepdims=True)
    return xc * lax.rsqrt(var + LN_EPS)


def _mod_kernel(c_ref, w_ref, b_ref, o_ref):
    o_ref[...] = _dot_f32(_silu(c_ref[...]), w_ref[...]) + b_ref[...]


def _adaln_mod(c_pad, w_ada, b_ada, tn):
    rows, d = c_pad.shape
    n = w_ada.shape[1]
    return pl.pallas_call(
        _mod_kernel,
        grid=(n // tn,),
        in_specs=[pl.BlockSpec((rows, d), lambda j: (0, 0)),
                  pl.BlockSpec((d, tn), lambda j: (0, j)),
                  pl.BlockSpec((1, tn), lambda j: (0, j))],
        out_specs=pl.BlockSpec((rows, tn), lambda j: (0, j)),
        out_shape=jax.ShapeDtypeStruct((rows, n), F32),
        compiler_params=_cparams(("arbitrary",)),
        name="adaln_mod",
    )(c_pad, w_ada, b_ada)


_W_GQKV = 0
_W_Z = 3 * GDN_WIDTH
_W_AK = _W_Z + GDN_WIDTH
_W_AB = _W_AK + ATTN_KV_WIDTH
_W_COLS = _W_AB + LANES
N_GATES = 4 * GDN_HEADS
LOG2E = math.log2(math.e)


def _rope(xh, cos, sin_signed, lane):
    fwd = pltpu.roll(xh, 32, 1)
    bwd = pltpu.roll(xh, LANES - 32, 1)
    partner = jnp.where((lane % 64) < 32, bwd, fwd)
    return xh * cos + partner * sin_signed


def _rope_t(xt, cos_t, sin_signed_t):
    q = HEAD_DIM // 4
    partner = jnp.concatenate([xt[q:2 * q], xt[0:q], xt[3 * q:4 * q], xt[2 * q:3 * q]], axis=0)
    return xt * cos_t + partner * sin_signed_t


def _inproj_kernel(x_ref, mod_ref, w_ref, wqvt_ref, wabt_ref, cos_ref, sin_ref, cost_ref, sint_ref, qg_ref, kg_ref,
                   gqkv_ref, z_ref, ab_ref, abt_ref, aqt_ref, ak_ref, avt_ref):
    sh1 = mod_ref[0, 0:1, :]
    sc1 = mod_ref[0, 1:2, :]
    q_scale = (HEAD_DIM ** -0.5) * LOG2E
    h = _layer_norm(x_ref[0]) * (1.0 + sc1) + sh1
    hb = h.astype(BF16)
    gqkv_ref[0] = _dot(hb, w_ref[:, _W_GQKV:_W_Z]).astype(BF16)
    z_ref[0] = _dot(hb, w_ref[:, _W_Z:_W_AK]).astype(BF16)
    ab_ref[0] = _dot(hb, w_ref[:, _W_AB:_W_COLS])[:, 0:N_GATES]
    abt_ref[0] = _dot_nt(wabt_ref[...], hb)
    ak = _dot(hb, w_ref[:, _W_AK:_W_AB])
    cos = cos_ref[...]
    sin = sin_ref[...]
    lane = lax.broadcasted_iota(I32, cos.shape, 1)
    for j in range(ATTN_KV_HEADS):
        xh = ak[:, j * HEAD_DIM:(j + 1) * HEAD_DIM]
        xn = xh * lax.rsqrt(jnp.mean(xh * xh, axis=-1, keepdims=True) + RMS_EPS) * kg_ref[...]
        ak_ref[0, :, j * HEAD_DIM:(j + 1) * HEAD_DIM] = _rope(xn, cos, sin, lane).astype(BF16)
    qvt = _dot_nt(wqvt_ref[...], hb)
    cos_t = cost_ref[...]
    sin_t = sint_ref[...]
    for i in range(ATTN_Q_HEADS):
        xt = qvt[i * HEAD_DIM:(i + 1) * HEAD_DIM, :]
        xn = xt * lax.rsqrt(jnp.mean(xt * xt, axis=0, keepdims=True) + RMS_EPS) * qg_ref[...]
        aqt_ref[0, i * HEAD_DIM:(i + 1) * HEAD_DIM, :] = (_rope_t(xn, cos_t, sin_t) * q_scale).astype(BF16)
    avt_ref[0] = qvt[ATTN_WIDTH:, :].astype(BF16)


def _in_projection(x, mod3, w_packed, wqv_t, wab_t, cos, sin, cos_t, sin_t, qg_col, kg_row, tm):
    b, s, d = x.shape
    grid = (b, s // tm)
    row = lambda w: pl.BlockSpec((1, tm, w), lambda bi, i: (bi, i, 0))
    col = lambda h: pl.BlockSpec((1, h, tm), lambda bi, i: (bi, 0, i))
    const = lambda shape: pl.BlockSpec(shape, lambda bi, i: (0,) * len(shape))
    out_shapes = [
        jax.ShapeDtypeStruct((b, s, 3 * GDN_WIDTH), BF16),
        jax.ShapeDtypeStruct((b, s, GDN_WIDTH), BF16),
        jax.ShapeDtypeStruct((b, s, N_GATES), F32),
        jax.ShapeDtypeStruct((b, N_GATES, s), F32),
        jax.ShapeDtypeStruct((b, ATTN_WIDTH, s), BF16),
        jax.ShapeDtypeStruct((b, s, ATTN_KV_WIDTH), BF16),
        jax.ShapeDtypeStruct((b, ATTN_KV_WIDTH, s), BF16),
    ]
    out_specs = [row(3 * GDN_WIDTH), row(GDN_WIDTH), row(N_GATES), col(N_GATES),
                 col(ATTN_WIDTH), row(ATTN_KV_WIDTH), col(ATTN_KV_WIDTH)]
    return pl.pallas_call(
        _inproj_kernel,
        grid=grid,
        in_specs=[row(d),
                  pl.BlockSpec((1, 6, d), lambda bi, i: (bi, 0, 0)),
                  const((d, _W_COLS)),
                  const((ATTN_WIDTH + ATTN_KV_WIDTH, d)),
                  const((N_GATES, d)),
                  pl.BlockSpec((tm, HEAD_DIM), lambda bi, i: (i, 0)),
                  pl.BlockSpec((tm, HEAD_DIM), lambda bi, i: (i, 0)),
                  pl.BlockSpec((HEAD_DIM, tm), lambda bi, i: (0, i)),
                  pl.BlockSpec((HEAD_DIM, tm), lambda bi, i: (0, i)),
                  const((HEAD_DIM, 1)),
                  const((1, HEAD_DIM))],
        out_specs=out_specs,
        out_shape=out_shapes,
        compiler_params=_cparams(("arbitrary", "arbitrary")),
        name="in_projection",
    )(x, mod3, w_packed, wqv_t, wab_t, cos, sin, cos_t, sin_t, qg_col, kg_row)


_HALO = 16


def _chunk_masks(n):
    r = lax.broadcasted_iota(I32, (n, n), 0)
    c = lax.broadcasted_iota(I32, (n, n), 1)
    same = (r // CHUNK) == (c // CHUNK)
    lower = same & (r >= c)
    upper = same & (r <= c)
    return lower, upper


def _gdn_prep_kernel(prev_ref, main_ref, next_ref, convw_ref, ab_ref, abt_ref,
                     alog_c_ref, dtb_c_ref, alog_r_ref, dtb_r_ref,
                     q_ref, k_ref, v_ref, gb_ref, egc_ref, egl_ref, ea_ref, gcr_ref, buf_ref):
    i = pl.program_id(1)
    n_i = pl.num_programs(1)
    tc = main_ref.shape[1]
    prev_scale = jnp.where(i > 0, 1.0, 0.0).astype(F32)
    next_scale = jnp.where(i < n_i - 1, 1.0, 0.0).astype(F32)
    buf_ref[0:_HALO, :] = prev_ref[0].astype(F32) * prev_scale
    buf_ref[_HALO:_HALO + tc, :] = main_ref[0].astype(F32)
    buf_ref[_HALO + tc:, :] = next_ref[0].astype(F32) * next_scale
    pad = CONV_K // 2
    for part, out_ref in enumerate((q_ref, k_ref, v_ref)):
        for hh in range(GDN_HEADS):
            c0 = part * GDN_WIDTH + hh * HEAD_DIM
            acc = None
            for j in range(CONV_K):
                term = buf_ref[_HALO - pad + j:_HALO - pad + j + tc, c0:c0 + HEAD_DIM] * convw_ref[j:j + 1, c0:c0 + HEAD_DIM]
                acc = term if acc is None else acc + term
            y = _silu(acc)
            if part < 2:
                y = y * lax.rsqrt(jnp.sum(y * y, axis=-1, keepdims=True) + RMS_EPS)
            if part == 0:
                y = y * (HEAD_DIM ** -0.5)
            out_ref[0, :, hh * HEAD_DIM:(hh + 1) * HEAD_DIM] = y.astype(BF16)

    lower, upper = _chunk_masks(tc)
    lower_b = jnp.where(lower, 1.0, 0.0).astype(BF16)
    upper_b = jnp.where(upper, 1.0, 0.0).astype(BF16)
    n_dir = 2 * GDN_HEADS
    ab = ab_ref[0]
    lane = lax.broadcasted_iota(I32, ab.shape, 1)
    g = -jnp.exp(alog_c_ref[...]) * _softplus(ab + dtb_c_ref[...])
    pre = _dot_f32_lhs_exact(lower_b, g)
    suf = _dot_f32_lhs_exact(upper_b, g)
    gtot = pre + suf - g
    gc = jnp.where(lane < GDN_HEADS, pre, suf)
    gb_ref[0] = jnp.where(lane < n_dir, gc, _sigmoid(ab))
    egc_ref[0] = jnp.exp(gc)
    egl_ref[0] = jnp.exp(gtot - gc)
    ea_ref[0] = jnp.exp(gtot)
    abt = abt_ref[0]
    row = lax.broadcasted_iota(I32, abt.shape, 0)
    g_r = -jnp.exp(alog_r_ref[...]) * _softplus(abt + dtb_r_ref[...])
    pre_r = _dot_f32_rhs_exact(g_r, upper_b)
    suf_r = _dot_f32_rhs_exact(g_r, lower_b)
    gcr_ref[0] = jnp.where(row < GDN_HEADS, pre_r, suf_r)[0:n_dir]


def _gdn_prep(gqkv, conv_w, ab, abt, alog_c, dtb_c, alog_r, dtb_r, tc):
    b, s, w = gqkv.shape
    nh = tc // _HALO
    n_dir = 2 * GDN_HEADS
    grid = (b, s // tc)
    row = lambda width: pl.BlockSpec((1, tc, width), lambda bi, i: (bi, i, 0))
    const = lambda shape: pl.BlockSpec(shape, lambda bi, i: (0,) * len(shape))
    last_halo = s // _HALO - 1
    out_shapes = ([jax.ShapeDtypeStruct((b, s, GDN_WIDTH), BF16)] * 3
                  + [jax.ShapeDtypeStruct((b, s, N_GATES), F32)] * 4
                  + [jax.ShapeDtypeStruct((b, n_dir, s), F32)])
    out_specs = ([row(GDN_WIDTH)] * 3 + [row(N_GATES)] * 4
                 + [pl.BlockSpec((1, n_dir, tc), lambda bi, i: (bi, 0, i))])
    return pl.pallas_call(
        _gdn_prep_kernel,
        grid=grid,
        in_specs=[pl.BlockSpec((1, _HALO, w), lambda bi, i: (bi, jnp.maximum(i * nh - 1, 0), 0)),
                  row(w),
                  pl.BlockSpec((1, _HALO, w), lambda bi, i: (bi, jnp.minimum((i + 1) * nh, last_halo), 0)),
                  const((CONV_K, w)),
                  row(N_GATES),
                  pl.BlockSpec((1, N_GATES, tc), lambda bi, i: (bi, 0, i)),
                  const((1, N_GATES)), const((1, N_GATES)),
                  const((N_GATES, 1)), const((N_GATES, 1))],
        out_specs=out_specs,
        out_shape=out_shapes,
        scratch_shapes=[pltpu.VMEM((tc + 2 * _HALO, w), F32)],
        compiler_params=_cparams(("arbitrary", "arbitrary")),
        name="gdn_prep",
    )(gqkv, gqkv, gqkv, conv_w, ab, abt, alog_c, dtb_c, alog_r, dtb_r)


def _gdn_local_kernel(q_ref, k_ref, v_ref, gb_ref, egc_ref, egl_ref, ea_ref, gcr_ref,
                      a_ref, b_ref, dg_ref, qe_ref, oi_ref):
    tc = q_ref.shape[1]
    nch = tc // CHUNK
    lower, upper = _chunk_masks(tc)
    r = lax.broadcasted_iota(I32, (tc, tc), 0)
    c = lax.broadcasted_iota(I32, (tc, tc), 1)
    eye = r == c
    blk_xor = r ^ c
    n_dir = 2 * GDN_HEADS
    combos = [(d, hh) for d in range(2) for hh in range(GDN_HEADS)]
    head = lambda hh: slice(hh * HEAD_DIM, (hh + 1) * HEAD_DIM)
    kk = [_dot_nt(k_ref[0, :, head(hh)], k_ref[0, :, head(hh)]) for hh in range(GDN_HEADS)]
    qk = [_dot_nt(q_ref[0, :, head(hh)], k_ref[0, :, head(hh)]) for hh in range(GDN_HEADS)]
    side_r = lax.broadcasted_iota(I32, (CHUNK, tc), 0)
    side_c = lax.broadcasted_iota(I32, (CHUNK, tc), 1)
    side_eye = (side_c % CHUNK) == side_r
    same_chunk = blk_xor < CHUNK

    def side_by_side(bd):
        out = None
        for ch in range(nch):
            part = jnp.where((side_c // CHUNK) == ch, bd[ch * CHUNK:(ch + 1) * CHUNK, :], 0.0)
            out = part if out is None else out + part
        return out

    def block_diag(ss):
        return jnp.where(same_chunk, jnp.concatenate([ss] * nch, axis=0), 0.0)

    decay, m, t = {}, {}, {}
    for cb in combos:
        d, hh = cb
        idx = d * GDN_HEADS + hh
        incl = lower if d == 0 else upper
        gc = gb_ref[0, :, idx:idx + 1]
        beta = gb_ref[0, :, n_dir + idx:n_dir + idx + 1]
        gcr = gcr_ref[0, idx:idx + 1, :]
        decay[cb] = jnp.where(incl, jnp.exp(jnp.minimum(gc - gcr, 0.0)), 0.0)
        m[cb] = jnp.where(eye, 0.0, beta * kk[hh] * decay[cb])
        t[cb] = jnp.where(side_eye, 1.0, 0.0) - side_by_side(jnp.where(blk_xor < 2, m[cb], 0.0))
    sz = 2
    while sz < CHUNK:
        join = (blk_xor >= sz) & (blk_xor < 2 * sz)
        tc_s = {cb: _dot(t[cb].astype(BF16), jnp.where(join, m[cb], 0.0).astype(BF16)).astype(BF16) for cb in combos}
        for cb in combos:
            t[cb] = t[cb] - _dot(tc_s[cb], block_diag(t[cb]).astype(BF16))
        sz *= 2
    t = {cb: block_diag(t[cb]) for cb in combos}
    solb = {}
    for cb in combos:
        d, hh = cb
        idx = d * GDN_HEADS + hh
        beta = gb_ref[0, :, n_dir + idx:n_dir + idx + 1]
        egc = egc_ref[0, :, idx:idx + 1]
        rhs = jnp.concatenate([v_ref[0, :, head(hh)].astype(F32) * beta,
                               k_ref[0, :, head(hh)].astype(F32) * (beta * egc)], axis=1).astype(BF16)
        solb[cb] = _dot(t[cb].astype(BF16), rhs).astype(BF16)
    for cb in combos:
        d, hh = cb
        idx = d * GDN_HEADS + hh
        egc = egc_ref[0, :, idx:idx + 1]
        qo = _dot((qk[hh] * decay[cb]).astype(BF16), solb[cb])
        oi_ref[0, d, :, head(hh)] = qo[:, :HEAD_DIM].astype(BF16)
        qe_ref[0, d, :, head(hh)] = (q_ref[0, :, head(hh)].astype(F32) * egc - qo[:, HEAD_DIM:]).astype(BF16)
    for cb in combos:
        d, hh = cb
        idx = d * GDN_HEADS + hh
        egl = egl_ref[0, :, idx:idx + 1]
        ea = ea_ref[0, :, idx:idx + 1]
        kg = (k_ref[0, :, head(hh)].astype(F32) * egl).astype(BF16)
        for ci in range(nch):
            rs = slice(ci * CHUNK, (ci + 1) * CHUNK)
            ab = _dot_tn(kg[rs], solb[cb][rs])
            a_ref[0, d, ci, :, head(hh)] = (-ab[:, HEAD_DIM:]).astype(BF16)
            b_ref[0, d, ci, :, head(hh)] = ab[:, :HEAD_DIM].astype(BF16)
            dg_ref[0, d, ci, :, head(hh)] = jnp.broadcast_to(ea[ci * CHUNK:ci * CHUNK + 1, :], (8, HEAD_DIM))


def _gdn_local(q, k, v, gb, egc, egl, ea, gcr, tc):
    b, s, w = q.shape
    n = s // CHUNK
    nch = tc // CHUNK
    n_dir = 2 * GDN_HEADS
    grid = (b, s // tc)
    row = lambda width: pl.BlockSpec((1, tc, width), lambda bi, i: (bi, i, 0))
    return pl.pallas_call(
        _gdn_local_kernel,
        grid=grid,
        in_specs=[row(w), row(w), row(w), row(N_GATES), row(N_GATES), row(N_GATES), row(N_GATES),
                  pl.BlockSpec((1, n_dir, tc), lambda bi, i: (bi, 0, i))],
        out_specs=[pl.BlockSpec((1, 2, nch, HEAD_DIM, w), lambda bi, i: (bi, 0, i, 0, 0)),
                   pl.BlockSpec((1, 2, nch, HEAD_DIM, w), lambda bi, i: (bi, 0, i, 0, 0)),
                   pl.BlockSpec((1, 2, nch, 8, w), lambda bi, i: (bi, 0, i, 0, 0)),
                   pl.BlockSpec((1, 2, tc, w), lambda bi, i: (bi, 0, i, 0)),
                   pl.BlockSpec((1, 2, tc, w), lambda bi, i: (bi, 0, i, 0))],
        out_shape=[jax.ShapeDtypeStruct((b, 2, n, HEAD_DIM, w), BF16),
                   jax.ShapeDtypeStruct((b, 2, n, HEAD_DIM, w), BF16),
                   jax.ShapeDtypeStruct((b, 2, n, 8, w), F32),
                   jax.ShapeDtypeStruct((b, 2, s, w), BF16),
                   jax.ShapeDtypeStruct((b, 2, s, w), BF16)],
        compiler_params=_cparams(("arbitrary", "arbitrary")),
        name="gdn_local",
    )(q, k, v, gb, egc, egl, ea, gcr)


def _gdn_scan_kernel(af_ref, bf_ref, df_ref, qf_ref, of_ref, ab_ref, bb_ref, db_ref, qb_ref, ob_ref,
                     outf_ref, outb_ref, s_ref):
    @pl.when(pl.program_id(1) == 0)
    def _():
        s_ref[...] = jnp.zeros_like(s_ref)

    cs = af_ref.shape[2]
    dirs = ((af_ref, bf_ref, df_ref, qf_ref, of_ref, outf_ref), (ab_ref, bb_ref, db_ref, qb_ref, ob_ref, outb_ref))
    for j in range(cs):
        for d, (a_ref, b_ref, dg_ref, q_ref, o_ref, out_ref) in enumerate(dirs):
            ci = j if d == 0 else cs - 1 - j
            rs = slice(ci * CHUNK, (ci + 1) * CHUNK)
            for hh in range(GDN_HEADS):
                hs = slice(hh * HEAD_DIM, (hh + 1) * HEAD_DIM)
                si = d * GDN_HEADS + hh
                st = s_ref[si]
                stb = st.astype(BF16)
                out_ref[0, rs, hs] = (_dot(q_ref[0, 0, rs, hs], stb) + o_ref[0, 0, rs, hs].astype(F32)).astype(BF16)
                s_ref[si] = (dg_ref[0, 0, ci, 0:1, hs] * st + _dot(a_ref[0, 0, ci, :, hs], stb)
                             + b_ref[0, 0, ci, :, hs].astype(F32))


def _gdn_scan(a, bm, dg, qe, oi, cs):
    b, _, n, _, w = a.shape
    s = n * CHUNK
    ns = n // cs
    tr = cs * CHUNK
    fwd5 = lambda bi, i: (bi, 0, i, 0, 0)
    bwd5 = lambda bi, i: (bi, 1, ns - 1 - i, 0, 0)
    fwd4 = lambda bi, i: (bi, 0, i, 0)
    bwd4 = lambda bi, i: (bi, 1, ns - 1 - i, 0)
    blk5 = (1, 1, cs, HEAD_DIM, w)
    blkd = (1, 1, cs, 8, w)
    blk4 = (1, 1, tr, w)
    return pl.pallas_call(
        _gdn_scan_kernel,
        grid=(b, ns),
        in_specs=[pl.BlockSpec(blk5, fwd5), pl.BlockSpec(blk5, fwd5), pl.BlockSpec(blkd, fwd5),
                  pl.BlockSpec(blk4, fwd4), pl.BlockSpec(blk4, fwd4),
                  pl.BlockSpec(blk5, bwd5), pl.BlockSpec(blk5, bwd5), pl.BlockSpec(blkd, bwd5),
                  pl.BlockSpec(blk4, bwd4), pl.BlockSpec(blk4, bwd4)],
        out_specs=[pl.BlockSpec((1, tr, w), lambda bi, i: (bi, i, 0)),
                   pl.BlockSpec((1, tr, w), lambda bi, i: (bi, ns - 1 - i, 0))],
        out_shape=[jax.ShapeDtypeStruct((b, s, w), BF16)] * 2,
        scratch_shapes=[pltpu.VMEM((2 * GDN_HEADS, HEAD_DIM, HEAD_DIM), F32)],
        compiler_params=_cparams(("arbitrary", "arbitrary")),
        name="gdn_scan",
    )(a, bm, dg, qe, oi, a, bm, dg, qe, oi)


ATTN_SUB = 512
ATTN_LAG_LIMIT = 64.0


def _attn_lagged_kernel(qt_ref, k_ref, vt_ref, ot_ref, ex_ref, q2t_ref, m_ref, l_ref, acc_ref, exc_ref):
    j = pl.program_id(3)
    tq = qt_ref.shape[2]
    tk = k_ref.shape[1]

    @pl.when(j == 0)
    def _():
        for gi in range(ATTN_GROUP):
            q2t_ref[:, gi * tq:(gi + 1) * tq] = qt_ref[0, gi * HEAD_DIM:(gi + 1) * HEAD_DIM, :]
        m_ref[...] = jnp.max(_dot(k_ref[0, 0:ATTN_SUB, :], q2t_ref[...]), axis=0, keepdims=True)
        l_ref[...] = jnp.zeros_like(l_ref)
        acc_ref[...] = jnp.zeros_like(acc_ref)
        exc_ref[...] = jnp.zeros_like(exc_ref)

    q2t = q2t_ref[...]
    m_est = m_ref[...]
    m_run = m_est
    l_run = l_ref[...]
    n_sub = tk // ATTN_SUB
    scores = lambda jj: _dot(k_ref[0, jj * ATTN_SUB:(jj + 1) * ATTN_SUB, :], q2t)
    st_next = scores(0)
    for jj in range(n_sub):
        ks = slice(jj * ATTN_SUB, (jj + 1) * ATTN_SUB)
        st = st_next
        if jj + 1 < n_sub:
            st_next = scores(jj + 1)
        p = jnp.exp2(st - m_est)
        m_run = jnp.maximum(m_run, jnp.max(st, axis=0, keepdims=True))
        l_run = l_run + jnp.sum(p, axis=0, keepdims=True)
        acc_ref[...] += _dot(vt_ref[0, :, ks], p.astype(BF16))
    rebase = jnp.exp2(m_est - m_run)
    acc_ref[...] = acc_ref[...] * rebase
    l_ref[...] = l_run * rebase
    m_ref[...] = m_run
    exc_ref[...] = jnp.maximum(exc_ref[...], m_run - m_est)

    @pl.when(j == pl.num_programs(3) - 1)
    def _():
        out = acc_ref[...] / l_ref[...]
        for gi in range(ATTN_GROUP):
            ot_ref[0, gi * HEAD_DIM:(gi + 1) * HEAD_DIM, :] = out[:, gi * tq:(gi + 1) * tq].astype(BF16)
        ex_ref[0] = exc_ref[...]


def _attention_lagged(q_t, k, v_t, tq, tk):
    b, _, s = q_t.shape
    gw = ATTN_GROUP * HEAD_DIM
    nq = s // tq
    assert tk % ATTN_SUB == 0
    return pl.pallas_call(
        _attn_lagged_kernel,
        grid=(b, ATTN_KV_HEADS, nq, s // tk),
        in_specs=[pl.BlockSpec((1, gw, tq), lambda bi, g, i, j: (bi, g, i)),
                  pl.BlockSpec((1, tk, HEAD_DIM), lambda bi, g, i, j: (bi, j, g)),
                  pl.BlockSpec((1, HEAD_DIM, tk), lambda bi, g, i, j: (bi, g, j))],
        out_specs=[pl.BlockSpec((1, gw, tq), lambda bi, g, i, j: (bi, g, i)),
                   pl.BlockSpec((1, 1, ATTN_GROUP * tq), lambda bi, g, i, j: ((bi * ATTN_KV_HEADS + g) * nq + i, 0, 0))],
        out_shape=[jax.ShapeDtypeStruct((b, ATTN_WIDTH, s), BF16),
                   jax.ShapeDtypeStruct((b * ATTN_KV_HEADS * nq, 1, ATTN_GROUP * tq), F32)],
        scratch_shapes=[pltpu.VMEM((HEAD_DIM, ATTN_GROUP * tq), BF16),
                        pltpu.VMEM((1, ATTN_GROUP * tq), F32),
                        pltpu.VMEM((1, ATTN_GROUP * tq), F32),
                        pltpu.VMEM((HEAD_DIM, ATTN_GROUP * tq), F32),
                        pltpu.VMEM((1, ATTN_GROUP * tq), F32)],
        compiler_params=_cparams(("arbitrary", "arbitrary", "arbitrary", "arbitrary")),
        name="gqa_attention_lagged",
    )(q_t, k, v_t)


def _attn_kernel(qt_ref, k_ref, vt_ref, ot_ref, q2t_ref, m_ref, l_ref, acc_ref):
    j = pl.program_id(3)
    tq = qt_ref.shape[2]
    tk = k_ref.shape[1]

    @pl.when(j == 0)
    def _():
        for gi in range(ATTN_GROUP):
            q2t_ref[:, gi * tq:(gi + 1) * tq] = qt_ref[0, gi * HEAD_DIM:(gi + 1) * HEAD_DIM, :]
        m_ref[...] = jnp.full_like(m_ref, NEG_BIG)
        l_ref[...] = jnp.zeros_like(l_ref)
        acc_ref[...] = jnp.zeros_like(acc_ref)

    q2t = q2t_ref[...]
    m_prev = m_ref[...]
    l_prev = l_ref[...]
    n_sub = tk // ATTN_SUB
    scores = lambda jj: _dot(k_ref[0, jj * ATTN_SUB:(jj + 1) * ATTN_SUB, :], q2t)
    st_next = scores(0)
    for jj in range(n_sub):
        ks = slice(jj * ATTN_SUB, (jj + 1) * ATTN_SUB)
        st = st_next
        if jj + 1 < n_sub:
            st_next = scores(jj + 1)
        m_new = jnp.maximum(m_prev, jnp.max(st, axis=0, keepdims=True))
        alpha = jnp.exp2(m_prev - m_new)
        p = jnp.exp2(st - m_new)
        l_prev = alpha * l_prev + jnp.sum(p, axis=0, keepdims=True)
        acc_ref[...] = alpha * acc_ref[...] + _dot(vt_ref[0, :, ks], p.astype(BF16))
        m_prev = m_new
    m_ref[...] = m_prev
    l_ref[...] = l_prev

    @pl.when(j == pl.num_programs(3) - 1)
    def _():
        out = acc_ref[...] / l_ref[...]
        for gi in range(ATTN_GROUP):
            ot_ref[0, gi * HEAD_DIM:(gi + 1) * HEAD_DIM, :] = out[:, gi * tq:(gi + 1) * tq].astype(BF16)


def _attention(q_t, k, v_t, tq, tk):
    b, _, s = q_t.shape
    gw = ATTN_GROUP * HEAD_DIM
    assert tk % ATTN_SUB == 0
    return pl.pallas_call(
        _attn_kernel,
        grid=(b, ATTN_KV_HEADS, s // tq, s // tk),
        in_specs=[pl.BlockSpec((1, gw, tq), lambda bi, g, i, j: (bi, g, i)),
                  pl.BlockSpec((1, tk, HEAD_DIM), lambda bi, g, i, j: (bi, j, g)),
                  pl.BlockSpec((1, HEAD_DIM, tk), lambda bi, g, i, j: (bi, g, j))],
        out_specs=pl.BlockSpec((1, gw, tq), lambda bi, g, i, j: (bi, g, i)),
        out_shape=jax.ShapeDtypeStruct((b, ATTN_WIDTH, s), BF16),
        scratch_shapes=[pltpu.VMEM((HEAD_DIM, ATTN_GROUP * tq), BF16),
                        pltpu.VMEM((1, ATTN_GROUP * tq), F32),
                        pltpu.VMEM((1, ATTN_GROUP * tq), F32),
                        pltpu.VMEM((HEAD_DIM, ATTN_GROUP * tq), F32)],
        compiler_params=_cparams(("arbitrary", "arbitrary", "arbitrary", "arbitrary")),
        name="gqa_attention",
    )(q_t, k, v_t)


ROUTE_W = 8


def _first_argmax(vals, lane_f, valid):
    vmax = jnp.max(jnp.where(valid, vals, NEG_BIG), axis=-1, keepdims=True)
    idx = jnp.min(jnp.where(valid & (vals == vmax), lane_f, float(LANES)), axis=-1, keepdims=True)
    return vmax, idx


def _mix_kernel(of_ref, ob_ref, z_ref, att_ref, x_ref, mod_ref, wout_ref, gng_ref, ang_ref,
                l1g_ref, l1b_ref, wrt_ref, brt_ref, x1_ref, h2_ref, route_ref):
    o = of_ref[0].astype(F32) + ob_ref[0].astype(F32)
    z = z_ref[0].astype(F32)
    parts = []
    for hh in range(GDN_HEADS):
        hs = slice(hh * HEAD_DIM, (hh + 1) * HEAD_DIM)
        oh = o[:, hs]
        on = oh * lax.rsqrt(jnp.mean(oh * oh, axis=-1, keepdims=True) + RMS_EPS) * gng_ref[...]
        parts.append((on * _silu(z[:, hs])).astype(BF16))
    att_t = att_ref[0].astype(F32)
    attn_t = (att_t * lax.rsqrt(jnp.mean(att_t * att_t, axis=0, keepdims=True) + RMS_EPS) * ang_ref[...]).astype(BF16)
    mixed = _dot(jnp.concatenate(parts, axis=1), wout_ref[:GDN_WIDTH, :]) + _dot_tn(attn_t, wout_ref[GDN_WIDTH:, :])
    gt1 = mod_ref[0, 2:3, :]
    sh2 = mod_ref[0, 3:4, :]
    sc2 = mod_ref[0, 4:5, :]
    x1 = _layer_norm(ALPHA * x_ref[0] + gt1 * mixed) * l1g_ref[...] + l1b_ref[...]
    x1_ref[0] = x1
    h2 = _layer_norm(x1) * (1.0 + sc2) + sh2
    _store_tile_rows(h2_ref, (), 0, h2)
    logits = _dot_f32(h2, wrt_ref[...]) + brt_ref[...]
    lane = lax.broadcasted_iota(I32, logits.shape, 1)
    lane_f = lane.astype(F32)
    is_grp = lane < N_GROUPS
    gmax, gidx = _first_argmax(logits, lane_f, is_grp)
    gsum = jnp.sum(jnp.where(is_grp, jnp.exp(jnp.minimum(logits - gmax, 0.0)), 0.0), axis=-1, keepdims=True)
    grp_p = 1.0 / gsum
    lo = float(N_GROUPS) + float(EXPERTS_PER_GROUP) * gidx
    in_grp = (lane_f >= lo) & (lane_f < lo + float(EXPERTS_PER_GROUP))
    v0, i0 = _first_argmax(logits, lane_f, in_grp)
    v1, i1 = _first_argmax(logits, lane_f, in_grp & (lane_f != i0))
    e1 = jnp.exp(v1 - v0)
    w0 = grp_p / (1.0 + e1)
    w1 = grp_p * e1 / (1.0 + e1)
    route = jnp.where(lane == 0, i0 - float(N_GROUPS),
                      jnp.where(lane == 1, i1 - float(N_GROUPS),
                                jnp.where(lane == 2, w0, jnp.where(lane == 3, w1, 0.0))))
    route_ref[0] = route[:, 0:ROUTE_W]


def _mixer_out(o_f, o_b, z, att, x, mod3, w_out, gng, ang, l1g, l1b, w_rt, b_rt, tm):
    b, s, d = x.shape
    row = lambda w: pl.BlockSpec((1, tm, w), lambda bi, i: (bi, i, 0))
    const = lambda shape: pl.BlockSpec(shape, lambda bi, i: (0,) * len(shape))
    return pl.pallas_call(
        _mix_kernel,
        grid=(b, s // tm),
        in_specs=[row(GDN_WIDTH), row(GDN_WIDTH), row(GDN_WIDTH),
                  pl.BlockSpec((1, ATTN_WIDTH, tm), lambda bi, i: (bi, 0, i)), row(d),
                  pl.BlockSpec((1, 6, d), lambda bi, i: (bi, 0, 0)),
                  const(w_out.shape), const((1, HEAD_DIM)), const((ATTN_WIDTH, 1)),
                  const((1, d)), const((1, d)), const((d, LANES)), const((1, LANES))],
        out_specs=[row(d), pl.BlockSpec((tm * ROW_TILE, LANES), lambda bi, i: (bi * (s // tm) + i, 0)), row(ROUTE_W)],
        out_shape=[jax.ShapeDtypeStruct((b, s, d), F32),
                   jax.ShapeDtypeStruct((b * s * ROW_TILE, LANES), F32),
                   jax.ShapeDtypeStruct((b, s, ROUTE_W), F32)],
        compiler_params=_cparams(("arbitrary", "arbitrary")),
        name="mixer_out",
    )(o_f, o_b, z, att, x, mod3, w_out, gng, ang, l1g, l1b, w_rt, b_rt)


def _rank_kernel(route_ref, rank_ref, cnt_ref, carry_ref):
    @pl.when(pl.program_id(0) == 0)
    def _():
        carry_ref[...] = jnp.zeros_like(carry_ref)

    th = route_ref.shape[0]
    route = route_ref[...]
    lane_f = lax.broadcasted_iota(I32, (th, LANES), 1).astype(F32)
    oh0 = lane_f == route[:, 0:1]
    oh1 = lane_f == route[:, 1:2]
    both = jnp.where(oh0 | oh1, 1.0, 0.0).astype(BF16)
    r = lax.broadcasted_iota(I32, (th, th), 0)
    c = lax.broadcasted_iota(I32, (th, th), 1)
    before = _dot(jnp.where(r > c, 1.0, 0.0).astype(BF16), both) + carry_ref[...]
    rank0 = jnp.sum(jnp.where(oh0, before, 0.0), axis=-1, keepdims=True)
    rank1 = jnp.sum(jnp.where(oh1, before, 0.0), axis=-1, keepdims=True)
    lane8 = lax.broadcasted_iota(I32, (th, ROUTE_W), 1)
    rank_ref[...] = jnp.where(lane8 == 0, rank0, jnp.where(lane8 == 1, rank1, 0.0))
    total = carry_ref[...] + jnp.sum(both.astype(F32), axis=0, keepdims=True)
    carry_ref[...] = total
    cnt_ref[...] = total


def _expert_ranks(route, th):
    t = route.shape[0]
    return pl.pallas_call(
        _rank_kernel,
        grid=(t // th,),
        in_specs=[pl.BlockSpec((th, ROUTE_W), lambda i: (i, 0))],
        out_specs=[pl.BlockSpec((th, ROUTE_W), lambda i: (i, 0)),
                   pl.BlockSpec((1, LANES), lambda i: (0, 0))],
        out_shape=[jax.ShapeDtypeStruct((t, ROUTE_W), F32),
                   jax.ShapeDtypeStruct((1, LANES), F32)],
        scratch_shapes=[pltpu.VMEM((1, LANES), F32)],
        compiler_params=_cparams(("arbitrary",)),
        name="expert_ranks",
    )(route)


def _dest_kernel(route_ref, rank_ref, start_ref, dest_ref):
    th = route_ref.shape[0]
    route = route_ref[...]
    rank = rank_ref[...]
    lane_f = lax.broadcasted_iota(I32, (th, LANES), 1).astype(F32)
    start = start_ref[...]
    d0 = rank[:, 0:1] + jnp.sum(jnp.where(lane_f == route[:, 0:1], start, 0.0), axis=-1, keepdims=True)
    d1 = rank[:, 1:2] + jnp.sum(jnp.where(lane_f == route[:, 1:2], start, 0.0), axis=-1, keepdims=True)
    lane8 = lax.broadcasted_iota(I32, (th, ROUTE_W), 1)
    dest_ref[...] = (jnp.where(lane8 == 0, d0, jnp.where(lane8 == 1, d1, 0.0)) * float(ROW_TILE)).astype(I32)


def _expert_dest(route, rank, start_row, th):
    t = route.shape[0]
    return pl.pallas_call(
        _dest_kernel,
        grid=(t // th,),
        in_specs=[pl.BlockSpec((th, ROUTE_W), lambda i: (i, 0)),
                  pl.BlockSpec((th, ROUTE_W), lambda i: (i, 0)),
                  pl.BlockSpec((1, LANES), lambda i: (0, 0))],
        out_specs=pl.BlockSpec((th, ROUTE_W), lambda i: (i, 0)),
        out_shape=jax.ShapeDtypeStruct((t, ROUTE_W), I32),
        compiler_params=_cparams(("arbitrary",)),
        name="expert_dest",
    )(route, rank, start_row)


def _prefetched_indices(dest_hbm, idx_smem, idx_sem, tile, n_tiles):
    def idx_copy(t, slot):
        return pltpu.make_async_copy(dest_hbm.at[t], idx_smem.at[slot], idx_sem.at[slot])

    slot = tile % 2

    @pl.when(tile == 0)
    def _():
        idx_copy(0, 0).start()

    idx_copy(tile, slot).wait()

    @pl.when(tile + 1 < n_tiles)
    def _():
        idx_copy(tile + 1, 1 - slot).start()

    return slot


_DISPATCH_SLOTS = 3


def _dispatch_kernel(pad_end_ref, nused_ref, dest_hbm, h_hbm, xs_hbm, idx_smem, h_buf, zero_buf,
                     idx_sem, tile_sem, row_sem, zero_sem):
    td = h_buf.shape[1] // ROW_TILE
    blk_rows = zero_buf.shape[0]
    tile = pl.program_id(0)
    n_tiles = pl.num_programs(0)

    @pl.when(tile == 0)
    def _():
        def zero_block(first_row):
            return pltpu.make_async_copy(
                zero_buf, xs_hbm.at[pl.ds(pl.multiple_of(first_row, ROW_TILE), blk_rows)], zero_sem)

        zero_buf[...] = jnp.zeros_like(zero_buf)
        for e in range(N_EXPERTS):
            zero_block(jnp.maximum(pad_end_ref[e] * ROW_TILE - blk_rows, 0)).start()
        for e in range(N_EXPERTS):
            zero_block(0).wait()

        def zero_unused(bi, carry):
            zero_block(bi * blk_rows).start()
            zero_block(0).wait()
            return carry

        lax.fori_loop(nused_ref[0], xs_hbm.shape[0] // blk_rows, zero_unused, 0)

    idx_slot = _prefetched_indices(dest_hbm, idx_smem, idx_sem, tile, n_tiles)
    slot = tile % _DISPATCH_SLOTS
    sem_slot = tile % 2

    def tile_copy(t, sl):
        return pltpu.make_async_copy(h_hbm.at[pl.ds(t * (td * ROW_TILE), td * ROW_TILE)], h_buf.at[sl],
                                     tile_sem.at[sl])

    def row_copy(sl, sem_sl, r, dst_row):
        return pltpu.make_async_copy(h_buf.at[sl, pl.ds(ROW_TILE * r, ROW_TILE)],
                                     xs_hbm.at[pl.ds(pl.multiple_of(dst_row, ROW_TILE), ROW_TILE)], row_sem.at[sem_sl])

    @pl.when(tile == 0)
    def _():
        tile_copy(0, 0).start()

    @pl.when(tile + 1 < n_tiles)
    def _():
        tile_copy(tile + 1, (tile + 1) % _DISPATCH_SLOTS).start()

    tile_copy(tile, slot).wait()
    for n in range(2 * td):
        row_copy(slot, sem_slot, n // 2, idx_smem[idx_slot, n]).start(priority=n % 2)

    @pl.when(tile > 0)
    def _():
        for n in range(2 * td):
            row_copy(slot, 1 - sem_slot, n // 2, 0).wait()

    @pl.when(tile == n_tiles - 1)
    def _():
        for n in range(2 * td):
            row_copy(slot, sem_slot, n // 2, 0).wait()


def _dispatch(pad_end, n_used, dest_tiles, h2, n_blocks, blk, td):
    t = h2.shape[0] // ROW_TILE
    grid_spec = pltpu.PrefetchScalarGridSpec(
        num_scalar_prefetch=2,
        grid=(t // td,),
        in_specs=[pl.BlockSpec(memory_space=pl.ANY), pl.BlockSpec(memory_space=pl.ANY)],
        out_specs=pl.BlockSpec(memory_space=pl.ANY),
        scratch_shapes=[pltpu.SMEM((2, 2 * td), I32), pltpu.VMEM((_DISPATCH_SLOTS, td * ROW_TILE, LANES), F32),
                        pltpu.VMEM((blk * ROW_TILE, LANES), F32),
                        pltpu.SemaphoreType.DMA((2,)), pltpu.SemaphoreType.DMA((_DISPATCH_SLOTS,)),
                        pltpu.SemaphoreType.DMA((2,)), pltpu.SemaphoreType.DMA],
    )
    return pl.pallas_call(
        _dispatch_kernel,
        grid_spec=grid_spec,
        out_shape=jax.ShapeDtypeStruct((n_blocks * blk * ROW_TILE, LANES), F32),
        compiler_params=_cparams(("arbitrary",)),
        name="moe_dispatch",
    )(pad_end, n_used, dest_tiles, h2)


def _expert_kernel(be_ref, nused_ref, xs_ref, w1_ref, w3_ref, w2_ref, ys_ref, w13b_ref, w2b_ref):
    i = pl.program_id(0)
    changed = jnp.logical_or(i == 0, be_ref[i] != be_ref[jnp.maximum(i - 1, 0)])

    @pl.when(jnp.logical_and(changed, i < nused_ref[0]))
    def _():
        w13b_ref[:, :D_EXPERT] = w1_ref[0].astype(BF16)
        w13b_ref[:, D_EXPERT:] = w3_ref[0].astype(BF16)
        w2b_ref[...] = w2_ref[0].astype(BF16)

    @pl.when(i < nused_ref[0])
    def _():
        rows = xs_ref.shape[0] // ROW_TILE
        h13 = _dot(_load_tile_rows(xs_ref, (), 0, rows).astype(BF16), w13b_ref[...])
        hid = _silu(h13[:, :D_EXPERT]) * h13[:, D_EXPERT:]
        _store_tile_rows(ys_ref, (), 0, _dot(hid.astype(BF16), w2b_ref[...]))

    @pl.when(i >= nused_ref[0])
    def _():
        ys_ref[...] = jnp.zeros_like(ys_ref)


def _experts(blk_expert, n_used, xs, w1, w3, w2, blk):
    d = w1.shape[1]
    n_blocks = xs.shape[0] // (blk * ROW_TILE)
    row_map = lambda i, be, nu: (i, 0)
    grid_spec = pltpu.PrefetchScalarGridSpec(
        num_scalar_prefetch=2,
        grid=(n_blocks,),
        in_specs=[pl.BlockSpec((blk * ROW_TILE, LANES), row_map),
                  pl.BlockSpec((1, d, D_EXPERT), lambda i, be, nu: (be[i], 0, 0)),
                  pl.BlockSpec((1, d, D_EXPERT), lambda i, be, nu: (be[i], 0, 0)),
                  pl.BlockSpec((1, D_EXPERT, d), lambda i, be, nu: (be[i], 0, 0))],
        out_specs=pl.BlockSpec((blk * ROW_TILE, LANES), row_map),
        scratch_shapes=[pltpu.VMEM((d, 2 * D_EXPERT), BF16), pltpu.VMEM((D_EXPERT, d), BF16)],
    )
    return pl.pallas_call(
        _expert_kernel,
        grid_spec=grid_spec,
        out_shape=jax.ShapeDtypeStruct(xs.shape, F32),
        compiler_params=_cparams(("arbitrary",)),
        name="moe_experts",
    )(blk_expert, n_used, xs, w1, w3, w2)


def _combine_kernel(dest_hbm, ys_hbm, x1_ref, route_ref, gt2_ref, l2g_ref, l2b_ref, o_ref,
                    idx_smem, buf_ref, idx_sem, row_sem):
    b_i = pl.program_id(0)
    i = pl.program_id(1)
    td = x1_ref.shape[1]
    tile = b_i * pl.num_programs(1) + i
    n_tiles = pl.num_programs(0) * pl.num_programs(1)
    slot = tile % 2

    def idx_copy(t, sl):
        return pltpu.make_async_copy(dest_hbm.at[t], idx_smem.at[sl], idx_sem.at[sl])

    def row_copy(sl, n, src_row):
        return pltpu.make_async_copy(ys_hbm.at[pl.ds(pl.multiple_of(src_row, ROW_TILE), ROW_TILE)],
                                     buf_ref.at[sl, n % 2, pl.ds(ROW_TILE * (n // 2), ROW_TILE)], row_sem.at[sl])

    def start_rows(sl):
        for n in range(2 * td):
            row_copy(sl, n, idx_smem[sl, n]).start(priority=n % 2)

    @pl.when(tile == 0)
    def _():
        idx_copy(0, 0).start()
        idx_copy(0, 0).wait()
        start_rows(0)

        @pl.when(n_tiles > 1)
        def _():
            idx_copy(1, 1).start()

    @pl.when(tile + 1 < n_tiles)
    def _():
        idx_copy(tile + 1, 1 - slot).wait()
        start_rows(1 - slot)

        @pl.when(tile + 2 < n_tiles)
        def _():
            idx_copy(tile + 2, slot).start()

    for n in range(2 * td):
        row_copy(slot, n, 0).wait()
    route = route_ref[0]
    ffn = (_load_tile_rows(buf_ref, (slot, 0), 0, td) * route[:, 2:3]
           + _load_tile_rows(buf_ref, (slot, 1), 0, td) * route[:, 3:4])
    o_ref[0] = _layer_norm(ALPHA * x1_ref[0] + gt2_ref[0] * ffn) * l2g_ref[...] + l2b_ref[...]


def _combine(dest_tiles, ys, x1, route3, gt2, l2g, l2b, td):
    b, s, d = x1.shape
    row = lambda w: pl.BlockSpec((1, td, w), lambda bi, i: (bi, i, 0))
    const = lambda shape: pl.BlockSpec(shape, lambda bi, i: (0,) * len(shape))
    return pl.pallas_call(
        _combine_kernel,
        grid=(b, s // td),
        in_specs=[pl.BlockSpec(memory_space=pl.ANY),
                  pl.BlockSpec(memory_space=pl.ANY),
                  row(d), row(ROUTE_W),
                  pl.BlockSpec((1, 1, d), lambda bi, i: (bi, 0, 0)),
                  const((1, d)), const((1, d))],
        out_specs=row(d),
        out_shape=jax.ShapeDtypeStruct((b, s, d), F32),
        scratch_shapes=[pltpu.SMEM((2, 2 * td), I32), pltpu.VMEM((2, 2, td * ROW_TILE, LANES), F32),
                        pltpu.SemaphoreType.DMA((2,)), pltpu.SemaphoreType.DMA((2,))],
        compiler_params=_cparams(("arbitrary", "arbitrary")),
        name="moe_combine",
    )(dest_tiles, ys, x1, route3, gt2, l2g, l2b)


def _tile(n, pref):
    t = min(n, pref)
    assert n % t == 0, (n, t)
    return t


def _rope_tables(s):
    half = HEAD_DIM // 2
    inv = ROPE_THETA ** (-jnp.arange(0, half, 2, dtype=F32) / half)
    pos = jnp.arange(s)
    row = (pos // GRID_W).astype(F32)[:, None] * inv[None, :]
    col = (pos % GRID_W).astype(F32)[:, None] * inv[None, :]
    cos = jnp.concatenate([jnp.cos(row), jnp.cos(row), jnp.cos(col), jnp.cos(col)], axis=-1)
    sin = jnp.concatenate([-jnp.sin(row), jnp.sin(row), -jnp.sin(col), jnp.sin(col)], axis=-1)
    return cos, sin


def _layer(x, c, w_ada, b_ada, w_in, conv_w, a_log, dt_bias, gdn_norm_g, q_norm_g, k_norm_g, attn_norm_g,
           w_out, ln1_g, ln1_b, w_group, b_group, w_router, b_router, w1, w3, w2, ln2_g, ln2_b):
    b, s, d = x.shape
    t = b * s
    assert s % CHUNK == 0 and s % GRID_W == 0

    c_pad = jnp.pad(c, ((0, (-b) % 8), (0, 0)))
    mod = _adaln_mod(c_pad, w_ada, b_ada.reshape(1, -1), _tile(6 * d, 1536))[:b]
    mod3 = mod.reshape(b, 6, d)

    gq, gk, gv, gz, gab, aq, ak, av = jnp.split(
        w_in, [GDN_WIDTH, 2 * GDN_WIDTH, 3 * GDN_WIDTH, 4 * GDN_WIDTH, 4 * GDN_WIDTH + N_GATES,
               4 * GDN_WIDTH + N_GATES + ATTN_WIDTH, 4 * GDN_WIDTH + N_GATES + ATTN_WIDTH + ATTN_KV_WIDTH], axis=1)
    w_packed = jnp.concatenate([gq, gk, gv, gz, ak, jnp.pad(gab, ((0, 0), (0, LANES - N_GATES)))], axis=1).astype(BF16)
    wqv_t = jnp.concatenate([aq, av], axis=1).T.astype(BF16)
    wab_t = gab.T.astype(BF16)
    cos, sin = _rope_tables(s)

    tm = _tile(s, 512)
    gqkv, z, ab, abt, a_qt, a_k, a_vt = _in_projection(
        x, mod3, w_packed, wqv_t, wab_t, cos, sin, cos.T, sin.T, q_norm_g.reshape(-1, 1), k_norm_g.reshape(1, -1), tm)

    pad_gates = lambda p: jnp.pad(p.reshape(1, -1), ((0, 0), (0, N_GATES - p.size)))
    alog_c = pad_gates(a_log)
    dtb_c = pad_gates(dt_bias)
    tc = _tile(s, 512)
    gq_n, gk_n, gv_n, gb, egc, egl, ea, gcr = _gdn_prep(
        gqkv, conv_w, ab, abt, alog_c, dtb_c, alog_c.reshape(-1, 1), dtb_c.reshape(-1, 1), tc)
    a_m, b_m, a_dg, q_eff, o_in = _gdn_local(gq_n, gk_n, gv_n, gb, egc, egl, ea, gcr, _tile(s, 256))
    o_f, o_b = _gdn_scan(a_m, b_m, a_dg, q_eff, o_in, _tile(s // CHUNK, 8))

    tq, tk = _tile(s, 1024), _tile(s, 4096)
    att_lagged, excess = _attention_lagged(a_qt, a_k, a_vt, tq, tk)
    att = lax.cond(jnp.max(excess) > ATTN_LAG_LIMIT,
                   lambda: _attention(a_qt, a_k, a_vt, tq, tk), lambda: att_lagged)

    w_rt = jnp.pad(jnp.concatenate([w_group, w_router], axis=1), ((0, 0), (0, LANES - N_GROUPS - N_EXPERTS)))
    b_rt = jnp.pad(jnp.concatenate([b_group, b_router]).reshape(1, -1), ((0, 0), (0, LANES - N_GROUPS - N_EXPERTS)))
    x1, h2, route = _mixer_out(o_f, o_b, z, att, x, mod3, w_out.astype(BF16), gdn_norm_g.reshape(1, -1),
                               attn_norm_g.reshape(-1, 1), ln1_g.reshape(1, -1), ln1_b.reshape(1, -1),
                               w_rt, b_rt, _tile(s, 512))

    blk = 512
    th = _tile(t, 1024)
    route2 = route.reshape(t, ROUTE_W)
    rank, counts = _expert_ranks(route2, th)
    counts_i = counts[0, :N_EXPERTS].astype(I32)
    padded = (counts_i + blk - 1) // blk * blk
    pad_end = jnp.cumsum(padded)
    pad_start = pad_end - padded
    n_blocks = -(-(2 * t) // blk) + N_EXPERTS
    blk_pos = jnp.arange(n_blocks, dtype=I32) * blk
    blk_expert = jnp.minimum(jnp.sum((pad_end[None, :] <= blk_pos[:, None]).astype(I32), axis=1), N_EXPERTS - 1)
    n_used = jnp.maximum(pad_end[-1:] // blk, 1).astype(I32)
    start_row = jnp.pad(pad_start.astype(F32).reshape(1, -1), ((0, 0), (0, LANES - N_EXPERTS)))
    dest = _expert_dest(route2, rank, start_row, _tile(t, 4096))

    td = _tile(s, 512)
    dest_tiles = dest[:, 0:2].reshape(t // td, 2 * td)
    assert d == ROW_TILE * LANES
    xs = _dispatch(pad_end.astype(I32), n_used, dest_tiles, h2, n_blocks, blk, td)
    ys = _experts(blk_expert, n_used, xs, w1, w3, w2, blk)
    gt2 = mod3[:, 5:6, :]
    return _combine(dest_tiles, ys, x1, route, gt2, ln2_g.reshape(1, -1), ln2_b.reshape(1, -1), td)


def kernel(x, c, w_ada, b_ada, w_in, conv_w, a_log, dt_bias, gdn_norm_g, q_norm_g, k_norm_g, attn_norm_g,
           w_out, ln1_g, ln1_b, w_group, b_group, w_router, b_router, w1, w3, w2, ln2_g, ln2_b):
    for layer in range(w_ada.shape[0]):
        x = _layer(x, c, w_ada[layer], b_ada[layer], w_in[layer], conv_w[layer], a_log[layer], dt_bias[layer],
                   gdn_norm_g[layer], q_norm_g[layer], k_norm_g[layer], attn_norm_g[layer], w_out[layer],
                   ln1_g[layer], ln1_b[layer], w_group[layer], b_group[layer], w_router[layer], b_router[layer],
                   w1[layer], w3[layer], w2[layer], ln2_g[layer], ln2_b[layer])
    return x
```
